```python
import jax, jax.numpy as jnp
from jax import lax
import numpy as np

D_MODEL = 1024
BATCH = 4
SEQ = 4096
DEPTH = 1
DEC_BATCH = 32
DEC_SEQ = 8
PAST_LEN = 16384
PAGE_SIZE = 128

D_PLE = 256
NSA_HEADS = 8
NSA_GROUPS = 2
HD = 64
HPG = NSA_HEADS // NSA_GROUPS
W_A = NSA_HEADS * HD
KV_W = NSA_GROUPS * HD
L_CMP = 32
D_CMP = 16
CMP_HID = 128
L_SEL = 64
N_SEL = 16
WIN = 512
Q_BLOCK = 128
FORCE_SCORE = 1e4
NEG_INF = -1e30
R_HEADS = 8
R_HD = 64
W_R = R_HEADS * R_HD
R_W = 64
R_A = 64
C_R = 3 * W_R + R_W + R_A
GN_EPS = 64e-5
ROPE_THETA = 10000.0
NORM_EPS = 1e-6
SPLIT_SIZES = (W_A, 6 * KV_W, 3 * NSA_HEADS, W_A, C_R, W_R, 2 * D_MODEL)
C_IN = sum(SPLIT_SIZES)
SPLIT_OFFSETS = tuple(int(v) for v in np.cumsum(SPLIT_SIZES)[:-1])

kernel_name = 'nsa_rwkv7_hybrid_step'


def rms_norm(x, g):
    xf = x.astype(jnp.float32)
    y = xf * lax.rsqrt(jnp.mean(xf * xf, axis=-1, keepdims=True) + NORM_EPS)
    return (y * g.astype(jnp.float32)).astype(x.dtype)


def rope(x, pos):
    half = HD // 2
    inv = ROPE_THETA ** (-jnp.arange(half, dtype=jnp.float32) / half)
    ang = pos.astype(jnp.float32)[:, None] * inv[None, :]
    shape = (1, pos.shape[0]) + (1,) * (x.ndim - 3) + (half,)
    cos = jnp.cos(ang).reshape(shape)
    sin = jnp.sin(ang).reshape(shape)
    xf = x.astype(jnp.float32)
    x1, x2 = xf[..., :half], xf[..., half:]
    return jnp.concatenate([x1 * cos - x2 * sin, x2 * cos + x1 * sin], axis=-1).astype(x.dtype)


def masked_softmax(s, mask):
    s = jnp.where(mask, s.astype(jnp.float32), NEG_INF)
    return jax.nn.softmax(s, axis=-1) * mask


def compress_rows(raw, pe, w1, b1, w2, b2, n_blocks):
    b, t = raw.shape[:2]
    n_chunk = -(-t // D_CMP)
    raw = jnp.pad(raw, ((0, 0), (0, n_chunk * D_CMP - t), (0, 0), (0, 0)))
    chunks = raw.reshape(b, n_chunk, D_CMP, NSA_GROUPS, HD)
    lo = jnp.einsum('bcjgd,jdf->bcgf', chunks, w1[:D_CMP])
    hi = jnp.einsum('bcjgd,jdf->bcgf', chunks, w1[D_CMP:])
    pos_term = jnp.einsum('jd,jdf->f', pe, w1)
    hid = (lo[:, :-1] + hi[:, 1:])[:, :n_blocks] + pos_term + b1
    return jnp.einsum('bcgf,fd->bcgd', jax.nn.silu(hid), w2) + b2


def nsa_context(cmp_kv, sel_kv, cmp_pe, cmp_w1, cmp_b1, cmp_w2, cmp_b2):
    b, t = cmp_kv.shape[:2]
    n_cmp = (t - L_CMP) // D_CMP + 1
    kc = compress_rows(cmp_kv[:, :, 0], cmp_pe[0], cmp_w1[0], cmp_b1[0], cmp_w2[0], cmp_b2[0], n_cmp)
    vc = compress_rows(cmp_kv[:, :, 1], cmp_pe[1], cmp_w1[1], cmp_b1[1], cmp_w2[1], cmp_b2[1], n_cmp)
    n_sb = -(-t // L_SEL)
    sel = jnp.pad(sel_kv, ((0, 0), (0, n_sb * L_SEL - t), (0, 0), (0, 0), (0, 0)))
    sel = sel.reshape(b, n_sb, L_SEL, 2, NSA_GROUPS, HD).transpose(3, 0, 4, 1, 2, 5)
    c_start = np.arange(n_cmp) * D_CMP
    s_start = np.arange(n_sb) * L_SEL
    overlap = (c_start[:, None] <= s_start[None, :] + L_SEL - 1) & (c_start[:, None] + L_CMP - 1 >= s_start[None, :])
    c_end = jnp.asarray(c_start + L_CMP - 1, dtype=jnp.int32)
    return kc, vc, c_end, sel[0], sel[1], jnp.asarray(overlap, dtype=jnp.float32)


def nsa_query_block(q, gates, qpos, kc, vc, c_end, ks, vs, overlap, kw, vw, kw_pos):
    b, nq = q.shape[:2]
    qg = q.reshape(b, nq, NSA_GROUPS, HPG, HD) * (HD ** -0.5)
    s_c = jnp.einsum('bqghd,bcgd->bqghc', qg, kc)
    p_c = masked_softmax(s_c, (c_end[None, :] <= qpos[:, None])[None, :, None, None, :])
    o_c = jnp.einsum('bqghc,bcgd->bqghd', p_c, vc)
    n_sb = ks.shape[2]
    n_sel = min(N_SEL, n_sb)
    imp = jnp.einsum('bqgc,cn->bqgn', p_c.sum(axis=3), overlap)
    blk = jnp.arange(n_sb)
    cur = (qpos // L_SEL)[:, None]
    forced = (blk[None, :] == 0) | (blk[None, :] == cur) | (blk[None, :] == cur - 1)
    future = blk[None, :] * L_SEL > qpos[:, None]
    imp = jnp.where(forced[None, :, None, :], FORCE_SCORE, imp)
    imp = jnp.where(future[None, :, None, :], -1.0, imp)
    _, idx = lax.top_k(imp, n_sel)
    bi = jnp.arange(b)[:, None, None, None]
    gi = jnp.arange(NSA_GROUPS)[None, None, :, None]
    k_sel = ks[bi, gi, idx]
    v_sel = vs[bi, gi, idx]
    sel_pos = idx[..., None] * L_SEL + jnp.arange(L_SEL)
    m_s = (sel_pos <= qpos[None, :, None, None, None]).reshape(b, nq, NSA_GROUPS, 1, n_sel * L_SEL)
    s_s = jnp.einsum('bqghd,bqgnld->bqghnl', qg, k_sel).reshape(b, nq, NSA_GROUPS, HPG, n_sel * L_SEL)
    p_s = masked_softmax(s_s, m_s).reshape(b, nq, NSA_GROUPS, HPG, n_sel, L_SEL)
    o_s = jnp.einsum('bqghnl,bqgnld->bqghd', p_s, v_sel)
    s_w = jnp.einsum('bqghd,bkgd->bqghk', qg, kw)
    dist = qpos[:, None] - kw_pos[None, :]
    m_w = (dist >= 0) & (dist < WIN) & (kw_pos[None, :] >= 0)
    p_w = masked_softmax(s_w, m_w[None, :, None, None, :])
    o_w = jnp.einsum('bqghk,bkgd->bqghd', p_w, vw)
    g = gates.reshape(b, nq, NSA_GROUPS, HPG, 3)
    o = g[..., 0:1] * o_c + g[..., 1:2] * o_s + g[..., 2:3] * o_w
    return o.reshape(b, nq, W_A).astype(q.dtype)


def rwkv_time_mix(rz, shift0, state0, mu, w0, w_up, a0, a_up, k_k, k_a, r_k, gn_g, gn_b):
    b, t = rz.shape[:2]
    z = rz.astype(jnp.float32)
    prev = jnp.concatenate([shift0.astype(jnp.float32)[:, None], z[:, :-1]], axis=1)
    zs = z + (prev - z) * mu
    r, k, v, wd, ad = jnp.split(zs, (W_R, 2 * W_R, 3 * W_R, 3 * W_R + R_W), axis=-1)
    w = -jax.nn.softplus(-(w0 + jnp.tanh(wd) @ w_up)) - 0.5
    decay = jnp.exp(-jnp.exp(w))
    a = jax.nn.sigmoid(a0 + ad @ a_up)
    heads = lambda u: u.reshape(b, t, R_HEADS, R_HD)
    kk = heads(k * k_k)
    kk = kk / jnp.maximum(jnp.linalg.norm(kk, axis=-1, keepdims=True), 1e-12)
    k = k * (1.0 + (a - 1.0) * k_a)
    r, k, v, decay, a = map(heads, (r, k, v, decay, a))

    def step(s, inp):
        r_t, d_t, k_t, v_t, kk_t, a_t = inp
        sa = jnp.einsum('bhvk,bhk->bhv', s, -kk_t)
        s = s * d_t[:, :, None, :] + sa[..., None] * (kk_t * a_t)[:, :, None, :] + v_t[..., None] * k_t[:, :, None, :]
        return s, jnp.einsum('bhvk,bhk->bhv', s, r_t)

    xs = tuple(jnp.moveaxis(u, 1, 0) for u in (r, decay, k, v, kk, a))
    state, y = lax.scan(step, state0.astype(jnp.float32), xs)
    y = jnp.moveaxis(y, 0, 1)
    mean = jnp.mean(y, axis=-1, keepdims=True)
    var = jnp.mean(jnp.square(y - mean), axis=-1, keepdims=True)
    y = ((y - mean) * lax.rsqrt(var + GN_EPS)).reshape(b, t, W_R) * gn_g + gn_b
    y = y + (jnp.sum(r * k * r_k, axis=-1, keepdims=True) * v).reshape(b, t, W_R)
    return y.astype(rz.dtype), state, z[:, -1]


def mixer_inputs(x, pos, norm_g, w_in):
    b, t = x.shape[:2]
    h = rms_norm(x, norm_g)
    z = jnp.einsum('btd,dc->btc', h, w_in)
    q, kv, nsa_g, a_gate, rz, r_gate, merge = jnp.split(z, SPLIT_OFFSETS, axis=-1)
    q = rope(q.reshape(b, t, NSA_HEADS, HD), pos)
    kv = kv.reshape(b, t, 3, 2, NSA_GROUPS, HD)
    kv = jnp.stack([rope(kv[:, :, :, 0], pos), kv[:, :, :, 1]], axis=3)
    gates = jax.nn.sigmoid(nsa_g.astype(jnp.float32)).reshape(b, t, NSA_HEADS, 3)
    return q, kv, gates, a_gate, rz, r_gate, merge


def mixer_outputs(x, o_a, o_b, a_gate, r_gate, merge, p_in, w_pa, w_pb, w_out, ple_norm_g, w_ple_gate, w_ple_proj):
    y_a = jnp.einsum('btc,cd->btd', o_a * jax.nn.silu(a_gate), w_pa)
    y_b = jnp.einsum('btc,cd->btd', o_b * jax.nn.silu(r_gate), w_pb)
    g_a, g_b = jnp.split(jax.nn.sigmoid(merge), 2, axis=-1)
    x = x + jnp.einsum('btd,de->bte', g_a * y_a + g_b * y_b, w_out).astype(x.dtype)
    gate = jax.nn.sigmoid(jnp.einsum('btd,de->bte', rms_norm(x, ple_norm_g), w_ple_gate))
    return x + (gate * jnp.einsum('btp,pd->btd', p_in, w_ple_proj)).astype(x.dtype)


def setup_inputs(seed: int = 0) -> dict:
    key = jax.random.key(seed)
    ks = jax.random.split(key, 40)
    n_pages = PAST_LEN // PAGE_SIZE
    n_used = DEC_BATCH * n_pages
    n_pool = n_used + max(1, n_used // 4)
    win_buf = min(WIN, PAST_LEN)
    nrm = lambda k, shape, s: s * jax.random.normal(k, shape, jnp.float32)
    uni = lambda k, shape, lo, hi: jax.random.uniform(k, shape, jnp.float32, lo, hi)
    page_table = jax.random.permutation(ks[9], n_pool)[:n_used].reshape(DEC_BATCH, n_pages).astype(jnp.int32)
    return {
        'x_prompt': nrm(ks[0], (BATCH, SEQ, D_MODEL), 1.0),
        'x_sample': nrm(ks[1], (DEC_BATCH, DEC_SEQ, D_MODEL), 1.0),
        'p_prompt': nrm(ks[2], (DEPTH, BATCH, SEQ, D_PLE), 1.0),
        'p_sample': nrm(ks[3], (DEPTH, DEC_BATCH, DEC_SEQ, D_PLE), 1.0),
        'cache_cmp_kv': nrm(ks[4], (DEPTH, n_pool, PAGE_SIZE, 2, NSA_GROUPS, HD), 1.0),
        'cache_sel_kv': nrm(ks[5], (DEPTH, n_pool, PAGE_SIZE, 2, NSA_GROUPS, HD), 1.0),
        'cache_win_kv': nrm(ks[6], (DEPTH, DEC_BATCH, win_buf, 2, NSA_GROUPS, HD), 1.0),
        'state_wkv': nrm(ks[7], (DEPTH, DEC_BATCH, R_HEADS, R_HD, R_HD), 0.1),
        'state_shift': nrm(ks[8], (DEPTH, DEC_BATCH, C_R), 1.0),
        'page_table': page_table,
        'norm_g': 1.0 + nrm(ks[10], (DEPTH, D_MODEL), 0.02),
        'w_in': nrm(ks[11], (DEPTH, D_MODEL, C_IN), D_MODEL ** -0.5),
        'cmp_pe': nrm(ks[12], (DEPTH, 2, L_CMP, HD), 0.02),
        'cmp_w1': nrm(ks[13], (DEPTH, 2, L_CMP, HD, CMP_HID), (L_CMP * HD) ** -0.5),
        'cmp_b1': nrm(ks[14], (DEPTH, 2, CMP_HID), 0.01),
        'cmp_w2': nrm(ks[15], (DEPTH, 2, CMP_HID, HD), CMP_HID ** -0.5),
        'cmp_b2': nrm(ks[16], (DEPTH, 2, HD), 0.01),
        'w_pa': nrm(ks[17], (DEPTH, W_A, D_MODEL), W_A ** -0.5),
        'rwkv_mu': uni(ks[18], (DEPTH, C_R), 0.0, 1.0),
        'rwkv_w0': uni(ks[19], (DEPTH, W_R), -6.0, -1.0),
        'rwkv_w_up': nrm(ks[20], (DEPTH, R_W, W_R), 0.1 * R_W ** -0.5),
        'rwkv_a0': nrm(ks[21], (DEPTH, W_R), 0.1),
        'rwkv_a_up': nrm(ks[22], (DEPTH, R_A, W_R), 0.1 * R_A ** -0.5),
        'rwkv_k_k': 0.85 + nrm(ks[23], (DEPTH, W_R), 0.05),
        'rwkv_k_a': 1.0 + nrm(ks[24], (DEPTH, W_R), 0.05),
        'rwkv_r_k': nrm(ks[25], (DEPTH, R_HEADS, R_HD), 0.1),
        'rwkv_gn_g': 1.0 + nrm(ks[26], (DEPTH, W_R), 0.02),
        'rwkv_gn_b': nrm(ks[27], (DEPTH, W_R), 0.01),
        'w_pb': nrm(ks[28], (DEPTH, W_R, D_MODEL), W_R ** -0.5),
        'w_out': nrm(ks[29], (DEPTH, D_MODEL, D_MODEL), D_MODEL ** -0.5),
        'ple_norm_g': 1.0 + nrm(ks[30], (DEPTH, D_MODEL), 0.02),
        'w_ple_gate': nrm(ks[31], (DEPTH, D_MODEL, D_MODEL), D_MODEL ** -0.5),
        'w_ple_proj': nrm(ks[32], (DEPTH, D_PLE, D_MODEL), D_PLE ** -0.5),
        'final_norm_g': 1.0 + nrm(ks[33], (D_MODEL,), 0.02),
    }


def reference(x_prompt, x_sample, p_prompt, p_sample, cache_cmp_kv, cache_sel_kv, cache_win_kv, state_wkv, state_shift,
              page_table, norm_g, w_in, cmp_pe, cmp_w1, cmp_b1, cmp_w2, cmp_b2, w_pa, rwkv_mu, rwkv_w0, rwkv_w_up,
              rwkv_a0, rwkv_a_up, rwkv_k_k, rwkv_k_a, rwkv_r_k, rwkv_gn_g, rwkv_gn_b, w_pb, w_out, ple_norm_g,
              w_ple_gate, w_ple_proj, final_norm_g):
    b, s_len = x_prompt.shape[:2]
    db, ds = x_sample.shape[:2]
    n_pages = page_table.shape[1]
    past = n_pages * PAGE_SIZE
    win_buf = cache_win_kv.shape[2]
    win_p = min(WIN, s_len)
    pos_p = jnp.arange(s_len, dtype=jnp.int32)
    pos_s = past + jnp.arange(ds, dtype=jnp.int32)
    kw_pos_p = jnp.concatenate([jnp.full((WIN,), -1, jnp.int32), pos_p])
    kw_pos_s = past - win_buf + jnp.arange(win_buf + ds, dtype=jnp.int32)
    xp, xs = x_prompt, x_sample
    cmp_p, cmp_s, sel_p, sel_s, win_p_l, win_s_l = [], [], [], [], [], []
    wkv_p, wkv_s, sh_p, sh_s = [], [], [], []
    for i in range(DEPTH):
        cmp_w = (cmp_pe[i], cmp_w1[i], cmp_b1[i], cmp_w2[i], cmp_b2[i])
        rw = (rwkv_mu[i], rwkv_w0[i], rwkv_w_up[i], rwkv_a0[i], rwkv_a_up[i], rwkv_k_k[i], rwkv_k_a[i],
              rwkv_r_k[i], rwkv_gn_g[i], rwkv_gn_b[i])
        out_w = (w_pa[i], w_pb[i], w_out[i], ple_norm_g[i], w_ple_gate[i], w_ple_proj[i])

        q, kv, gates, a_gate, rz, r_gate, merge = mixer_inputs(xp, pos_p, norm_g[i], w_in[i])
        kc, vc, c_end, k_s, v_s, ov = nsa_context(kv[:, :, 0], kv[:, :, 1], *cmp_w)
        kw = jnp.pad(kv[:, :, 2], ((0, 0), (WIN, 0), (0, 0), (0, 0), (0, 0)))

        def block(j):
            s0 = j * Q_BLOCK
            qb = lax.dynamic_slice_in_dim(q, s0, Q_BLOCK, axis=1)
            gb = lax.dynamic_slice_in_dim(gates, s0, Q_BLOCK, axis=1)
            pb = lax.dynamic_slice_in_dim(pos_p, s0, Q_BLOCK)
            wb = lax.dynamic_slice_in_dim(kw, s0, Q_BLOCK + WIN, axis=1)
            wp = lax.dynamic_slice_in_dim(kw_pos_p, s0, Q_BLOCK + WIN)
            return nsa_query_block(qb, gb, pb, kc, vc, c_end, k_s, v_s, ov, wb[:, :, 0], wb[:, :, 1], wp)

        o_a = lax.map(block, jnp.arange(s_len // Q_BLOCK))
        o_a = jnp.moveaxis(o_a, 0, 1).reshape(b, s_len, W_A)
        o_b, st_p, shift_p = rwkv_time_mix(rz, jnp.zeros((b, C_R), rz.dtype),
                                           jnp.zeros((b, R_HEADS, R_HD, R_HD), jnp.float32), *rw)
        xp = mixer_outputs(xp, o_a, o_b, a_gate, r_gate, merge, p_prompt[i], *out_w)
        cmp_p.append(kv[:, :, 0].astype(cache_cmp_kv.dtype))
        sel_p.append(kv[:, :, 1].astype(cache_sel_kv.dtype))
        win_p_l.append(kv[:, s_len - win_p:, 2].astype(cache_win_kv.dtype))
        wkv_p.append(st_p.astype(state_wkv.dtype))
        sh_p.append(shift_p.astype(state_shift.dtype))

        q, kv, gates, a_gate, rz, r_gate, merge = mixer_inputs(xs, pos_s, norm_g[i], w_in[i])
        past_cmp = cache_cmp_kv[i][page_table].reshape(db, past, 2, NSA_GROUPS, HD)
        past_sel = cache_sel_kv[i][page_table].reshape(db, past, 2, NSA_GROUPS, HD)
        full_cmp = jnp.concatenate([past_cmp, kv[:, :, 0].astype(past_cmp.dtype)], axis=1)
        full_sel = jnp.concatenate([past_sel, kv[:, :, 1].astype(past_sel.dtype)], axis=1)
        kc, vc, c_end, k_s, v_s, ov = nsa_context(full_cmp, full_sel, *cmp_w)
        win = jnp.concatenate([cache_win_kv[i], kv[:, :, 2].astype(cache_win_kv.dtype)], axis=1)
        o_a = nsa_query_block(q, gates, pos_s, kc, vc, c_end, k_s, v_s, ov, win[:, :, 0], win[:, :, 1], kw_pos_s)
        o_b, st_s, shift_s = rwkv_time_mix(rz, state_shift[i], state_wkv[i], *rw)
        xs = mixer_outputs(xs, o_a, o_b, a_gate, r_gate, merge, p_sample[i], *out_w)
        cmp_s.append(kv[:, :, 0].astype(cache_cmp_kv.dtype))
        sel_s.append(kv[:, :, 1].astype(cache_sel_kv.dtype))
        win_s_l.append(win[:, win.shape[1] - win_buf:])
        wkv_s.append(st_s.astype(state_wkv.dtype))
        sh_s.append(shift_s.astype(state_shift.dtype))

    y_prompt = rms_norm(xp, final_norm_g)
    y_sample = rms_norm(xs, final_norm_g)
    return (y_prompt, y_sample, jnp.stack(cmp_p), jnp.stack(cmp_s), jnp.stack(sel_p), jnp.stack(sel_s),
            jnp.stack(win_p_l), jnp.stack(win_s_l), jnp.stack(wkv_p), jnp.stack(wkv_s), jnp.stack(sh_p), jnp.stack(sh_s))
```

```python
import functools

import numpy as np
import jax
import jax.numpy as jnp
from jax import lax
from jax.experimental import pallas as pl
from jax.experimental.pallas import tpu as pltpu

F32 = jnp.float32
BF16 = jnp.bfloat16

D_MODEL = 1024
D_PLE = 256
NSA_HEADS = 8
NSA_GROUPS = 2
HD = 64
HPG = NSA_HEADS // NSA_GROUPS
W_A = NSA_HEADS * HD
KV_W = NSA_GROUPS * HD
L_CMP = 32
D_CMP = 16
CMP_HID = 128
L_SEL = 64
N_SEL = 16
WIN = 512
FORCE_SCORE = 1e4
NEG_INF = -1e30
R_HEADS = 8
R_HD = 64
W_R = R_HEADS * R_HD
R_W = 64
R_A = 64
C_R = 3 * W_R + R_W + R_A
GN_EPS = 64e-5
ROPE_THETA = 10000.0
NORM_EPS = 1e-6
PAGE_SIZE = 128

LANES = 128
CHUNKS_PER_PAGE = PAGE_SIZE // D_CMP
CHUNK_W = D_CMP * 2 * KV_W
VMEM_LIMIT = 56 * 1024 * 1024

C_Q = 0
C_KV = 512
C_AG = 1280
C_RZ = 1792
C_RG = 3456
C_MG = 3968
C_NG = 6016
C_PACK = 6144

NN = ((1,), (0,))
NT = ((1,), (1,))
TN = ((0,), (0,))


def _dg(a, b, dims=NN):
    return lax.dot_general(a, b, (dims, ((), ())), preferred_element_type=F32)


def _split(x):
    hi = x.astype(BF16)
    lo = (x - hi.astype(F32)).astype(BF16)
    return hi, lo


def _hdot_s(a, b, dims=NN):
    (ah, al), (bh, bl) = a, b
    return _dg(ah, bh, dims) + (_dg(ah, bl, dims) + _dg(al, bh, dims))


def _hdot(a, b, dims=NN):
    return _hdot_s(_split(a), _split(b), dims)


def _sigmoid(x):
    return 1.0 / (1.0 + jnp.exp(-x))


def _silu(x):
    return x * _sigmoid(x)


def _rms(x, g):
    return x * lax.rsqrt(jnp.mean(x * x, axis=-1, keepdims=True) + NORM_EPS) * g


def _const_spec(shape):
    zeros = (0,) * len(shape)
    return pl.BlockSpec(shape, lambda *_: zeros)


def _rope128(z, cos, sin, first):
    partner = jnp.where(first, pltpu.roll(z, LANES - HD // 2, 1), pltpu.roll(z, HD // 2, 1))
    return z * cos + partner * sin


def _proj_kernel(x_ref, g_ref, w_ref, cos_ref, sin_ref,
                 q_ref, cmp_ref, sel_ref, win_ref, gate_ref, sa_ref, rz_ref, sr_ref, mg_ref):
    hb = _rms(x_ref[...], g_ref[...]).astype(BF16)
    cos = cos_ref[...]
    sin = sin_ref[...]
    lane = lax.broadcasted_iota(jnp.int32, cos.shape, 1)
    first = (lane % HD) < (HD // 2)

    def mm(a, b):
        return jnp.dot(hb, w_ref[:, a:b], preferred_element_type=F32)

    zq = mm(C_Q, C_Q + W_A)
    for c in range(W_A // LANES):
        q_ref[:, c * LANES:(c + 1) * LANES] = _rope128(zq[:, c * LANES:(c + 1) * LANES], cos, sin, first)
    for i, ref in enumerate((cmp_ref, sel_ref, win_ref)):
        zkv = mm(C_KV + 2 * KV_W * i, C_KV + 2 * KV_W * (i + 1))
        ref[:, 0:KV_W] = _rope128(zkv[:, 0:KV_W], cos, sin, first)
        ref[:, KV_W:2 * KV_W] = zkv[:, KV_W:2 * KV_W]
    sa_ref[...] = _silu(mm(C_AG, C_AG + W_A))
    rz_ref[...] = mm(C_RZ, C_RZ + C_R)
    sr_ref[...] = _silu(mm(C_RG, C_RG + W_R))
    mg_ref[...] = _sigmoid(mm(C_MG, C_MG + 2 * D_MODEL))
    gate_ref[...] = _sigmoid(mm(C_NG, C_PACK))


def _project(x2d, norm_g, w_pack, cos_t, sin_t, tm):
    n = x2d.shape[0]
    n_pos_blocks = cos_t.shape[0] // tm
    widths = (W_A, 2 * KV_W, 2 * KV_W, 2 * KV_W, LANES, W_A, C_R, W_R, 2 * D_MODEL)
    row = lambda w: pl.BlockSpec((tm, w), lambda i: (i, 0))
    pos = pl.BlockSpec((tm, LANES), lambda i: (i % n_pos_blocks, 0))
    return pl.pallas_call(
        _proj_kernel,
        grid=(n // tm,),
        in_specs=[row(D_MODEL), _const_spec((1, D_MODEL)), _const_spec((D_MODEL, C_PACK)), pos, pos],
        out_specs=[row(w) for w in widths],
        out_shape=[jax.ShapeDtypeStruct((n, w), F32) for w in widths],
        compiler_params=pltpu.CompilerParams(dimension_semantics=("parallel",), vmem_limit_bytes=VMEM_LIMIT),
        name="project",
    )(x2d, norm_g, w_pack, cos_t, sin_t)


def _merge_kernel(x_ref, oa_ref, ob_ref, sa_ref, sr_ref, mg_ref, p_ref,
                  wpa_ref, wpb_ref, wout_ref, pg_ref, wgate_ref, wproj_ref, fg_ref, y_ref):
    bdot = lambda a, w_ref: jnp.dot(a.astype(BF16), w_ref[...], preferred_element_type=F32)
    ya = bdot(oa_ref[...] * sa_ref[...], wpa_ref)
    yb = bdot(ob_ref[...] * sr_ref[...], wpb_ref)
    m = mg_ref[:, 0:D_MODEL] * ya + mg_ref[:, D_MODEL:2 * D_MODEL] * yb
    x2 = x_ref[...] + bdot(m, wout_ref)
    gate = _sigmoid(bdot(_rms(x2, pg_ref[...]), wgate_ref))
    x3 = x2 + gate * bdot(p_ref[...], wproj_ref)
    y_ref[...] = _rms(x3, fg_ref[...])


def _merge(x2d, oa, ob, sa, sr, mg, p2d, wpa, wpb, wout, pg, wgate, wproj, fg, tm):
    n = x2d.shape[0]
    row = lambda w: pl.BlockSpec((tm, w), lambda i: (i, 0))
    return pl.pallas_call(
        _merge_kernel,
        grid=(n // tm,),
        in_specs=[row(D_MODEL), row(W_A), row(W_R), row(W_A), row(W_R), row(2 * D_MODEL), row(D_PLE),
                  _const_spec((W_A, D_MODEL)), _const_spec((W_R, D_MODEL)), _const_spec((D_MODEL, D_MODEL)),
                  _const_spec((1, D_MODEL)), _const_spec((D_MODEL, D_MODEL)), _const_spec((D_PLE, D_MODEL)),
                  _const_spec((1, D_MODEL))],
        out_specs=row(D_MODEL),
        out_shape=jax.ShapeDtypeStruct((n, D_MODEL), F32),
        compiler_params=pltpu.CompilerParams(dimension_semantics=("parallel",), vmem_limit_bytes=VMEM_LIMIT),
        name="merge",
    )(x2d, oa, ob, sa, sr, mg, p2d, wpa, wpb, wout, pg, wgate, wproj, fg)


def _page_copies(pt_ref, pool_ref, buf, sem, b, t, slot, n_pages):
    return [pltpu.make_async_copy(pool_ref.at[pt_ref[b, t * n_pages + p]], buf.at[slot, p], sem.at[slot])
            for p in range(n_pages)]


def _paged_tile(pt_ref, pool_ref, buf, sem, n_pages):
    b, t = pl.program_id(0), pl.program_id(1)
    nt = pl.num_programs(1)
    step = b * nt + t
    slot = step % 2

    @pl.when(step == 0)
    def _():
        for cp in _page_copies(pt_ref, pool_ref, buf, sem, b, t, slot, n_pages):
            cp.start()

    nxt = step + 1

    @pl.when(nxt < pl.num_programs(0) * nt)
    def _():
        for cp in _page_copies(pt_ref, pool_ref, buf, sem, nxt // nt, nxt % nt, 1 - slot, n_pages):
            cp.start()

    for cp in _page_copies(pt_ref, pool_ref, buf, sem, b, t, slot, n_pages):
        cp.wait()
    return slot


def _compress_kernel(pt_ref, pool_ref, w1_ref, pe_ref, b1_ref, w2_ref, b2_ref, out_ref, buf, sem, carry, *, n_pages):
    slot = _paged_tile(pt_ref, pool_ref, buf, sem, n_pages)
    rows = n_pages * CHUNKS_PER_PAGE
    half = 2 * NSA_GROUPS * CMP_HID

    @pl.when(pl.program_id(1) == 0)
    def _():
        carry[...] = jnp.zeros_like(carry)

    x = buf[slot].reshape(rows, CHUNK_W).astype(BF16)
    z = jnp.dot(x, w1_ref[...], preferred_element_type=F32)
    zp = jnp.dot(pe_ref[...].astype(BF16), w1_ref[...], preferred_element_type=F32)
    pos = zp[0:1, 0:half] + zp[1:2, half:2 * half]
    lo = z[:, 0:half]
    ridx = lax.broadcasted_iota(jnp.int32, (rows, half), 0)
    lo_prev = jnp.where(ridx == 0, carry[...], pltpu.roll(lo, 1, 0))
    carry[...] = lo[rows - 1:rows]
    hid = lo_prev + z[:, half:2 * half] + pos + b1_ref[...]
    out_ref[0] = jnp.dot(_silu(hid).astype(BF16), w2_ref[...], preferred_element_type=F32) + b2_ref[...]


def _compress(page_table, pool, w1big, pe2, b1big, w2big, b2big, n_pages):
    nb, pages_per_seq = page_table.shape
    nt = pages_per_seq // n_pages
    rows = n_pages * CHUNKS_PER_PAGE
    half = 2 * NSA_GROUPS * CMP_HID
    cst = lambda shape: pl.BlockSpec(shape, lambda b, t, pt: (0,) * len(shape))
    return pl.pallas_call(
        functools.partial(_compress_kernel, n_pages=n_pages),
        grid_spec=pltpu.PrefetchScalarGridSpec(
            num_scalar_prefetch=1,
            grid=(nb, nt),
            in_specs=[pl.BlockSpec(memory_space=pl.ANY), cst((CHUNK_W, 2 * half)), cst((8, CHUNK_W)),
                      cst((1, half)), cst((half, 2 * KV_W)), cst((1, 2 * KV_W))],
            out_specs=pl.BlockSpec((1, rows, 2 * KV_W), lambda b, t, pt: (b, t, 0)),
            scratch_shapes=[pltpu.VMEM((2, n_pages, CHUNKS_PER_PAGE, CHUNK_W), F32),
                            pltpu.SemaphoreType.DMA((2,)),
                            pltpu.VMEM((1, half), F32)]),
        out_shape=jax.ShapeDtypeStruct((nb, nt * rows, 2 * KV_W), F32),
        compiler_params=pltpu.CompilerParams(dimension_semantics=("arbitrary", "arbitrary"),
                                             vmem_limit_bytes=VMEM_LIMIT),
        name="compress",
    )(page_table, pool, w1big, pe2, b1big, w2big, b2big)


def _compress_weights(cmp_pe, cmp_w1, cmp_b1, cmp_w2, cmp_b2):
    eye_kv = jnp.eye(2, dtype=F32)
    eye_g = jnp.eye(NSA_GROUPS, dtype=F32)
    w1 = cmp_w1.reshape(2, 2, D_CMP, HD, CMP_HID)
    w1big = jnp.einsum('kljdf,kK,gG->jkgdlKGf', w1, eye_kv, eye_g).reshape(CHUNK_W, 2 * 2 * NSA_GROUPS * CMP_HID)
    pe = cmp_pe.reshape(2, 2, D_CMP, HD)
    pe2 = jnp.broadcast_to(jnp.transpose(pe, (1, 2, 0, 3))[:, :, :, None, :], (2, D_CMP, 2, NSA_GROUPS, HD))
    pe2 = jnp.pad(pe2.reshape(2, CHUNK_W), ((0, 6), (0, 0)))
    b1big = jnp.broadcast_to(cmp_b1[:, None, :], (2, NSA_GROUPS, CMP_HID)).reshape(1, -1)
    w2big = jnp.einsum('kfd,kK,gG->kgfKGd', cmp_w2, eye_kv, eye_g).reshape(2 * NSA_GROUPS * CMP_HID, 2 * KV_W)
    b2big = jnp.broadcast_to(cmp_b2[:, None, :], (2, NSA_GROUPS, HD)).reshape(1, -1)
    return w1big.astype(BF16), pe2, b1big, w2big.astype(BF16), b2big


def _overlap_matrix(n_rows, n_cols):
    c_start = (np.arange(n_rows) - 1) * D_CMP
    s_start = np.arange(n_cols) * L_SEL
    ov = (c_start[:, None] <= s_start[None, :] + L_SEL - 1) & (c_start[:, None] + L_CMP - 1 >= s_start[None, :])
    ov &= (np.arange(n_rows) >= 1)[:, None]
    return ov.astype(np.float32)


def _online(state, s, msk, v_bf):
    m, l, acc = state
    s = jnp.where(msk, s, NEG_INF)
    m_new = jnp.maximum(m, jnp.max(s, axis=-1, keepdims=True))
    alpha = jnp.exp(m - m_new)
    p = jnp.where(msk, jnp.exp(s - m_new), 0.0)
    l = alpha * l + jnp.sum(p, axis=-1, keepdims=True)
    acc = alpha * acc + jnp.dot(p.astype(BF16), v_bf, preferred_element_type=F32)
    return m_new, l, acc


def _online_init(rows):
    return jnp.full((rows, 1), NEG_INF, F32), jnp.zeros((rows, 1), F32), jnp.zeros((rows, HD), F32)


def _online_out(state):
    _, l, acc = state
    return acc / jnp.maximum(l, 1e-30)


def _stack_heads(q, g):
    return jnp.concatenate([q[:, (HPG * g + h) * HD:(HPG * g + h + 1) * HD] for h in range(HPG)], axis=0) * (HD ** -0.5)


def _compressed_branch(qs, kc, vc, valid, n_tok):
    s = jnp.where(valid, _hdot(qs, kc, NT), NEG_INF)
    m = jnp.max(s, axis=-1, keepdims=True)
    p = jnp.where(valid, jnp.exp(s - m), 0.0)
    p = p / jnp.maximum(jnp.sum(p, axis=-1, keepdims=True), 1e-30)
    o = jnp.dot(p.astype(BF16), vc.astype(BF16), preferred_element_type=F32)
    psum = p[0:n_tok]
    for h in range(1, HPG):
        psum = psum + p[h * n_tok:(h + 1) * n_tok]
    return o, psum


def _select_blocks(imp, tpos, n_real):
    blk = lax.broadcasted_iota(jnp.int32, imp.shape, 1)
    cur = tpos // L_SEL
    forced = (blk == 0) | (blk == cur) | (blk == cur - 1)
    imp = jnp.where(forced, FORCE_SCORE, imp)
    imp = jnp.where(blk * L_SEL > tpos, -1.0, imp)
    imp = jnp.where(blk >= n_real, -2.0, imp)
    cnt = jnp.zeros(imp.shape, F32)
    for mcol in range(n_real):
        col = imp[:, mcol:mcol + 1]
        ahead = (col > imp) | ((col == imp) & (blk > mcol))
        cnt = cnt + jnp.where(ahead, 1.0, 0.0)
    return jnp.where(cnt < N_SEL, 1.0, 0.0)


def _nsa_prompt_kernel(q_ref, gate_ref, kc_ref, ks_ref, vs_ref, kw_ref, vw_ref, ov_ref, ex_ref, o_ref, *, tq, tk):
    i = pl.program_id(1)
    n_cmp = kc_ref.shape[1]
    n_sb = ov_ref.shape[1]
    rows = HPG * tq
    q = q_ref[0]
    gates = gate_ref[0]
    t0 = i * tq
    tpos4 = t0 + lax.broadcasted_iota(jnp.int32, (rows, 1), 0) % tq
    tpos = t0 + lax.broadcasted_iota(jnp.int32, (tq, 1), 0)
    n_kt = (t0 + tq + tk - 1) // tk
    w_start = pl.multiple_of(jnp.maximum(t0 - WIN, 0), tq)
    for g in range(NSA_GROUPS):
        qs = _stack_heads(q, g)
        qsb = qs.astype(BF16)
        ridx = lax.broadcasted_iota(jnp.int32, (rows, n_cmp), 1)
        valid = (ridx >= 1) & (D_CMP * ridx + (L_CMP - D_CMP - 1) <= tpos4)
        o_c, psum = _compressed_branch(qs, kc_ref[0, :, g * HD:(g + 1) * HD],
                                       kc_ref[0, :, KV_W + g * HD:KV_W + (g + 1) * HD], valid, tq)
        sel = _select_blocks(_hdot(psum, ov_ref[...]), tpos, n_sb)
        sel4 = jnp.concatenate([sel] * HPG, axis=0).astype(BF16)

        def body(j, state):
            k0 = pl.multiple_of(j * tk, tk)
            kt = ks_ref[0, pl.ds(k0, tk), g * HD:(g + 1) * HD].astype(BF16)
            vt = vs_ref[0, pl.ds(k0, tk), g * HD:(g + 1) * HD].astype(BF16)
            kpos = k0 + lax.broadcasted_iota(jnp.int32, (rows, tk), 1)
            msk = (jnp.dot(sel4, ex_ref[j], preferred_element_type=F32) > 0.5) & (kpos <= tpos4)
            return _online(state, _dg(qsb, kt, NT), msk, vt)

        o_s = _online_out(lax.fori_loop(0, n_kt, body, _online_init(rows)))
        kw = kw_ref[0, pl.ds(w_start, WIN + tq), g * HD:(g + 1) * HD].astype(BF16)
        vw = vw_ref[0, pl.ds(w_start, WIN + tq), g * HD:(g + 1) * HD].astype(BF16)
        dist = tpos4 - (w_start + lax.broadcasted_iota(jnp.int32, (rows, WIN + tq), 1))
        o_w = _online_out(_online(_online_init(rows), _dg(qsb, kw, NT), (dist >= 0) & (dist < WIN), vw))
        for h in range(HPG):
            hd = HPG * g + h
            r = slice(h * tq, (h + 1) * tq)
            o_ref[0, :, hd * HD:(hd + 1) * HD] = (gates[:, 3 * hd:3 * hd + 1] * o_c[r]
                                                  + gates[:, 3 * hd + 1:3 * hd + 2] * o_s[r]
                                                  + gates[:, 3 * hd + 2:3 * hd + 3] * o_w[r])


def _nsa_prompt(q, gates, kc, sel, win, tq, tk):
    b, t = q.shape[:2]
    n_cmp = kc.shape[1]
    n_sb = t // L_SEL
    ov = jnp.asarray(_overlap_matrix(n_cmp, n_sb))
    ex = np.arange(n_sb)[None, :, None] == (np.arange(t).reshape(t // tk, 1, tk) // L_SEL)
    ex = jnp.asarray(ex, dtype=BF16)
    seq_cols = lambda c: pl.BlockSpec((1, t, KV_W), lambda bi, i: (bi, 0, c))
    return pl.pallas_call(
        functools.partial(_nsa_prompt_kernel, tq=tq, tk=tk),
        grid=(b, t // tq),
        in_specs=[pl.BlockSpec((1, tq, W_A), lambda bi, i: (bi, i, 0)),
                  pl.BlockSpec((1, tq, LANES), lambda bi, i: (bi, i, 0)),
                  pl.BlockSpec((1, n_cmp, 2 * KV_W), lambda bi, i: (bi, 0, 0)),
                  seq_cols(0), seq_cols(1), seq_cols(0), seq_cols(1),
                  _const_spec((n_cmp, n_sb)), _const_spec((t // tk, n_sb, tk))],
        out_specs=pl.BlockSpec((1, tq, W_A), lambda bi, i: (bi, i, 0)),
        out_shape=jax.ShapeDtypeStruct((b, t, W_A), F32),
        compiler_params=pltpu.CompilerParams(dimension_semantics=("parallel", "arbitrary"),
                                             vmem_limit_bytes=VMEM_LIMIT),
        name="nsa_prompt",
    )(q, gates, kc, sel, sel, win, win, ov, ex)


def _nsa_sample_kernel(pt_ref, q_ref, gate_ref, kc_ref, pool_ref, snew_ref, wcache_ref, wnew_ref, ov_ref, ex_ref,
                       o_ref, buf, sem, sel_scr, oc_scr, m_scr, l_scr, acc_scr, *, n_pages, past):
    slot = _paged_tile(pt_ref, pool_ref, buf, sem, n_pages)
    t = pl.program_id(1)
    nt = pl.num_programs(1)
    ds = q_ref.shape[1]
    rows = HPG * ds
    n_cmp = kc_ref.shape[1]
    n_sbp = ov_ref.shape[1]
    blocks_per_tile = n_pages * PAGE_SIZE // L_SEL
    n_past_blocks = past // L_SEL
    keys = n_pages * PAGE_SIZE
    q = q_ref[0]
    tok4 = lax.broadcasted_iota(jnp.int32, (rows, 1), 0) % ds

    @pl.when(t == 0)
    def _():
        tpos = past + lax.broadcasted_iota(jnp.int32, (ds, 1), 0)
        for g in range(NSA_GROUPS):
            qs = _stack_heads(q, g)
            ridx = lax.broadcasted_iota(jnp.int32, (rows, n_cmp), 1)
            valid = (ridx >= 1) & (D_CMP * ridx + (L_CMP - D_CMP - 1) <= past + tok4)
            o_c, psum = _compressed_branch(qs, kc_ref[0, :, g * HD:(g + 1) * HD],
                                           kc_ref[0, :, KV_W + g * HD:KV_W + (g + 1) * HD], valid, ds)
            oc_scr[g] = o_c
            sel = _select_blocks(_hdot(psum, ov_ref[...]), tpos, n_past_blocks + 1)
            sel4 = jnp.concatenate([sel] * HPG, axis=0)
            for tt in range(sel_scr.shape[0]):
                sel_scr[tt, g] = sel4[:, tt * LANES:(tt + 1) * LANES]
            m_scr[g], l_scr[g], acc_scr[g] = _online_init(rows)

    kv = buf[slot].reshape(keys, 2 * KV_W)
    for g in range(NSA_GROUPS):
        qsb = _stack_heads(q, g).astype(BF16)
        kt = kv[:, g * HD:(g + 1) * HD].astype(BF16)
        vt = kv[:, KV_W + g * HD:KV_W + (g + 1) * HD].astype(BF16)
        first_blk = t * blocks_per_tile
        sel_t = sel_scr[first_blk // LANES, g]
        msk = jnp.dot(sel_t.astype(BF16), ex_ref[(first_blk % LANES) // blocks_per_tile],
                      preferred_element_type=F32) > 0.5
        m_scr[g], l_scr[g], acc_scr[g] = _online((m_scr[g], l_scr[g], acc_scr[g]), _dg(qsb, kt, NT), msk, vt)

    @pl.when(t == nt - 1)
    def _():
        gates = gate_ref[0]
        knew_pos = lax.broadcasted_iota(jnp.int32, (rows, ds), 1)
        snew = snew_ref[0]
        wnew = wnew_ref[0]
        wc = wcache_ref[0]
        n_wc = wc.shape[0]
        for g in range(NSA_GROUPS):
            qsb = _stack_heads(q, g).astype(BF16)
            kcol = slice(g * HD, (g + 1) * HD)
            vcol = slice(KV_W + g * HD, KV_W + (g + 1) * HD)
            sel_new = sel_scr[n_past_blocks // LANES, g][:, n_past_blocks % LANES:n_past_blocks % LANES + 1]
            st = _online((m_scr[g], l_scr[g], acc_scr[g]), _dg(qsb, snew[:, kcol].astype(BF16), NT),
                         (sel_new > 0.5) & (knew_pos <= tok4), snew[:, vcol].astype(BF16))
            o_s = _online_out(st)
            dist_c = (n_wc + tok4) - lax.broadcasted_iota(jnp.int32, (rows, n_wc), 1)
            st = _online(_online_init(rows), _dg(qsb, wc[:, kcol].astype(BF16), NT),
                         (dist_c >= 0) & (dist_c < WIN), wc[:, vcol].astype(BF16))
            dist_n = tok4 - knew_pos
            st = _online(st, _dg(qsb, wnew[:, kcol].astype(BF16), NT), (dist_n >= 0) & (dist_n < WIN),
                         wnew[:, vcol].astype(BF16))
            o_w = _online_out(st)
            o_c = oc_scr[g]
            for h in range(HPG):
                hd = HPG * g + h
                r = slice(h * ds, (h + 1) * ds)
                o_ref[0, :, hd * HD:(hd + 1) * HD] = (gates[:, 3 * hd:3 * hd + 1] * o_c[r]
                                                      + gates[:, 3 * hd + 1:3 * hd + 2] * o_s[r]
                                                      + gates[:, 3 * hd + 2:3 * hd + 3] * o_w[r])


def _nsa_sample(page_table, q, gates, kc, pool, sel_new, win_cache, win_new, n_pages):
    nb, ds = q.shape[:2]
    pages_per_seq = page_table.shape[1]
    past = pages_per_seq * PAGE_SIZE
    nt = pages_per_seq // n_pages
    n_cmp = kc.shape[1]
    n_sb = past // L_SEL + 1
    n_sbp = -(-n_sb // LANES) * LANES
    blocks_per_tile = n_pages * PAGE_SIZE // L_SEL
    keys = n_pages * PAGE_SIZE
    rows = HPG * ds
    ov = jnp.asarray(_overlap_matrix(n_cmp, n_sbp) * (np.arange(n_sbp) < n_sb)[None, :])
    ex = (np.arange(LANES)[None, :, None]
          == (np.arange(LANES // blocks_per_tile)[:, None, None] * blocks_per_tile + np.arange(keys)[None, None, :] // L_SEL))
    ex = jnp.asarray(ex, dtype=BF16)
    seq = lambda r, w: pl.BlockSpec((1, r, w), lambda b, t, pt: (b, 0, 0))
    cst = lambda shape: pl.BlockSpec(shape, lambda b, t, pt: (0,) * len(shape))
    return pl.pallas_call(
        functools.partial(_nsa_sample_kernel, n_pages=n_pages, past=past),
        grid_spec=pltpu.PrefetchScalarGridSpec(
            num_scalar_prefetch=1,
            grid=(nb, nt),
            in_specs=[seq(ds, W_A), seq(ds, LANES), seq(n_cmp, 2 * KV_W), pl.BlockSpec(memory_space=pl.ANY),
                      seq(ds, 2 * KV_W), seq(win_cache.shape[1], 2 * KV_W), seq(ds, 2 * KV_W),
                      cst((n_cmp, n_sbp)), cst(ex.shape)],
            out_specs=seq(ds, W_A),
            scratch_shapes=[pltpu.VMEM((2, n_pages, PAGE_SIZE, 2 * KV_W), F32),
                            pltpu.SemaphoreType.DMA((2,)),
                            pltpu.VMEM((n_sbp // LANES, NSA_GROUPS, rows, LANES), F32),
                            pltpu.VMEM((NSA_GROUPS, rows, HD), F32),
                            pltpu.VMEM((NSA_GROUPS, rows, 1), F32),
                            pltpu.VMEM((NSA_GROUPS, rows, 1), F32),
                            pltpu.VMEM((NSA_GROUPS, rows, HD), F32)]),
        out_shape=jax.ShapeDtypeStruct((nb, ds, W_A), F32),
        compiler_params=pltpu.CompilerParams(dimension_semantics=("arbitrary", "arbitrary"),
                                             vmem_limit_bytes=VMEM_LIMIT),
        name="nsa_sample",
    )(page_table, q, gates, kc, pool, sel_new, win_cache, win_new, ov, ex)


def _rwkv_kernel(rz_ref, shift_ref, s0_ref, mu_ref, w0_ref, wup_ref, a0_ref, aup_ref, kk_ref, ka_ref, rk_ref,
                 gng_ref, gnb_ref, y_ref, sout_ref, st, carry, *, t_valid):
    c = pl.program_id(1)
    cl_rows = rz_ref.shape[1]

    @pl.when(c == 0)
    def _():
        st[...] = s0_ref[0]
        carry[...] = shift_ref[0]

    z = rz_ref[0]
    ridx = lax.broadcasted_iota(jnp.int32, (cl_rows, 1), 0)
    prev = jnp.where(ridx == 0, carry[...], pltpu.roll(z, 1, 0))
    carry[...] = z[cl_rows - 1:cl_rows]
    zs = z + (prev - z) * mu_ref[...]
    r = zs[:, 0:W_R]
    k = zs[:, W_R:2 * W_R]
    v = zs[:, 2 * W_R:3 * W_R]
    wd = zs[:, 3 * W_R:3 * W_R + R_W]
    ad = zs[:, 3 * W_R + R_W:C_R]
    wpre = w0_ref[...] + _hdot(jnp.tanh(wd), wup_ref[...])
    w = -(jnp.maximum(-wpre, 0.0) + jnp.log(1.0 + jnp.exp(-jnp.abs(wpre)))) - 0.5
    ld = -jnp.exp(w)
    a = _sigmoid(a0_ref[...] + _hdot(ad, aup_ref[...]))
    kkn = k * kk_ref[...]
    k2 = k * (1.0 + (a - 1.0) * ka_ref[...])
    if t_valid < cl_rows:
        live = ridx < t_valid
        ld = jnp.where(live, ld, 0.0)
        kkn = jnp.where(live, kkn, 0.0)
        k2 = jnp.where(live, k2, 0.0)
    ti = lax.broadcasted_iota(jnp.int32, (cl_rows, cl_rows), 0)
    tj = lax.broadcasted_iota(jnp.int32, (cl_rows, cl_rows), 1)
    incl = tj <= ti
    strict = tj < ti
    tri = jnp.where(incl, 1.0, 0.0).astype(BF16)
    ld_h = ld.astype(BF16)
    ld_m = (ld - ld_h.astype(F32)).astype(BF16)
    ld_l = (ld - ld_h.astype(F32) - ld_m.astype(F32)).astype(BF16)
    cum = _dg(tri, ld_h) + (_dg(tri, ld_m) + _dg(tri, ld_l))
    n_levels = max(1, int(np.ceil(np.log2(cl_rows))))
    for h in range(R_HEADS):
        hs = slice(h * R_HD, (h + 1) * R_HD)
        cum_h = cum[:, hs]
        kk_h = kkn[:, hs]
        kk_h = kk_h / jnp.maximum(jnp.sqrt(jnp.sum(kk_h * kk_h, axis=-1, keepdims=True)), 1e-12)
        r_h, k_h, v_h, a_h = r[:, hs], k2[:, hs], v[:, hs], a[:, hs]
        p_in = jnp.exp(cum_h)
        p_ex = jnp.exp(cum_h - ld[:, hs])
        p_inv = jnp.exp(-cum_h)
        cum_end = cum_h[cl_rows - 1:cl_rows]
        p_end = jnp.exp(cum_end - cum_h)
        ar = _split(jnp.concatenate([-kk_h * p_ex, r_h * p_in], axis=0))
        bk = _split(jnp.concatenate([kk_h * a_h * p_inv, k_h * p_inv], axis=0))
        s_old = st[h]
        mm = _hdot_s(ar, bk, NT)
        gs = _hdot_s(ar, _split(s_old), NT)
        low = jnp.where(strict, mm[0:cl_rows, 0:cl_rows], 0.0)
        m_ak = jnp.where(strict, mm[0:cl_rows, cl_rows:], 0.0)
        m_rb = jnp.where(incl, mm[cl_rows:, 0:cl_rows], 0.0)
        m_rk = jnp.where(incl, mm[cl_rows:, cl_rows:], 0.0)
        v_s = _split(v_h)
        u = gs[0:cl_rows] + _hdot_s(_split(m_ak), v_s)
        lp = low
        for lvl in range(n_levels):
            lp_s = _split(lp)
            u = u + _hdot_s(lp_s, _split(u))
            if lvl + 1 < n_levels:
                lp = _hdot_s(lp_s, lp_s)
        u_s = _split(u)
        y = gs[cl_rows:] + _hdot_s(_split(m_rb), u_s) + _hdot_s(_split(m_rk), v_s)
        uv = _split(jnp.concatenate([u, v_h], axis=0))
        bk_end = _split(jnp.concatenate([kk_h * a_h * p_end, k_h * p_end], axis=0))
        st[h] = s_old * jnp.exp(cum_end) + _hdot_s(uv, bk_end, TN)
        mean = jnp.mean(y, axis=-1, keepdims=True)
        var = jnp.mean(jnp.square(y - mean), axis=-1, keepdims=True)
        yn = (y - mean) * lax.rsqrt(var + GN_EPS) * gng_ref[:, hs] + gnb_ref[:, hs]
        bonus = jnp.sum(r_h * k_h * rk_ref[:, hs], axis=-1, keepdims=True) * v_h
        y_ref[0, :, hs] = yn + bonus

    @pl.when(c == pl.num_programs(1) - 1)
    def _():
        sout_ref[0] = st[...]


def _rwkv(rz, shift0, state0, params, chunk, t_valid):
    b, t = rz.shape[:2]
    vec = lambda n: _const_spec((1, n))
    return pl.pallas_call(
        functools.partial(_rwkv_kernel, t_valid=t_valid),
        grid=(b, t // chunk),
        in_specs=[pl.BlockSpec((1, chunk, C_R), lambda bi, c: (bi, c, 0)),
                  pl.BlockSpec((1, 1, C_R), lambda bi, c: (bi, 0, 0)),
                  pl.BlockSpec((1, R_HEADS, R_HD, R_HD), lambda bi, c: (bi, 0, 0, 0)),
                  vec(C_R), vec(W_R), _const_spec((R_W, W_R)), vec(W_R), _const_spec((R_A, W_R)),
                  vec(W_R), vec(W_R), vec(W_R), vec(W_R), vec(W_R)],
        out_specs=[pl.BlockSpec((1, chunk, W_R), lambda bi, c: (bi, c, 0)),
                   pl.BlockSpec((1, R_HEADS, R_HD, R_HD), lambda bi, c: (bi, 0, 0, 0))],
        out_shape=[jax.ShapeDtypeStruct((b, t, W_R), F32),
                   jax.ShapeDtypeStruct((b, R_HEADS, R_HD, R_HD), F32)],
        scratch_shapes=[pltpu.VMEM((R_HEADS, R_HD, R_HD), F32), pltpu.VMEM((1, C_R), F32)],
        compiler_params=pltpu.CompilerParams(dimension_semantics=("parallel", "arbitrary"),
                                             vmem_limit_bytes=VMEM_LIMIT),
        name="rwkv",
    )(rz, shift0.reshape(b, 1, C_R), state0, *params)


def _rope_tables(pos):
    half = HD // 2
    inv = ROPE_THETA ** (-jnp.arange(half, dtype=F32) / half)
    ang = pos.astype(F32)[:, None] * inv[None, :]
    cos, sin = jnp.cos(ang), jnp.sin(ang)
    reps = LANES // HD
    return (jnp.tile(jnp.concatenate([cos, cos], axis=-1), (1, reps)),
            jnp.tile(jnp.concatenate([-sin, sin], axis=-1), (1, reps)))


def _pick_tile(n, cap):
    t = cap
    while n % t:
        t //= 2
    return t


def kernel(x_prompt, x_sample, p_prompt, p_sample, cache_cmp_kv, cache_sel_kv, cache_win_kv, state_wkv, state_shift, page_table, norm_g, w_in, cmp_pe, cmp_w1, cmp_b1, cmp_w2, cmp_b2, w_pa, rwkv_mu, rwkv_w0, rwkv_w_up, rwkv_a0, rwkv_a_up, rwkv_k_k, rwkv_k_a, rwkv_r_k, rwkv_gn_g, rwkv_gn_b, w_pb, w_out, ple_norm_g, w_ple_gate, w_ple_proj, final_norm_g):
    b, s_len = x_prompt.shape[:2]
    db, ds = x_sample.shape[:2]
    depth = norm_g.shape[0]
    assert depth == 1, "single-layer trunk"
    n_pages = page_table.shape[1]
    past = n_pages * PAGE_SIZE
    win_buf = cache_win_kv.shape[2]
    n_pool = cache_cmp_kv.shape[1]
    i = 0

    w = w_in[i]
    o_q, o_kv, o_ng, o_ag = 0, W_A, W_A + 6 * KV_W, W_A + 6 * KV_W + 3 * NSA_HEADS
    w_pack = jnp.concatenate([w[:, o_q:o_ng], w[:, o_ag:], w[:, o_ng:o_ag],
                              jnp.zeros((D_MODEL, C_PACK - w.shape[1]), w.dtype)], axis=1).astype(BF16)
    cw = _compress_weights(cmp_pe[i], cmp_w1[i], cmp_b1[i], cmp_w2[i], cmp_b2[i])
    row = lambda u: u.reshape(1, -1)
    rw = (row(rwkv_mu[i]), row(rwkv_w0[i]), rwkv_w_up[i], row(rwkv_a0[i]), rwkv_a_up[i], row(rwkv_k_k[i]),
          row(rwkv_k_a[i]), row(rwkv_r_k[i]), row(rwkv_gn_g[i]), row(rwkv_gn_b[i]))
    ow = (w_pa[i].astype(BF16), w_pb[i].astype(BF16), w_out[i].astype(BF16), row(ple_norm_g[i]),
          w_ple_gate[i].astype(BF16), w_ple_proj[i].astype(BF16), row(final_norm_g))

    tm = _pick_tile(s_len, 256)
    cos_p, sin_p = _rope_tables(jnp.arange(s_len, dtype=jnp.int32))
    xp2 = x_prompt.reshape(b * s_len, D_MODEL)
    q, cmp_p, sel_p, win_p, gates, sa, rz, sr, mg = _project(xp2, row(norm_g[i]), w_pack, cos_p, sin_p, tm)
    pages_p = s_len // PAGE_SIZE
    ident_table = jnp.arange(b * pages_p, dtype=jnp.int32).reshape(b, pages_p)
    kc_p = _compress(ident_table, cmp_p.reshape(b * pages_p, CHUNKS_PER_PAGE, CHUNK_W), *cw,
                     n_pages=_pick_tile(pages_p, 32))
    seq = lambda u: u.reshape(b, s_len, u.shape[-1])
    o_a = _nsa_prompt(seq(q), seq(gates), kc_p, seq(sel_p), seq(win_p), tq=128, tk=min(512, s_len))
    chunk = 64
    o_b, st_p = _rwkv(seq(rz), jnp.zeros((b, C_R), F32), jnp.zeros((b, R_HEADS, R_HD, R_HD), F32), rw, chunk, chunk)
    y_p = _merge(xp2, o_a.reshape(b * s_len, W_A), o_b.reshape(b * s_len, W_R), sa, sr, mg,
                 p_prompt[i].reshape(b * s_len, D_PLE), *ow, tm=tm)
    kv_shape = lambda u, n, t: u.reshape(n, t, 2, NSA_GROUPS, HD)
    cmp_kv_p = kv_shape(cmp_p, b, s_len)
    sel_kv_p = kv_shape(sel_p, b, s_len)
    win_kv_p = kv_shape(win_p, b, s_len)[:, s_len - min(WIN, s_len):]
    shift_p = seq(rz)[:, -1]

    n_s = db * ds
    cos_s, sin_s = _rope_tables(jnp.tile(past + jnp.arange(ds, dtype=jnp.int32), db))
    xs2 = x_sample.reshape(n_s, D_MODEL)
    q, cmp_s, sel_s, win_s, gates, sa, rz, sr, mg = _project(xs2, row(norm_g[i]), w_pack, cos_s, sin_s, n_s)
    kc_s = _compress(page_table, cache_cmp_kv[i].reshape(n_pool, CHUNKS_PER_PAGE, CHUNK_W), *cw,
                     n_pages=_pick_tile(n_pages, 32))
    seqs = lambda u: u.reshape(db, ds, u.shape[-1])
    o_a = _nsa_sample(page_table, seqs(q), seqs(gates), kc_s, cache_sel_kv[i].reshape(n_pool, PAGE_SIZE, 2 * KV_W),
                      seqs(sel_s), cache_win_kv[i].reshape(db, win_buf, 2 * KV_W), seqs(win_s),
                      n_pages=_pick_tile(n_pages, 32))
    pad_t = 16
    rz_pad = jnp.pad(seqs(rz), ((0, 0), (0, pad_t - ds), (0, 0)))
    o_b, st_s = _rwkv(rz_pad, state_shift[i], state_wkv[i], rw, pad_t, ds)
    y_s = _merge(xs2, o_a.reshape(n_s, W_A), o_b[:, :ds].reshape(n_s, W_R), sa, sr, mg,
                 p_sample[i].reshape(n_s, D_PLE), *ow, tm=n_s)
    win_kv_s = jnp.concatenate([cache_win_kv[i], kv_shape(win_s, db, ds)], axis=1)[:, ds:]
    shift_s = seqs(rz)[:, -1]

    return (y_p.reshape(b, s_len, D_MODEL), y_s.reshape(db, ds, D_MODEL),
            cmp_kv_p[None], kv_shape(cmp_s, db, ds)[None], sel_kv_p[None], kv_shape(sel_s, db, ds)[None],
            win_kv_p[None], win_kv_s[None], st_p[None], st_s[None], shift_p[None], shift_s[None])
```

```python
import functools

import numpy as np
import jax
import jax.numpy as jnp
from jax import lax
from jax.experimental import pallas as pl
from jax.experimental.pallas import tpu as pltpu

F32 = jnp.float32
BF16 = jnp.bfloat16

D_MODEL = 1024
D_PLE = 256
NSA_HEADS = 8
NSA_GROUPS = 2
HD = 64
HPG = NSA_HEADS // NSA_GROUPS
W_A = NSA_HEADS * HD
KV_W = NSA_GROUPS * HD
L_CMP = 32
D_CMP = 16
CMP_HID = 128
L_SEL = 64
N_SEL = 16
WIN = 512
FORCE_SCORE = 1e4
NEG_INF = -1e30
R_HEADS = 8
R_HD = 64
W_R = R_HEADS * R_HD
R_W = 64
R_A = 64
C_R = 3 * W_R + R_W + R_A
GN_EPS = 64e-5
ROPE_THETA = 10000.0
NORM_EPS = 1e-6
PAGE_SIZE = 128

LANES = 128
CHUNKS_PER_PAGE = PAGE_SIZE // D_CMP
CHUNK_W = D_CMP * 2 * KV_W
VMEM_LIMIT = 56 * 1024 * 1024

C_Q = 0
C_KV = 512
C_AG = 1280
C_RZ = 1792
C_RG = 3456
C_MG = 3968
C_NG = 6016
C_PACK = 6144

NN = ((1,), (0,))
NT = ((1,), (1,))
TN = ((0,), (0,))


def _dg(a, b, dims=NN):
    return lax.dot_general(a, b, (dims, ((), ())), preferred_element_type=F32)


def _split(x):
    hi = x.astype(BF16)
    lo = (x - hi.astype(F32)).astype(BF16)
    return hi, lo


def _hdot_s(a, b, dims=NN):
    (ah, al), (bh, bl) = a, b
    return _dg(ah, bh, dims) + (_dg(ah, bl, dims) + _dg(al, bh, dims))


def _hdot(a, b, dims=NN):
    return _hdot_s(_split(a), _split(b), dims)


def _sigmoid(x):
    return 1.0 / (1.0 + jnp.exp(-x))


def _silu(x):
    return x * _sigmoid(x)


def _rms(x, g):
    return x * lax.rsqrt(jnp.mean(x * x, axis=-1, keepdims=True) + NORM_EPS) * g


def _const_spec(shape):
    zeros = (0,) * len(shape)
    return pl.BlockSpec(shape, lambda *_: zeros)


def _rope128(z, cos, sin, first):
    partner = jnp.where(first, pltpu.roll(z, LANES - HD // 2, 1), pltpu.roll(z, HD // 2, 1))
    return z * cos + partner * sin


def _proj_kernel(x_ref, g_ref, w_ref, cos_ref, sin_ref,
                 q_ref, cmp_ref, sel_ref, win_ref, gate_ref, sa_ref, rz_ref, sr_ref, mg_ref, *, kv_t):
    hb = _rms(x_ref[...], g_ref[...]).astype(BF16)
    cos = cos_ref[...]
    sin = sin_ref[...]
    lane = lax.broadcasted_iota(jnp.int32, cos.shape, 1)
    first = (lane % HD) < (HD // 2)

    def mm(a, b):
        return jnp.dot(hb, w_ref[:, a:b], preferred_element_type=F32)

    zq = mm(C_Q, C_Q + W_A)
    for c in range(W_A // LANES):
        q_ref[:, c * LANES:(c + 1) * LANES] = _rope128(zq[:, c * LANES:(c + 1) * LANES], cos, sin, first)
    for i, ref in enumerate((cmp_ref, sel_ref, win_ref)):
        zkv = mm(C_KV + 2 * KV_W * i, C_KV + 2 * KV_W * (i + 1))
        k_rot = _rope128(zkv[:, 0:KV_W], cos, sin, first)
        if kv_t:
            ref[0] = jnp.concatenate([k_rot, zkv[:, KV_W:2 * KV_W]], axis=1).T
        else:
            ref[:, 0:KV_W] = k_rot
            ref[:, KV_W:2 * KV_W] = zkv[:, KV_W:2 * KV_W]
    sa_ref[...] = _silu(mm(C_AG, C_AG + W_A))
    rz_ref[...] = mm(C_RZ, C_RZ + C_R)
    sr_ref[...] = _silu(mm(C_RG, C_RG + W_R))
    mg_ref[...] = _sigmoid(mm(C_MG, C_MG + 2 * D_MODEL))
    gate_ref[...] = _sigmoid(mm(C_NG, C_PACK))


def _project(x2d, norm_g, w_pack, cos_t, sin_t, tm, kv_t):
    n = x2d.shape[0]
    n_pos_blocks = cos_t.shape[0] // tm
    widths = (W_A, 2 * KV_W, 2 * KV_W, 2 * KV_W, LANES, W_A, C_R, W_R, 2 * D_MODEL)
    row = lambda w: pl.BlockSpec((tm, w), lambda i: (i, 0))
    pos = pl.BlockSpec((tm, LANES), lambda i: (i % n_pos_blocks, 0))
    out_specs = [row(w) for w in widths]
    out_shape = [jax.ShapeDtypeStruct((n, w), F32) for w in widths]
    if kv_t:
        for o in (1, 2, 3):
            out_specs[o] = pl.BlockSpec((1, 2 * KV_W, tm), lambda i: (i // n_pos_blocks, 0, i % n_pos_blocks))
            out_shape[o] = jax.ShapeDtypeStruct((n // cos_t.shape[0], 2 * KV_W, cos_t.shape[0]), F32)
    return pl.pallas_call(
        functools.partial(_proj_kernel, kv_t=kv_t),
        grid=(n // tm,),
        in_specs=[row(D_MODEL), _const_spec((1, D_MODEL)), _const_spec((D_MODEL, C_PACK)), pos, pos],
        out_specs=out_specs,
        out_shape=out_shape,
        compiler_params=pltpu.CompilerParams(dimension_semantics=("parallel",), vmem_limit_bytes=VMEM_LIMIT),
        name="project",
    )(x2d, norm_g, w_pack, cos_t, sin_t)


def _merge_kernel(x_ref, oa_ref, ob_ref, sa_ref, sr_ref, mg_ref, p_ref,
                  wpa_ref, wpb_ref, wout_ref, pg_ref, wgate_ref, wproj_ref, fg_ref, y_ref):
    bdot = lambda a, w_ref: jnp.dot(a.astype(BF16), w_ref[...], preferred_element_type=F32)
    ya = bdot(oa_ref[...] * sa_ref[...], wpa_ref)
    yb = bdot(ob_ref[...] * sr_ref[...], wpb_ref)
    m = mg_ref[:, 0:D_MODEL] * ya + mg_ref[:, D_MODEL:2 * D_MODEL] * yb
    x2 = x_ref[...] + bdot(m, wout_ref)
    gate = _sigmoid(bdot(_rms(x2, pg_ref[...]), wgate_ref))
    x3 = x2 + gate * bdot(p_ref[...], wproj_ref)
    y_ref[...] = _rms(x3, fg_ref[...])


def _merge(x2d, oa, ob, sa, sr, mg, p2d, wpa, wpb, wout, pg, wgate, wproj, fg, tm):
    n = x2d.shape[0]
    row = lambda w: pl.BlockSpec((tm, w), lambda i: (i, 0))
    return pl.pallas_call(
        _merge_kernel,
        grid=(n // tm,),
        in_specs=[row(D_MODEL), row(W_A), row(W_R), row(W_A), row(W_R), row(2 * D_MODEL), row(D_PLE),
                  _const_spec((W_A, D_MODEL)), _const_spec((W_R, D_MODEL)), _const_spec((D_MODEL, D_MODEL)),
                  _const_spec((1, D_MODEL)), _const_spec((D_MODEL, D_MODEL)), _const_spec((D_PLE, D_MODEL)),
                  _const_spec((1, D_MODEL))],
        out_specs=row(D_MODEL),
        out_shape=jax.ShapeDtypeStruct((n, D_MODEL), F32),
        compiler_params=pltpu.CompilerParams(dimension_semantics=("parallel",), vmem_limit_bytes=VMEM_LIMIT),
        name="merge",
    )(x2d, oa, ob, sa, sr, mg, p2d, wpa, wpb, wout, pg, wgate, wproj, fg)


def _page_copies(pt_ref, pool_ref, buf, sem, b, t, slot, n_pages):
    return [pltpu.make_async_copy(pool_ref.at[pt_ref[b, t * n_pages + p]],
                                  buf.at[slot, :, pl.ds(p * PAGE_SIZE, PAGE_SIZE)], sem.at[slot])
            for p in range(n_pages)]


def _paged_tile(pt_ref, pool_ref, buf, sem, n_pages):
    b, t = pl.program_id(0), pl.program_id(1)
    nt = pl.num_programs(1)
    step = b * nt + t
    slot = step % 2

    @pl.when(step == 0)
    def _():
        for cp in _page_copies(pt_ref, pool_ref, buf, sem, b, t, slot, n_pages):
            cp.start()

    nxt = step + 1

    @pl.when(nxt < pl.num_programs(0) * nt)
    def _():
        for cp in _page_copies(pt_ref, pool_ref, buf, sem, nxt // nt, nxt % nt, 1 - slot, n_pages):
            cp.start()

    for cp in _page_copies(pt_ref, pool_ref, buf, sem, b, t, slot, n_pages):
        cp.wait()
    return slot


CMP_HALF = 2 * NSA_GROUPS * CMP_HID


def _compress_core(page_t, xs, perm_ref, w1_ref, pe_ref, b1_ref, w2_ref, b2_ref, carry, out_ref):
    n_pages = xs.shape[0]
    rows = n_pages * CHUNKS_PER_PAGE
    for p in range(n_pages):
        xp = _dg(perm_ref[...], page_t(p).astype(BF16), NT)
        xs[p] = xp.reshape(D_CMP, CHUNKS_PER_PAGE, 2 * KV_W)
    z = None
    for j in range(D_CMP):
        xj = jnp.concatenate([xs[:, j].reshape(rows, 2 * KV_W), pe_ref[j]], axis=0).astype(BF16)
        zj = jnp.dot(xj, w1_ref[j], preferred_element_type=F32)
        z = zj if z is None else z + zj
    pos = z[rows:rows + 1, 0:CMP_HALF] + z[rows + 1:rows + 2, CMP_HALF:]
    lo = z[0:rows, 0:CMP_HALF]
    ridx = lax.broadcasted_iota(jnp.int32, (rows, CMP_HALF), 0)
    lo_prev = jnp.where(ridx == 0, carry[...], pltpu.roll(lo, 1, 0))
    carry[...] = lo[rows - 1:rows]
    hid = lo_prev + z[0:rows, CMP_HALF:] + pos + b1_ref[...]
    out_ref[0] = jnp.dot(_silu(hid).astype(BF16), w2_ref[...], preferred_element_type=F32) + b2_ref[...]


def _compress_paged_kernel(pt_ref, pool_ref, perm_ref, w1_ref, pe_ref, b1_ref, w2_ref, b2_ref, out_ref,
                           buf, sem, xs, carry, *, n_pages):
    slot = _paged_tile(pt_ref, pool_ref, buf, sem, n_pages)

    @pl.when(pl.program_id(1) == 0)
    def _():
        carry[...] = jnp.zeros_like(carry)

    _compress_core(lambda p: buf[slot, :, p * PAGE_SIZE:(p + 1) * PAGE_SIZE], xs, perm_ref, w1_ref, pe_ref, b1_ref,
                   w2_ref, b2_ref, carry, out_ref)


def _compress_seq_kernel(x_ref, perm_ref, w1_ref, pe_ref, b1_ref, w2_ref, b2_ref, out_ref, xs, carry):
    carry[...] = jnp.zeros_like(carry)
    _compress_core(lambda p: x_ref[0, :, p * PAGE_SIZE:(p + 1) * PAGE_SIZE], xs, perm_ref, w1_ref, pe_ref, b1_ref,
                   w2_ref, b2_ref, carry, out_ref)


def _compress_specs(cst):
    return [cst((PAGE_SIZE, PAGE_SIZE)), cst((D_CMP, 2 * KV_W, 2 * CMP_HALF)), cst((D_CMP, 8, 2 * KV_W)),
            cst((1, CMP_HALF)), cst((CMP_HALF, 2 * KV_W)), cst((1, 2 * KV_W))]


def _compress_scratch(n_pages):
    return [pltpu.VMEM((n_pages, D_CMP, CHUNKS_PER_PAGE, 2 * KV_W), F32), pltpu.VMEM((1, CMP_HALF), F32)]


def _compress_paged(page_table, pool, cw, n_pages):
    nb, pages_per_seq = page_table.shape
    nt = pages_per_seq // n_pages
    toks = n_pages * PAGE_SIZE
    rows = toks // D_CMP
    cst = lambda shape: pl.BlockSpec(shape, lambda b, t, pt: (0,) * len(shape))
    return pl.pallas_call(
        functools.partial(_compress_paged_kernel, n_pages=n_pages),
        grid_spec=pltpu.PrefetchScalarGridSpec(
            num_scalar_prefetch=1,
            grid=(nb, nt),
            in_specs=[pl.BlockSpec(memory_space=pl.ANY)] + _compress_specs(cst),
            out_specs=pl.BlockSpec((1, rows, 2 * KV_W), lambda b, t, pt: (b, t, 0)),
            scratch_shapes=[pltpu.VMEM((2, 2 * KV_W, toks), F32), pltpu.SemaphoreType.DMA((2,))]
            + _compress_scratch(n_pages)),
        out_shape=jax.ShapeDtypeStruct((nb, nt * rows, 2 * KV_W), F32),
        compiler_params=pltpu.CompilerParams(dimension_semantics=("arbitrary", "arbitrary"),
                                             vmem_limit_bytes=VMEM_LIMIT),
        name="compress_paged",
    )(page_table, pool, *cw)


def _compress_seq(x_t, cw):
    nb, _, toks = x_t.shape
    rows = toks // D_CMP
    return pl.pallas_call(
        _compress_seq_kernel,
        grid=(nb,),
        in_specs=[pl.BlockSpec((1, 2 * KV_W, toks), lambda b: (b, 0, 0))] + _compress_specs(_const_spec),
        out_specs=pl.BlockSpec((1, rows, 2 * KV_W), lambda b: (b, 0, 0)),
        out_shape=jax.ShapeDtypeStruct((nb, rows, 2 * KV_W), F32),
        scratch_shapes=_compress_scratch(toks // PAGE_SIZE),
        compiler_params=pltpu.CompilerParams(dimension_semantics=("parallel",), vmem_limit_bytes=VMEM_LIMIT),
        name="compress_seq",
    )(x_t, *cw)


def _compress_weights(cmp_pe, cmp_w1, cmp_b1, cmp_w2, cmp_b2):
    eye_kv = jnp.eye(2, dtype=F32)
    eye_g = jnp.eye(NSA_GROUPS, dtype=F32)
    w1 = cmp_w1.reshape(2, 2, D_CMP, HD, CMP_HID)
    w1j = jnp.einsum('kljdf,kK,gG->jkgdlKGf', w1, eye_kv, eye_g).reshape(D_CMP, 2 * KV_W, 2 * CMP_HALF)
    pe = cmp_pe.reshape(2, 2, D_CMP, HD)
    pe = jnp.broadcast_to(jnp.transpose(pe, (2, 1, 0, 3))[:, :, :, None, :], (D_CMP, 2, 2, NSA_GROUPS, HD))
    pej = jnp.pad(pe.reshape(D_CMP, 2, 2 * KV_W), ((0, 0), (0, 6), (0, 0)))
    b1big = jnp.broadcast_to(cmp_b1[:, None, :], (2, NSA_GROUPS, CMP_HID)).reshape(1, -1)
    w2big = jnp.einsum('kfd,kK,gG->kgfKGd', cmp_w2, eye_kv, eye_g).reshape(CMP_HALF, 2 * KV_W)
    b2big = jnp.broadcast_to(cmp_b2[:, None, :], (2, NSA_GROUPS, HD)).reshape(1, -1)
    dst = np.arange(PAGE_SIZE)
    perm = (dst[:, None] % CHUNKS_PER_PAGE) * D_CMP + dst[:, None] // CHUNKS_PER_PAGE == dst[None, :]
    return jnp.asarray(perm, dtype=BF16), w1j.astype(BF16), pej, b1big, w2big.astype(BF16), b2big


def _overlap_matrix(n_rows, n_cols):
    c_start = (np.arange(n_rows) - 1) * D_CMP
    s_start = np.arange(n_cols) * L_SEL
    ov = (c_start[:, None] <= s_start[None, :] + L_SEL - 1) & (c_start[:, None] + L_CMP - 1 >= s_start[None, :])
    ov &= (np.arange(n_rows) >= 1)[:, None]
    return ov.astype(np.float32)


def _online(state, s, msk, v_bf, v_dims=NN):
    m, l, acc = state
    if msk is not None:
        s = jnp.where(msk, s, NEG_INF)
    m_new = jnp.maximum(m, jnp.max(s, axis=-1, keepdims=True))
    alpha = jnp.exp(m - m_new)
    p = jnp.exp(s - m_new)
    if msk is not None:
        p = jnp.where(msk, p, 0.0)
    l = alpha * l + jnp.sum(p, axis=-1, keepdims=True)
    acc = alpha * acc + _dg(p.astype(BF16), v_bf, v_dims)
    return m_new, l, acc


def _online_init(rows):
    return jnp.full((rows, 1), NEG_INF, F32), jnp.zeros((rows, 1), F32), jnp.zeros((rows, HD), F32)


def _online_out(state):
    _, l, acc = state
    return acc / jnp.maximum(l, 1e-30)


def _stack_heads(q, g):
    return jnp.concatenate([q[:, (HPG * g + h) * HD:(HPG * g + h + 1) * HD] for h in range(HPG)], axis=0) * (HD ** -0.5)


def _compressed_branch(qs, kc, vc, valid, n_tok):
    s = jnp.where(valid, _hdot(qs, kc, NT), NEG_INF)
    m = jnp.max(s, axis=-1, keepdims=True)
    p = jnp.where(valid, jnp.exp(s - m), 0.0)
    p = p / jnp.maximum(jnp.sum(p, axis=-1, keepdims=True), 1e-30)
    o = jnp.dot(p.astype(BF16), vc.astype(BF16), preferred_element_type=F32)
    psum = p[0:n_tok]
    for h in range(1, HPG):
        psum = psum + p[h * n_tok:(h + 1) * n_tok]
    return o, psum


def _select_blocks(imp, tpos, n_real):
    blk = lax.broadcasted_iota(jnp.int32, imp.shape, 1)
    cur = tpos // L_SEL
    forced = (blk == 0) | (blk == cur) | (blk == cur - 1)
    imp = jnp.where(forced, FORCE_SCORE, imp)
    imp = jnp.where(blk * L_SEL > tpos, -1.0, imp)
    imp = jnp.where(blk >= n_real, -2.0, imp)
    cnt = jnp.zeros(imp.shape, F32)
    for mcol in range(n_real):
        col = imp[:, mcol:mcol + 1]
        ahead = (col > imp) | ((col == imp) & (blk > mcol))
        cnt = cnt + jnp.where(ahead, 1.0, 0.0)
    return jnp.where(cnt < N_SEL, 1.0, 0.0)


def _select_blocks_t(imp_t, tpos_row, n_real):
    sub = 8
    blk = lax.broadcasted_iota(jnp.int32, imp_t.shape, 0)
    cur = jnp.right_shift(tpos_row, int(np.log2(L_SEL)))
    forced = (blk == 0) | (blk == cur) | (blk == cur - 1)
    imp_t = jnp.where(forced, FORCE_SCORE, imp_t)
    imp_t = jnp.where(blk * L_SEL > tpos_row, -1.0, imp_t)
    slabs = [imp_t[s0:s0 + sub] for s0 in range(0, n_real, sub)]
    cnt = [jnp.zeros(s.shape, F32) for s in slabs]
    sub_iota = lax.broadcasted_iota(jnp.int32, slabs[0].shape, 0)
    for m in range(n_real):
        row = jnp.broadcast_to(imp_t[m:m + 1], slabs[0].shape)
        for si, slab in enumerate(slabs):
            ge = jnp.where(row >= slab, 1.0, 0.0)
            gt = jnp.where(row > slab, 1.0, 0.0)
            if si * sub > m:
                ahead = ge
            elif si * sub + sub - 1 < m:
                ahead = gt
            else:
                ahead = jnp.where(sub_iota > m - si * sub, ge, gt)
            cnt[si] = cnt[si] + ahead
    return jnp.where(jnp.concatenate(cnt, axis=0) < N_SEL, 1.0, 0.0)


def _nsa_prompt_kernel(q_ref, gate_ref, kc_ref, sel_ref, win_ref, ovt_ref, ex_ref, eye_ref, o_ref, *, tq, tk):
    i = pl.program_id(1)
    n_cmp = kc_ref.shape[1]
    n_sb = ovt_ref.shape[0]
    rows = HPG * tq
    q = q_ref[0]
    gates = gate_ref[0]
    t0 = i * tq
    tpos = t0 + lax.broadcasted_iota(jnp.int32, (tq, 1), 0)
    tpos4 = jnp.concatenate([tpos] * HPG, axis=0)
    tpos_row = t0 + lax.broadcasted_iota(jnp.int32, (1, tq), 1)
    n_full = t0 // tk
    w_start = pl.multiple_of(jnp.maximum(t0 - WIN, 0), tq)
    for g in range(NSA_GROUPS):
        qs = _stack_heads(q, g)
        qsb = qs.astype(BF16)
        ridx = lax.broadcasted_iota(jnp.int32, (rows, n_cmp), 1)
        valid = (ridx >= 1) & (D_CMP * ridx + (L_CMP - D_CMP - 1) <= tpos4)
        o_c, psum = _compressed_branch(qs, kc_ref[0, :, g * HD:(g + 1) * HD],
                                       kc_ref[0, :, KV_W + g * HD:KV_W + (g + 1) * HD], valid, tq)
        sel_t = _select_blocks_t(_hdot(ovt_ref[...], psum, NT), tpos_row, n_sb)
        sel = _dg(sel_t.astype(BF16), eye_ref[...], TN)
        bias4 = jnp.concatenate([(sel - 1.0) * (-NEG_INF)] * HPG, axis=0).astype(BF16)
        k_rows = slice(g * HD, (g + 1) * HD)
        v_rows = slice(KV_W + g * HD, KV_W + (g + 1) * HD)

        qa = jnp.concatenate([qsb, bias4], axis=1)

        def scores(j):
            k0 = pl.multiple_of(j * tk, tk)
            ka = jnp.concatenate([sel_ref[0, k_rows, pl.ds(k0, tk)].astype(BF16), ex_ref[j]], axis=0)
            return k0, _dg(qa, ka), sel_ref[0, v_rows, pl.ds(k0, tk)].astype(BF16)

        def body(j, state):
            _, s, vt = scores(j)
            return _online(state, s, None, vt, NT)

        state = lax.fori_loop(0, n_full, body, _online_init(rows))
        k0, s, vt = scores(n_full)
        kpos = k0 + lax.broadcasted_iota(jnp.int32, (rows, tk), 1)
        o_s = _online_out(_online(state, jnp.where(kpos <= tpos4, s, NEG_INF), None, vt, NT))
        kw = win_ref[0, k_rows, pl.ds(w_start, WIN + tq)].astype(BF16)
        vw = win_ref[0, v_rows, pl.ds(w_start, WIN + tq)].astype(BF16)
        dist = tpos4 - (w_start + lax.broadcasted_iota(jnp.int32, (rows, WIN + tq), 1))
        o_w = _online_out(_online(_online_init(rows), _dg(qsb, kw), (dist >= 0) & (dist < WIN), vw, NT))
        for h in range(HPG):
            hd = HPG * g + h
            r = slice(h * tq, (h + 1) * tq)
            o_ref[0, :, hd * HD:(hd + 1) * HD] = (gates[:, 3 * hd:3 * hd + 1] * o_c[r]
                                                  + gates[:, 3 * hd + 1:3 * hd + 2] * o_s[r]
                                                  + gates[:, 3 * hd + 2:3 * hd + 3] * o_w[r])


def _nsa_prompt(q, gates, kc, sel_t, win_t, tq, tk):
    b, t = q.shape[:2]
    n_cmp = kc.shape[1]
    n_sb = t // L_SEL
    ovt = jnp.asarray(_overlap_matrix(n_cmp, n_sb).T)
    ex = np.arange(n_sb)[None, :, None] == (np.arange(t).reshape(t // tk, 1, tk) // L_SEL)
    ex = jnp.asarray(ex, dtype=BF16)
    eye = jnp.eye(n_sb, dtype=BF16)
    whole_seq = pl.BlockSpec((1, 2 * KV_W, t), lambda bi, i: (bi, 0, 0))
    return pl.pallas_call(
        functools.partial(_nsa_prompt_kernel, tq=tq, tk=tk),
        grid=(b, t // tq),
        in_specs=[pl.BlockSpec((1, tq, W_A), lambda bi, i: (bi, i, 0)),
                  pl.BlockSpec((1, tq, LANES), lambda bi, i: (bi, i, 0)),
                  pl.BlockSpec((1, n_cmp, 2 * KV_W), lambda bi, i: (bi, 0, 0)),
                  whole_seq, whole_seq,
                  _const_spec((n_sb, n_cmp)), _const_spec((t // tk, n_sb, tk)), _const_spec((n_sb, n_sb))],
        out_specs=pl.BlockSpec((1, tq, W_A), lambda bi, i: (bi, i, 0)),
        out_shape=jax.ShapeDtypeStruct((b, t, W_A), F32),
        compiler_params=pltpu.CompilerParams(dimension_semantics=("parallel", "arbitrary"),
                                             vmem_limit_bytes=VMEM_LIMIT),
        name="nsa_prompt",
    )(q, gates, kc, sel_t, win_t, ovt, ex, eye)


def _nsa_sample_kernel(pt_ref, q_ref, gate_ref, kc_ref, pool_ref, snew_ref, wcache_ref, wnew_ref, ov_ref, ex_ref,
                       o_ref, buf, sem, sel_scr, oc_scr, m_scr, l_scr, acc_scr, *, n_pages, past):
    slot = _paged_tile(pt_ref, pool_ref, buf, sem, n_pages)
    t = pl.program_id(1)
    nt = pl.num_programs(1)
    ds = q_ref.shape[1]
    rows = HPG * ds
    n_cmp = kc_ref.shape[1]
    n_sbp = ov_ref.shape[1]
    blocks_per_tile = n_pages * PAGE_SIZE // L_SEL
    n_past_blocks = past // L_SEL
    keys = n_pages * PAGE_SIZE
    q = q_ref[0]
    tok4 = lax.broadcasted_iota(jnp.int32, (rows, 1), 0) % ds

    @pl.when(t == 0)
    def _():
        tpos = past + lax.broadcasted_iota(jnp.int32, (ds, 1), 0)
        for g in range(NSA_GROUPS):
            qs = _stack_heads(q, g)
            ridx = lax.broadcasted_iota(jnp.int32, (rows, n_cmp), 1)
            valid = (ridx >= 1) & (D_CMP * ridx + (L_CMP - D_CMP - 1) <= past + tok4)
            o_c, psum = _compressed_branch(qs, kc_ref[0, :, g * HD:(g + 1) * HD],
                                           kc_ref[0, :, KV_W + g * HD:KV_W + (g + 1) * HD], valid, ds)
            oc_scr[g] = o_c
            sel = _select_blocks(_hdot(psum, ov_ref[...]), tpos, n_past_blocks + 1)
            sel4 = jnp.concatenate([sel] * HPG, axis=0)
            for tt in range(sel_scr.shape[0]):
                sel_scr[tt, g] = sel4[:, tt * LANES:(tt + 1) * LANES]
            m_scr[g], l_scr[g], acc_scr[g] = _online_init(rows)

    kv_t = buf[slot]
    first_blk = t * blocks_per_tile
    ex = ex_ref[(first_blk % LANES) // blocks_per_tile]
    for g in range(NSA_GROUPS):
        qsb = _stack_heads(q, g).astype(BF16)
        bias = ((sel_scr[first_blk // LANES, g] - 1.0) * (-NEG_INF)).astype(BF16)
        qa = jnp.concatenate([qsb, bias], axis=1)
        ka = jnp.concatenate([kv_t[g * HD:(g + 1) * HD].astype(BF16), ex], axis=0)
        vt = kv_t[KV_W + g * HD:KV_W + (g + 1) * HD].astype(BF16)
        m_scr[g], l_scr[g], acc_scr[g] = _online((m_scr[g], l_scr[g], acc_scr[g]), _dg(qa, ka), None, vt, NT)

    @pl.when(t == nt - 1)
    def _():
        gates = gate_ref[0]
        knew_pos = lax.broadcasted_iota(jnp.int32, (rows, ds), 1)
        snew = snew_ref[0]
        wnew = wnew_ref[0]
        wc_t = wcache_ref[0]
        n_wc = wc_t.shape[1]
        for g in range(NSA_GROUPS):
            qsb = _stack_heads(q, g).astype(BF16)
            kcol = slice(g * HD, (g + 1) * HD)
            vcol = slice(KV_W + g * HD, KV_W + (g + 1) * HD)
            sel_new = sel_scr[n_past_blocks // LANES, g][:, n_past_blocks % LANES:n_past_blocks % LANES + 1]
            st = _online((m_scr[g], l_scr[g], acc_scr[g]), _dg(qsb, snew[:, kcol].astype(BF16), NT),
                         (sel_new > 0.5) & (knew_pos <= tok4), snew[:, vcol].astype(BF16))
            o_s = _online_out(st)
            dist_c = (n_wc + tok4) - lax.broadcasted_iota(jnp.int32, (rows, n_wc), 1)
            st = _online(_online_init(rows), _dg(qsb, wc_t[kcol].astype(BF16)),
                         (dist_c >= 0) & (dist_c < WIN), wc_t[vcol].astype(BF16), NT)
            dist_n = tok4 - knew_pos
            st = _online(st, _dg(qsb, wnew[:, kcol].astype(BF16), NT), (dist_n >= 0) & (dist_n < WIN),
                         wnew[:, vcol].astype(BF16))
            o_w = _online_out(st)
            o_c = oc_scr[g]
            for h in range(HPG):
                hd = HPG * g + h
                r = slice(h * ds, (h + 1) * ds)
                o_ref[0, :, hd * HD:(hd + 1) * HD] = (gates[:, 3 * hd:3 * hd + 1] * o_c[r]
                                                      + gates[:, 3 * hd + 1:3 * hd + 2] * o_s[r]
                                                      + gates[:, 3 * hd + 2:3 * hd + 3] * o_w[r])


def _nsa_sample(page_table, q, gates, kc, pool, sel_new, win_cache, win_new, n_pages):
    nb, ds = q.shape[:2]
    pages_per_seq = page_table.shape[1]
    past = pages_per_seq * PAGE_SIZE
    nt = pages_per_seq // n_pages
    n_cmp = kc.shape[1]
    n_sb = past // L_SEL + 1
    n_sbp = -(-n_sb // LANES) * LANES
    blocks_per_tile = n_pages * PAGE_SIZE // L_SEL
    keys = n_pages * PAGE_SIZE
    rows = HPG * ds
    ov = jnp.asarray(_overlap_matrix(n_cmp, n_sbp) * (np.arange(n_sbp) < n_sb)[None, :])
    ex = (np.arange(LANES)[None, :, None]
          == (np.arange(LANES // blocks_per_tile)[:, None, None] * blocks_per_tile + np.arange(keys)[None, None, :] // L_SEL))
    ex = jnp.asarray(ex, dtype=BF16)
    seq = lambda r, w: pl.BlockSpec((1, r, w), lambda b, t, pt: (b, 0, 0))
    cst = lambda shape: pl.BlockSpec(shape, lambda b, t, pt: (0,) * len(shape))
    return pl.pallas_call(
        functools.partial(_nsa_sample_kernel, n_pages=n_pages, past=past),
        grid_spec=pltpu.PrefetchScalarGridSpec(
            num_scalar_prefetch=1,
            grid=(nb, nt),
            in_specs=[seq(ds, W_A), seq(ds, LANES), seq(n_cmp, 2 * KV_W), pl.BlockSpec(memory_space=pl.ANY),
                      seq(ds, 2 * KV_W), seq(2 * KV_W, win_cache.shape[2]), seq(ds, 2 * KV_W),
                      cst((n_cmp, n_sbp)), cst(ex.shape)],
            out_specs=seq(ds, W_A),
            scratch_shapes=[pltpu.VMEM((2, 2 * KV_W, keys), F32),
                            pltpu.SemaphoreType.DMA((2,)),
                            pltpu.VMEM((n_sbp // LANES, NSA_GROUPS, rows, LANES), F32),
                            pltpu.VMEM((NSA_GROUPS, rows, HD), F32),
                            pltpu.VMEM((NSA_GROUPS, rows, 1), F32),
                            pltpu.VMEM((NSA_GROUPS, rows, 1), F32),
                            pltpu.VMEM((NSA_GROUPS, rows, HD), F32)]),
        out_shape=jax.ShapeDtypeStruct((nb, ds, W_A), F32),
        compiler_params=pltpu.CompilerParams(dimension_semantics=("arbitrary", "arbitrary"),
                                             vmem_limit_bytes=VMEM_LIMIT),
        name="nsa_sample",
    )(page_table, q, gates, kc, pool, sel_new, win_cache, win_new, ov, ex)


def _rwkv_kernel(rz_ref, shift_ref, s0_ref, mu_ref, w0_ref, wup_ref, a0_ref, aup_ref, kk_ref, ka_ref, rk_ref,
                 gng_ref, gnb_ref, y_ref, sout_ref, st, carry, *, t_valid):
    c = pl.program_id(1)
    cl_rows = rz_ref.shape[1]

    @pl.when(c == 0)
    def _():
        st[...] = s0_ref[0]
        carry[...] = shift_ref[0]

    z = rz_ref[0]
    ridx = lax.broadcasted_iota(jnp.int32, (cl_rows, 1), 0)
    prev = jnp.where(ridx == 0, carry[...], pltpu.roll(z, 1, 0))
    carry[...] = z[cl_rows - 1:cl_rows]
    zs = z + (prev - z) * mu_ref[...]
    r = zs[:, 0:W_R]
    k = zs[:, W_R:2 * W_R]
    v = zs[:, 2 * W_R:3 * W_R]
    wd = zs[:, 3 * W_R:3 * W_R + R_W]
    ad = zs[:, 3 * W_R + R_W:C_R]
    wpre = w0_ref[...] + _hdot(jnp.tanh(wd), wup_ref[...])
    w = -(jnp.maximum(-wpre, 0.0) + jnp.log(1.0 + jnp.exp(-jnp.abs(wpre)))) - 0.5
    ld = -jnp.exp(w)
    a = _sigmoid(a0_ref[...] + _hdot(ad, aup_ref[...]))
    kkn = k * kk_ref[...]
    k2 = k * (1.0 + (a - 1.0) * ka_ref[...])
    if t_valid < cl_rows:
        live = ridx < t_valid
        ld = jnp.where(live, ld, 0.0)
        kkn = jnp.where(live, kkn, 0.0)
        k2 = jnp.where(live, k2, 0.0)
    ti = lax.broadcasted_iota(jnp.int32, (cl_rows, cl_rows), 0)
    tj = lax.broadcasted_iota(jnp.int32, (cl_rows, cl_rows), 1)
    incl = tj <= ti
    strict = tj < ti
    tri = jnp.where(incl, 1.0, 0.0).astype(BF16)
    ld_h = ld.astype(BF16)
    ld_m = (ld - ld_h.astype(F32)).astype(BF16)
    ld_l = (ld - ld_h.astype(F32) - ld_m.astype(F32)).astype(BF16)
    cum = _dg(tri, ld_h) + (_dg(tri, ld_m) + _dg(tri, ld_l))
    n_levels = max(1, int(np.ceil(np.log2(cl_rows))))
    heads = range(R_HEADS)
    hsl = [slice(h * R_HD, (h + 1) * R_HD) for h in heads]
    ar, bk, bk_end, v_s, s_old, r_l, k_l, v_l, dec_end = [], [], [], [], [], [], [], [], []
    for h in heads:
        hs = hsl[h]
        cum_h = cum[:, hs]
        kk_h = kkn[:, hs]
        kk_h = kk_h / jnp.maximum(jnp.sqrt(jnp.sum(kk_h * kk_h, axis=-1, keepdims=True)), 1e-12)
        r_h, k_h, v_h, a_h = r[:, hs], k2[:, hs], v[:, hs], a[:, hs]
        p_in = jnp.exp(cum_h)
        p_ex = jnp.exp(cum_h - ld[:, hs])
        p_inv = jnp.exp(-cum_h)
        cum_end = cum_h[cl_rows - 1:cl_rows]
        p_end = jnp.exp(cum_end - cum_h)
        ar.append(_split(jnp.concatenate([-kk_h * p_ex, r_h * p_in], axis=0)))
        bk.append(_split(jnp.concatenate([kk_h * a_h * p_inv, k_h * p_inv], axis=0)))
        bk_end.append(_split(jnp.concatenate([kk_h * a_h * p_end, k_h * p_end], axis=0)))
        v_s.append(_split(v_h))
        s_old.append(st[h])
        r_l.append(r_h)
        k_l.append(k_h)
        v_l.append(v_h)
        dec_end.append(jnp.exp(cum_end))
    mm = [_hdot_s(ar[h], bk[h], NT) for h in heads]
    gs = [_hdot_s(ar[h], _split(s_old[h]), NT) for h in heads]
    lp = [jnp.where(strict, mm[h][0:cl_rows, 0:cl_rows], 0.0) for h in heads]
    m_ak = [_split(jnp.where(strict, mm[h][0:cl_rows, cl_rows:], 0.0)) for h in heads]
    m_rb = [_split(jnp.where(incl, mm[h][cl_rows:, 0:cl_rows], 0.0)) for h in heads]
    m_rk = [_split(jnp.where(incl, mm[h][cl_rows:, cl_rows:], 0.0)) for h in heads]
    u = [gs[h][0:cl_rows] + _hdot_s(m_ak[h], v_s[h]) for h in heads]
    for lvl in range(n_levels):
        lp_s = [_split(lp[h]) for h in heads]
        u = [u[h] + _hdot_s(lp_s[h], _split(u[h])) for h in heads]
        if lvl + 1 < n_levels:
            lp = [_hdot_s(lp_s[h], lp_s[h]) for h in heads]
    u_s = [_split(u[h]) for h in heads]
    y_l = [gs[h][cl_rows:] + _hdot_s(m_rb[h], u_s[h]) + _hdot_s(m_rk[h], v_s[h]) for h in heads]
    for h in heads:
        uv = _split(jnp.concatenate([u[h], v_l[h]], axis=0))
        st[h] = s_old[h] * dec_end[h] + _hdot_s(uv, bk_end[h], TN)
    for h in heads:
        hs = hsl[h]
        y = y_l[h]
        mean = jnp.mean(y, axis=-1, keepdims=True)
        var = jnp.mean(jnp.square(y - mean), axis=-1, keepdims=True)
        yn = (y - mean) * lax.rsqrt(var + GN_EPS) * gng_ref[:, hs] + gnb_ref[:, hs]
        bonus = jnp.sum(r_l[h] * k_l[h] * rk_ref[:, hs], axis=-1, keepdims=True) * v_l[h]
        y_ref[0, :, hs] = yn + bonus

    @pl.when(c == pl.num_programs(1) - 1)
    def _():
        sout_ref[0] = st[...]


def _rwkv(rz, shift0, state0, params, chunk, t_valid):
    b, t = rz.shape[:2]
    vec = lambda n: _const_spec((1, n))
    return pl.pallas_call(
        functools.partial(_rwkv_kernel, t_valid=t_valid),
        grid=(b, t // chunk),
        in_specs=[pl.BlockSpec((1, chunk, C_R), lambda bi, c: (bi, c, 0)),
                  pl.BlockSpec((1, 1, C_R), lambda bi, c: (bi, 0, 0)),
                  pl.BlockSpec((1, R_HEADS, R_HD, R_HD), lambda bi, c: (bi, 0, 0, 0)),
                  vec(C_R), vec(W_R), _const_spec((R_W, W_R)), vec(W_R), _const_spec((R_A, W_R)),
                  vec(W_R), vec(W_R), vec(W_R), vec(W_R), vec(W_R)],
        out_specs=[pl.BlockSpec((1, chunk, W_R), lambda bi, c: (bi, c, 0)),
                   pl.BlockSpec((1, R_HEADS, R_HD, R_HD), lambda bi, c: (bi, 0, 0, 0))],
        out_shape=[jax.ShapeDtypeStruct((b, t, W_R), F32),
                   jax.ShapeDtypeStruct((b, R_HEADS, R_HD, R_HD), F32)],
        scratch_shapes=[pltpu.VMEM((R_HEADS, R_HD, R_HD), F32), pltpu.VMEM((1, C_R), F32)],
        compiler_params=pltpu.CompilerParams(dimension_semantics=("parallel", "arbitrary"),
                                             vmem_limit_bytes=VMEM_LIMIT),
        name="rwkv",
    )(rz, shift0.reshape(b, 1, C_R), state0, *params)


def _rope_tables(pos):
    half = HD // 2
    inv = ROPE_THETA ** (-jnp.arange(half, dtype=F32) / half)
    ang = pos.astype(F32)[:, None] * inv[None, :]
    cos, sin = jnp.cos(ang), jnp.sin(ang)
    reps = LANES // HD
    return (jnp.tile(jnp.concatenate([cos, cos], axis=-1), (1, reps)),
            jnp.tile(jnp.concatenate([-sin, sin], axis=-1), (1, reps)))


def _pick_tile(n, cap):
    t = cap
    while n % t:
        t //= 2
    return t


def kernel(x_prompt, x_sample, p_prompt, p_sample, cache_cmp_kv, cache_sel_kv, cache_win_kv, state_wkv, state_shift, page_table, norm_g, w_in, cmp_pe, cmp_w1, cmp_b1, cmp_w2, cmp_b2, w_pa, rwkv_mu, rwkv_w0, rwkv_w_up, rwkv_a0, rwkv_a_up, rwkv_k_k, rwkv_k_a, rwkv_r_k, rwkv_gn_g, rwkv_gn_b, w_pb, w_out, ple_norm_g, w_ple_gate, w_ple_proj, final_norm_g):
    b, s_len = x_prompt.shape[:2]
    db, ds = x_sample.shape[:2]
    depth = norm_g.shape[0]
    assert depth == 1, "single-layer trunk"
    n_pages = page_table.shape[1]
    past = n_pages * PAGE_SIZE
    win_buf = cache_win_kv.shape[2]
    n_pool = cache_cmp_kv.shape[1]
    i = 0

    w = w_in[i]
    o_q, o_kv, o_ng, o_ag = 0, W_A, W_A + 6 * KV_W, W_A + 6 * KV_W + 3 * NSA_HEADS
    w_pack = jnp.concatenate([w[:, o_q:o_ng], w[:, o_ag:], w[:, o_ng:o_ag],
                              jnp.zeros((D_MODEL, C_PACK - w.shape[1]), w.dtype)], axis=1).astype(BF16)
    cw = _compress_weights(cmp_pe[i], cmp_w1[i], cmp_b1[i], cmp_w2[i], cmp_b2[i])
    row = lambda u: u.reshape(1, -1)
    rw = (row(rwkv_mu[i]), row(rwkv_w0[i]), rwkv_w_up[i], row(rwkv_a0[i]), rwkv_a_up[i], row(rwkv_k_k[i]),
          row(rwkv_k_a[i]), row(rwkv_r_k[i]), row(rwkv_gn_g[i]), row(rwkv_gn_b[i]))
    ow = (w_pa[i].astype(BF16), w_pb[i].astype(BF16), w_out[i].astype(BF16), row(ple_norm_g[i]),
          w_ple_gate[i].astype(BF16), w_ple_proj[i].astype(BF16), row(final_norm_g))

    tm = _pick_tile(s_len, 256)
    cos_p, sin_p = _rope_tables(jnp.arange(s_len, dtype=jnp.int32))
    xp2 = x_prompt.reshape(b * s_len, D_MODEL)
    q, cmp_p, sel_p, win_p, gates, sa, rz, sr, mg = _project(xp2, row(norm_g[i]), w_pack, cos_p, sin_p, tm, True)
    kc_p = _compress_seq(cmp_p, cw)
    seq = lambda u: u.reshape(b, s_len, u.shape[-1])
    o_a = _nsa_prompt(seq(q), seq(gates), kc_p, sel_p, win_p, tq=128, tk=min(512, s_len))
    chunk = 64
    o_b, st_p = _rwkv(seq(rz), jnp.zeros((b, C_R), F32), jnp.zeros((b, R_HEADS, R_HD, R_HD), F32), rw, chunk, chunk)
    y_p = _merge(xp2, o_a.reshape(b * s_len, W_A), o_b.reshape(b * s_len, W_R), sa, sr, mg,
                 p_prompt[i].reshape(b * s_len, D_PLE), *ow, tm=tm)
    kv_shape = lambda u, n, t: u.reshape(n, t, 2, NSA_GROUPS, HD)
    from_cm = lambda u: jnp.transpose(u.reshape(u.shape[0], 2, NSA_GROUPS, HD, u.shape[2]), (0, 4, 1, 2, 3))
    to_cm = lambda u: jnp.transpose(u, (0, 2, 3, 4, 1)).reshape(u.shape[0], 2 * KV_W, u.shape[1])
    cmp_kv_p = from_cm(cmp_p)
    sel_kv_p = from_cm(sel_p)
    win_kv_p = from_cm(win_p[:, :, s_len - min(WIN, s_len):])
    shift_p = seq(rz)[:, -1]

    n_s = db * ds
    cos_s, sin_s = _rope_tables(jnp.tile(past + jnp.arange(ds, dtype=jnp.int32), db))
    xs2 = x_sample.reshape(n_s, D_MODEL)
    q, cmp_s, sel_s, win_s, gates, sa, rz, sr, mg = _project(xs2, row(norm_g[i]), w_pack, cos_s, sin_s, n_s, False)
    kc_s = _compress_paged(page_table, to_cm(cache_cmp_kv[i]), cw, n_pages=_pick_tile(n_pages, 32))
    seqs = lambda u: u.reshape(db, ds, u.shape[-1])
    o_a = _nsa_sample(page_table, seqs(q), seqs(gates), kc_s, to_cm(cache_sel_kv[i]),
                      seqs(sel_s), to_cm(cache_win_kv[i]), seqs(win_s), n_pages=_pick_tile(n_pages, 32))
    pad_t = 16
    rz_pad = jnp.pad(seqs(rz), ((0, 0), (0, pad_t - ds), (0, 0)))
    o_b, st_s = _rwkv(rz_pad, state_shift[i], state_wkv[i], rw, pad_t, ds)
    y_s = _merge(xs2, o_a.reshape(n_s, W_A), o_b[:, :ds].reshape(n_s, W_R), sa, sr, mg,
                 p_sample[i].reshape(n_s, D_PLE), *ow, tm=n_s)
    win_kv_s = jnp.concatenate([cache_win_kv[i], kv_shape(win_s, db, ds)], axis=1)[:, ds:]
    shift_s = seqs(rz)[:, -1]

    return (y_p.reshape(b, s_len, D_MODEL), y_s.reshape(db, ds, D_MODEL),
            cmp_kv_p[None], kv_shape(cmp_s, db, ds)[None], sel_kv_p[None], kv_shape(sel_s, db, ds)[None],
            win_kv_p[None], win_kv_s[None], st_p[None], st_s[None], shift_p[None], shift_s[None])
```

```python
import functools

import numpy as np
import jax
import jax.numpy as jnp
from jax import lax
from jax.experimental import pallas as pl
from jax.experimental.pallas import tpu as pltpu

F32 = jnp.float32
BF16 = jnp.bfloat16

D_MODEL = 1024
D_PLE = 256
NSA_HEADS = 8
NSA_GROUPS = 2
HD = 64
HPG = NSA_HEADS // NSA_GROUPS
W_A = NSA_HEADS * HD
KV_W = NSA_GROUPS * HD
L_CMP = 32
D_CMP = 16
CMP_HID = 128
L_SEL = 64
N_SEL = 16
WIN = 512
FORCE_SCORE = 1e4
NEG_INF = -1e30
R_HEADS = 8
R_HD = 64
W_R = R_HEADS * R_HD
R_W = 64
R_A = 64
C_R = 3 * W_R + R_W + R_A
GN_EPS = 64e-5
ROPE_THETA = 10000.0
NORM_EPS = 1e-6
PAGE_SIZE = 128

LANES = 128
CHUNKS_PER_PAGE = PAGE_SIZE // D_CMP
CHUNK_W = D_CMP * 2 * KV_W
VMEM_LIMIT = 56 * 1024 * 1024

C_Q = 0
C_KV = 512
C_AG = 1280
C_RZ = 1792
C_RG = 3456
C_MG = 3968
C_NG = 6016
C_PACK = 6144

NN = ((1,), (0,))
NT = ((1,), (1,))
TN = ((0,), (0,))


def _dg(a, b, dims=NN):
    return lax.dot_general(a, b, (dims, ((), ())), preferred_element_type=F32)


def _split(x):
    hi = x.astype(BF16)
    lo = (x - hi.astype(F32)).astype(BF16)
    return hi, lo


def _hdot_s(a, b, dims=NN):
    (ah, al), (bh, bl) = a, b
    return _dg(ah, bh, dims) + (_dg(ah, bl, dims) + _dg(al, bh, dims))


def _hdot(a, b, dims=NN):
    return _hdot_s(_split(a), _split(b), dims)


def _sigmoid(x):
    return 1.0 / (1.0 + jnp.exp(-x))


def _silu(x):
    return x * _sigmoid(x)


def _rms(x, g):
    return x * lax.rsqrt(jnp.mean(x * x, axis=-1, keepdims=True) + NORM_EPS) * g


def _const_spec(shape):
    zeros = (0,) * len(shape)
    return pl.BlockSpec(shape, lambda *_: zeros)


def _rope128(z, cos, sin, first):
    partner = jnp.where(first, pltpu.roll(z, LANES - HD // 2, 1), pltpu.roll(z, HD // 2, 1))
    return z * cos + partner * sin


def _proj_kernel(x_ref, g_ref, w_ref, cos_ref, sin_ref,
                 q_ref, cmp_ref, sel_ref, win_ref, gate_ref, sa_ref, rz_ref, sr_ref, mg_ref, *aug_refs,
                 seq_major, n_pos_blocks):
    hb = _rms(x_ref[...], g_ref[...]).astype(BF16)
    cos = cos_ref[...]
    sin = sin_ref[...]
    tm = cos.shape[0]
    lane = lax.broadcasted_iota(jnp.int32, cos.shape, 1)
    first = (lane % HD) < (HD // 2)

    def mm(a, b):
        return jnp.dot(hb, w_ref[:, a:b], preferred_element_type=F32)

    zq = mm(C_Q, C_Q + W_A)
    q = jnp.concatenate([_rope128(zq[:, c * LANES:(c + 1) * LANES], cos, sin, first) for c in range(W_A // LANES)],
                        axis=1)
    gates = _sigmoid(mm(C_NG, C_PACK))
    if seq_major:
        q_ref[0] = q.T
        gate_ref[0] = gates.T
    else:
        q_ref[...] = q
        gate_ref[...] = gates
    for i, ref in enumerate((cmp_ref, sel_ref, win_ref)):
        zkv = mm(C_KV + 2 * KV_W * i, C_KV + 2 * KV_W * (i + 1))
        k_rot = _rope128(zkv[:, 0:KV_W], cos, sin, first)
        if seq_major:
            ref[0] = jnp.concatenate([k_rot, zkv[:, KV_W:2 * KV_W]], axis=1).T
            if i > 0:
                aug_ref = aug_refs[i - 1]
                if i == 1:
                    t_in_seq = ((pl.program_id(0) % n_pos_blocks) * tm
                                + lax.broadcasted_iota(jnp.int32, cos.shape, 0))
                    tail = jnp.where(jnp.right_shift(t_in_seq, int(np.log2(L_SEL))) == lane - HD, 1.0, 0.0)
                else:
                    tail = jnp.zeros(cos.shape, F32)
                for g in range(NSA_GROUPS):
                    k_g = k_rot if g == 0 else pltpu.roll(k_rot, LANES - g * HD, 1)
                    aug_ref[0, g] = jnp.where(lane < HD, k_g, tail).astype(BF16)
        else:
            ref[:, 0:KV_W] = k_rot
            ref[:, KV_W:2 * KV_W] = zkv[:, KV_W:2 * KV_W]
    sa_ref[...] = _silu(mm(C_AG, C_AG + W_A))
    rz_ref[...] = mm(C_RZ, C_RZ + C_R)
    sr_ref[...] = _silu(mm(C_RG, C_RG + W_R))
    mg_ref[...] = _sigmoid(mm(C_MG, C_MG + 2 * D_MODEL))


def _project(x2d, norm_g, w_pack, cos_t, sin_t, tm, seq_major):
    n = x2d.shape[0]
    seq_len = cos_t.shape[0]
    n_pos_blocks = seq_len // tm
    n_seq = n // seq_len
    widths = (W_A, 2 * KV_W, 2 * KV_W, 2 * KV_W, LANES, W_A, C_R, W_R, 2 * D_MODEL)
    row = lambda w: pl.BlockSpec((tm, w), lambda i: (i, 0))
    pos = pl.BlockSpec((tm, LANES), lambda i: (i % n_pos_blocks, 0))
    out_specs = [row(w) for w in widths]
    out_shape = [jax.ShapeDtypeStruct((n, w), F32) for w in widths]
    in_specs = [row(D_MODEL), _const_spec((1, D_MODEL)), _const_spec((D_MODEL, C_PACK)), pos, pos]
    args = [x2d, norm_g, w_pack, cos_t, sin_t]
    if seq_major:
        assert seq_len // L_SEL <= LANES - HD, "selection-block one-hot must fit beside the key"
        for o in range(5):
            out_specs[o] = pl.BlockSpec((1, widths[o], tm), lambda i: (i // n_pos_blocks, 0, i % n_pos_blocks))
            out_shape[o] = jax.ShapeDtypeStruct((n_seq, widths[o], seq_len), F32)
        for _ in range(2):
            out_specs.append(pl.BlockSpec((1, NSA_GROUPS, tm, LANES),
                                          lambda i: (i // n_pos_blocks, 0, i % n_pos_blocks, 0)))
            out_shape.append(jax.ShapeDtypeStruct((n_seq, NSA_GROUPS, seq_len, LANES), BF16))
    return pl.pallas_call(
        functools.partial(_proj_kernel, seq_major=seq_major, n_pos_blocks=n_pos_blocks),
        grid=(n // tm,),
        in_specs=in_specs,
        out_specs=out_specs,
        out_shape=out_shape,
        compiler_params=pltpu.CompilerParams(dimension_semantics=("parallel",), vmem_limit_bytes=VMEM_LIMIT),
        name="project",
    )(*args)


def _merge_kernel(x_ref, oa_ref, ob_ref, sa_ref, sr_ref, mg_ref, p_ref,
                  wpa_ref, wpb_ref, wout_ref, pg_ref, wgate_ref, wproj_ref, fg_ref, y_ref):
    bdot = lambda a, w_ref: jnp.dot(a.astype(BF16), w_ref[...], preferred_element_type=F32)
    ya = bdot(oa_ref[...] * sa_ref[...], wpa_ref)
    yb = bdot(ob_ref[...] * sr_ref[...], wpb_ref)
    m = mg_ref[:, 0:D_MODEL] * ya + mg_ref[:, D_MODEL:2 * D_MODEL] * yb
    x2 = x_ref[...] + bdot(m, wout_ref)
    gate = _sigmoid(bdot(_rms(x2, pg_ref[...]), wgate_ref))
    x3 = x2 + gate * bdot(p_ref[...], wproj_ref)
    y_ref[...] = _rms(x3, fg_ref[...])


def _merge(x2d, oa, ob, sa, sr, mg, p2d, wpa, wpb, wout, pg, wgate, wproj, fg, tm):
    n = x2d.shape[0]
    row = lambda w: pl.BlockSpec((tm, w), lambda i: (i, 0))
    return pl.pallas_call(
        _merge_kernel,
        grid=(n // tm,),
        in_specs=[row(D_MODEL), row(W_A), row(W_R), row(W_A), row(W_R), row(2 * D_MODEL), row(D_PLE),
                  _const_spec((W_A, D_MODEL)), _const_spec((W_R, D_MODEL)), _const_spec((D_MODEL, D_MODEL)),
                  _const_spec((1, D_MODEL)), _const_spec((D_MODEL, D_MODEL)), _const_spec((D_PLE, D_MODEL)),
                  _const_spec((1, D_MODEL))],
        out_specs=row(D_MODEL),
        out_shape=jax.ShapeDtypeStruct((n, D_MODEL), F32),
        compiler_params=pltpu.CompilerParams(dimension_semantics=("parallel",), vmem_limit_bytes=VMEM_LIMIT),
        name="merge",
    )(x2d, oa, ob, sa, sr, mg, p2d, wpa, wpb, wout, pg, wgate, wproj, fg)


def _page_copies(pt_ref, pool_ref, buf, sem, b, t, slot, n_pages):
    return [pltpu.make_async_copy(pool_ref.at[pt_ref[b, t * n_pages + p]],
                                  buf.at[slot, :, pl.ds(p * PAGE_SIZE, PAGE_SIZE)], sem.at[slot])
            for p in range(n_pages)]


def _paged_tile(pt_ref, pool_ref, buf, sem, n_pages):
    b, t = pl.program_id(0), pl.program_id(1)
    nt = pl.num_programs(1)
    step = b * nt + t
    slot = step % 2

    @pl.when(step == 0)
    def _():
        for cp in _page_copies(pt_ref, pool_ref, buf, sem, b, t, slot, n_pages):
            cp.start()

    nxt = step + 1

    @pl.when(nxt < pl.num_programs(0) * nt)
    def _():
        for cp in _page_copies(pt_ref, pool_ref, buf, sem, nxt // nt, nxt % nt, 1 - slot, n_pages):
            cp.start()

    for cp in _page_copies(pt_ref, pool_ref, buf, sem, b, t, slot, n_pages):
        cp.wait()
    return slot


CMP_HALF = 2 * NSA_GROUPS * CMP_HID


def _compress_core(page_t, xs, perm_ref, w1_ref, pe_ref, b1_ref, w2_ref, b2_ref, carry, out_ref):
    n_pages = xs.shape[0]
    rows = n_pages * CHUNKS_PER_PAGE
    for p in range(n_pages):
        xp = _dg(perm_ref[...], page_t(p).astype(BF16), NT)
        xs[p] = xp.reshape(D_CMP, CHUNKS_PER_PAGE, 2 * KV_W)
    z = None
    for j in range(D_CMP):
        xj = jnp.concatenate([xs[:, j].reshape(rows, 2 * KV_W), pe_ref[j]], axis=0).astype(BF16)
        zj = jnp.dot(xj, w1_ref[j], preferred_element_type=F32)
        z = zj if z is None else z + zj
    pos = z[rows:rows + 1, 0:CMP_HALF] + z[rows + 1:rows + 2, CMP_HALF:]
    lo = z[0:rows, 0:CMP_HALF]
    ridx = lax.broadcasted_iota(jnp.int32, (rows, CMP_HALF), 0)
    lo_prev = jnp.where(ridx == 0, carry[...], pltpu.roll(lo, 1, 0))
    carry[...] = lo[rows - 1:rows]
    hid = lo_prev + z[0:rows, CMP_HALF:] + pos + b1_ref[...]
    out_ref[0] = jnp.dot(_silu(hid).astype(BF16), w2_ref[...], preferred_element_type=F32) + b2_ref[...]


def _compress_paged_kernel(pt_ref, pool_ref, perm_ref, w1_ref, pe_ref, b1_ref, w2_ref, b2_ref, out_ref,
                           buf, sem, xs, carry, *, n_pages):
    slot = _paged_tile(pt_ref, pool_ref, buf, sem, n_pages)

    @pl.when(pl.program_id(1) == 0)
    def _():
        carry[...] = jnp.zeros_like(carry)

    _compress_core(lambda p: buf[slot, :, p * PAGE_SIZE:(p + 1) * PAGE_SIZE], xs, perm_ref, w1_ref, pe_ref, b1_ref,
                   w2_ref, b2_ref, carry, out_ref)


def _compress_seq_kernel(x_ref, perm_ref, w1_ref, pe_ref, b1_ref, w2_ref, b2_ref, out_ref, xs, carry):
    carry[...] = jnp.zeros_like(carry)
    _compress_core(lambda p: x_ref[0, :, p * PAGE_SIZE:(p + 1) * PAGE_SIZE], xs, perm_ref, w1_ref, pe_ref, b1_ref,
                   w2_ref, b2_ref, carry, out_ref)


def _compress_specs(cst):
    return [cst((PAGE_SIZE, PAGE_SIZE)), cst((D_CMP, 2 * KV_W, 2 * CMP_HALF)), cst((D_CMP, 8, 2 * KV_W)),
            cst((1, CMP_HALF)), cst((CMP_HALF, 2 * KV_W)), cst((1, 2 * KV_W))]


def _compress_scratch(n_pages):
    return [pltpu.VMEM((n_pages, D_CMP, CHUNKS_PER_PAGE, 2 * KV_W), F32), pltpu.VMEM((1, CMP_HALF), F32)]


def _compress_paged(page_table, pool, cw, n_pages):
    nb, pages_per_seq = page_table.shape
    nt = pages_per_seq // n_pages
    toks = n_pages * PAGE_SIZE
    rows = toks // D_CMP
    cst = lambda shape: pl.BlockSpec(shape, lambda b, t, pt: (0,) * len(shape))
    return pl.pallas_call(
        functools.partial(_compress_paged_kernel, n_pages=n_pages),
        grid_spec=pltpu.PrefetchScalarGridSpec(
            num_scalar_prefetch=1,
            grid=(nb, nt),
            in_specs=[pl.BlockSpec(memory_space=pl.ANY)] + _compress_specs(cst),
            out_specs=pl.BlockSpec((1, rows, 2 * KV_W), lambda b, t, pt: (b, t, 0)),
            scratch_shapes=[pltpu.VMEM((2, 2 * KV_W, toks), F32), pltpu.SemaphoreType.DMA((2,))]
            + _compress_scratch(n_pages)),
        out_shape=jax.ShapeDtypeStruct((nb, nt * rows, 2 * KV_W), F32),
        compiler_params=pltpu.CompilerParams(dimension_semantics=("arbitrary", "arbitrary"),
                                             vmem_limit_bytes=VMEM_LIMIT),
        name="compress_paged",
    )(page_table, pool, *cw)


def _compress_seq(x_t, cw):
    nb, _, toks = x_t.shape
    rows = toks // D_CMP
    return pl.pallas_call(
        _compress_seq_kernel,
        grid=(nb,),
        in_specs=[pl.BlockSpec((1, 2 * KV_W, toks), lambda b: (b, 0, 0))] + _compress_specs(_const_spec),
        out_specs=pl.BlockSpec((1, rows, 2 * KV_W), lambda b: (b, 0, 0)),
        out_shape=jax.ShapeDtypeStruct((nb, rows, 2 * KV_W), F32),
        scratch_shapes=_compress_scratch(toks // PAGE_SIZE),
        compiler_params=pltpu.CompilerParams(dimension_semantics=("parallel",), vmem_limit_bytes=VMEM_LIMIT),
        name="compress_seq",
    )(x_t, *cw)


def _compress_weights(cmp_pe, cmp_w1, cmp_b1, cmp_w2, cmp_b2):
    eye_kv = jnp.eye(2, dtype=F32)
    eye_g = jnp.eye(NSA_GROUPS, dtype=F32)
    w1 = cmp_w1.reshape(2, 2, D_CMP, HD, CMP_HID)
    w1j = jnp.einsum('kljdf,kK,gG->jkgdlKGf', w1, eye_kv, eye_g).reshape(D_CMP, 2 * KV_W, 2 * CMP_HALF)
    pe = cmp_pe.reshape(2, 2, D_CMP, HD)
    pe = jnp.broadcast_to(jnp.transpose(pe, (2, 1, 0, 3))[:, :, :, None, :], (D_CMP, 2, 2, NSA_GROUPS, HD))
    pej = jnp.pad(pe.reshape(D_CMP, 2, 2 * KV_W), ((0, 0), (0, 6), (0, 0)))
    b1big = jnp.broadcast_to(cmp_b1[:, None, :], (2, NSA_GROUPS, CMP_HID)).reshape(1, -1)
    w2big = jnp.einsum('kfd,kK,gG->kgfKGd', cmp_w2, eye_kv, eye_g).reshape(CMP_HALF, 2 * KV_W)
    b2big = jnp.broadcast_to(cmp_b2[:, None, :], (2, NSA_GROUPS, HD)).reshape(1, -1)
    dst = np.arange(PAGE_SIZE)
    perm = (dst[:, None] % CHUNKS_PER_PAGE) * D_CMP + dst[:, None] // CHUNKS_PER_PAGE == dst[None, :]
    return jnp.asarray(perm, dtype=BF16), w1j.astype(BF16), pej, b1big, w2big.astype(BF16), b2big


def _overlap_matrix(n_rows, n_cols):
    c_start = (np.arange(n_rows) - 1) * D_CMP
    s_start = np.arange(n_cols) * L_SEL
    ov = (c_start[:, None] <= s_start[None, :] + L_SEL - 1) & (c_start[:, None] + L_CMP - 1 >= s_start[None, :])
    ov &= (np.arange(n_rows) >= 1)[:, None]
    return ov.astype(np.float32)


def _online(state, s, msk, v_bf, v_dims=NN):
    m, l, acc = state
    if msk is not None:
        s = jnp.where(msk, s, NEG_INF)
    m_new = jnp.maximum(m, jnp.max(s, axis=-1, keepdims=True))
    alpha = jnp.exp(m - m_new)
    p = jnp.exp(s - m_new)
    if msk is not None:
        p = jnp.where(msk, p, 0.0)
    l = alpha * l + jnp.sum(p, axis=-1, keepdims=True)
    acc = alpha * acc + _dg(p.astype(BF16), v_bf, v_dims)
    return m_new, l, acc


def _online_init(rows):
    return jnp.full((rows, 1), NEG_INF, F32), jnp.zeros((rows, 1), F32), jnp.zeros((rows, HD), F32)


def _online_out(state):
    _, l, acc = state
    return acc / jnp.maximum(l, 1e-30)


def _stack_heads(q, g):
    return jnp.concatenate([q[:, (HPG * g + h) * HD:(HPG * g + h + 1) * HD] for h in range(HPG)], axis=0) * (HD ** -0.5)


def _compressed_branch(qs, kc, vc, valid, n_tok):
    s = jnp.where(valid, _hdot(qs, kc, NT), NEG_INF)
    m = jnp.max(s, axis=-1, keepdims=True)
    p = jnp.where(valid, jnp.exp(s - m), 0.0)
    p = p / jnp.maximum(jnp.sum(p, axis=-1, keepdims=True), 1e-30)
    o = jnp.dot(p.astype(BF16), vc.astype(BF16), preferred_element_type=F32)
    psum = p[0:n_tok]
    for h in range(1, HPG):
        psum = psum + p[h * n_tok:(h + 1) * n_tok]
    return o, psum


def _select_blocks(imp, tpos, n_real):
    blk = lax.broadcasted_iota(jnp.int32, imp.shape, 1)
    cur = tpos // L_SEL
    forced = (blk == 0) | (blk == cur) | (blk == cur - 1)
    imp = jnp.where(forced, FORCE_SCORE, imp)
    imp = jnp.where(blk * L_SEL > tpos, -1.0, imp)
    imp = jnp.where(blk >= n_real, -2.0, imp)
    cnt = jnp.zeros(imp.shape, F32)
    for mcol in range(n_real):
        col = imp[:, mcol:mcol + 1]
        ahead = (col > imp) | ((col == imp) & (blk > mcol))
        cnt = cnt + jnp.where(ahead, 1.0, 0.0)
    return jnp.where(cnt < N_SEL, 1.0, 0.0)


def _select_blocks_t(imp_t, tpos_row, n_real):
    sub = 8
    blk = lax.broadcasted_iota(jnp.int32, imp_t.shape, 0)
    cur = jnp.right_shift(tpos_row, int(np.log2(L_SEL)))
    forced = (blk == 0) | (blk == cur) | (blk == cur - 1)
    imp_t = jnp.where(forced, FORCE_SCORE, imp_t)
    imp_t = jnp.where(blk * L_SEL > tpos_row, -1.0, imp_t)
    slabs = [imp_t[s0:s0 + sub] for s0 in range(0, n_real, sub)]
    cnt = [jnp.zeros(s.shape, F32) for s in slabs]
    sub_iota = lax.broadcasted_iota(jnp.int32, slabs[0].shape, 0)
    for m in range(n_real):
        row = jnp.broadcast_to(imp_t[m:m + 1], slabs[0].shape)
        for si, slab in enumerate(slabs):
            ge = jnp.where(row >= slab, 1.0, 0.0)
            gt = jnp.where(row > slab, 1.0, 0.0)
            if si * sub > m:
                ahead = ge
            elif si * sub + sub - 1 < m:
                ahead = gt
            else:
                ahead = jnp.where(sub_iota > m - si * sub, ge, gt)
            cnt[si] = cnt[si] + ahead
    return jnp.where(jnp.concatenate(cnt, axis=0) < N_SEL, 1.0, 0.0)


def _online_t(state, s_t, msk, v_t):
    m, l, acc = state
    if msk is not None:
        s_t = jnp.where(msk, s_t, NEG_INF)
    m_new = jnp.maximum(m, jnp.max(s_t, axis=0, keepdims=True))
    alpha = jnp.exp(m - m_new)
    p = jnp.exp(s_t - m_new)
    if msk is not None:
        p = jnp.where(msk, p, 0.0)
    l = alpha * l + jnp.sum(p, axis=0, keepdims=True)
    acc = alpha * acc + _dg(v_t, p.astype(BF16))
    return m_new, l, acc


def _sweep_t(units, states):
    states = list(states)
    s_t = {0: units[0][1]()}
    pending = None

    def finish(k, alpha, p, values):
        m, l, acc = states[k]
        states[k] = (m, l, alpha * acc + _dg(values(), p))

    for u, (k, _, msk, values) in enumerate(units):
        if u + 1 < len(units):
            s_t[u + 1] = units[u + 1][1]()
        m, l, acc = states[k]
        s = s_t.pop(u)
        if msk is not None:
            s = jnp.where(msk, s, NEG_INF)
        m_new = jnp.maximum(m, jnp.max(s, axis=0, keepdims=True))
        alpha = jnp.exp(m - m_new)
        p = jnp.exp(s - m_new)
        states[k] = (m_new, alpha * l + jnp.sum(p, axis=0, keepdims=True), acc)
        if pending is not None:
            finish(*pending)
        pending = (k, alpha, p.astype(BF16), values)
    finish(*pending)
    return tuple(states)


def _online_t_init(cols):
    return jnp.full((1, cols), NEG_INF, F32), jnp.zeros((1, cols), F32), jnp.zeros((HD, cols), F32)


def _nsa_prompt_kernel(qt_ref, gatet_ref, kc_ref, ksel_ref, vsel_ref, kwin_ref, vwin_ref, ovt_ref, o_ref, *, tq, tk):
    i = pl.program_id(1)
    n_cmp = kc_ref.shape[1]
    n_sb = ovt_ref.shape[0]
    cols = HPG * tq
    aug = LANES - HD
    t0 = i * tq
    tpos_row = t0 + lax.broadcasted_iota(jnp.int32, (1, tq), 1)
    tpos4 = jnp.concatenate([tpos_row] * HPG, axis=1)
    n_full = t0 // tk
    w_start = pl.multiple_of(jnp.maximum(t0 - WIN, 0), tq)
    gates_t = gatet_ref[0]
    groups = range(NSA_GROUPS)
    n_w = WIN + tq
    dist = tpos4 - (w_start + lax.broadcasted_iota(jnp.int32, (n_w, 1), 0))
    ridx = lax.broadcasted_iota(jnp.int32, (n_cmp, cols), 0)
    valid = (ridx >= 1) & (D_CMP * ridx + (L_CMP - D_CMP - 1) <= tpos4)
    qa, o_c = [], []
    for g in groups:
        q_t = jnp.concatenate([qt_ref[0, (HPG * g + h) * HD:(HPG * g + h + 1) * HD, :] for h in range(HPG)],
                              axis=1) * (HD ** -0.5)
        s = jnp.where(valid, _hdot(kc_ref[0, :, g * HD:(g + 1) * HD], q_t), NEG_INF)
        p = jnp.where(valid, jnp.exp(s - jnp.max(s, axis=0, keepdims=True)), 0.0)
        p = p / jnp.maximum(jnp.sum(p, axis=0, keepdims=True), 1e-30)
        o_c.append(_dg(kc_ref[0, :, KV_W + g * HD:KV_W + (g + 1) * HD].astype(BF16), p.astype(BF16), TN))
        psum = p[:, 0:tq]
        for h in range(1, HPG):
            psum = psum + p[:, h * tq:(h + 1) * tq]
        sel_t = _select_blocks_t(_hdot(ovt_ref[...], psum), tpos_row, n_sb)
        bias = jnp.concatenate([(sel_t - 1.0) * (-NEG_INF)] * HPG, axis=1)
        if n_sb < aug:
            bias = jnp.concatenate([bias, jnp.zeros((aug - n_sb, cols), F32)], axis=0)
        qa.append(jnp.concatenate([q_t, bias], axis=0).astype(BF16))

    def sel_unit(g, j, causal):
        k0 = pl.multiple_of(j * tk, tk)
        kpos = k0 + lax.broadcasted_iota(jnp.int32, (tk, 1), 0)
        return (g, lambda: _dg(ksel_ref[0, g, pl.ds(k0, tk), :], qa[g]), (kpos <= tpos4) if causal else None,
                lambda: vsel_ref[0, g * HD:(g + 1) * HD, pl.ds(k0, tk)].astype(BF16))

    def win_unit(g):
        return (NSA_GROUPS + g, lambda: _dg(kwin_ref[0, g, pl.ds(w_start, n_w), :], qa[g]),
                (dist >= 0) & (dist < WIN),
                lambda: vwin_ref[0, g * HD:(g + 1) * HD, pl.ds(w_start, n_w)].astype(BF16))

    states = lax.fori_loop(
        0, n_full // 2,
        lambda jj, st: _sweep_t([sel_unit(g, 2 * jj + d, False) for d in range(2) for g in groups], st),
        tuple(_online_t_init(cols) for g in groups))
    states = lax.cond(n_full % 2 == 1, lambda st: _sweep_t([sel_unit(g, n_full - 1, False) for g in groups], st),
                      lambda st: st, states)
    states = _sweep_t([u for g in groups for u in (sel_unit(g, n_full, True), win_unit(g))],
                      states + tuple(_online_t_init(cols) for g in groups))
    heads_out = []
    for g in groups:
        o_s = _online_out(states[g])
        o_w = _online_out(states[NSA_GROUPS + g])
        for h in range(HPG):
            hd = HPG * g + h
            c = slice(h * tq, (h + 1) * tq)
            heads_out.append(gates_t[3 * hd:3 * hd + 1] * o_c[g][:, c] + gates_t[3 * hd + 1:3 * hd + 2] * o_s[:, c]
                             + gates_t[3 * hd + 2:3 * hd + 3] * o_w[:, c])
    o_ref[0] = jnp.concatenate(heads_out, axis=0).T


def _nsa_prompt(q_t, gates_t, kc, ksel, sel_t, kwin, win_t, tq, tk):
    b, _, t = q_t.shape
    n_cmp = kc.shape[1]
    n_sb = t // L_SEL
    ovt = jnp.asarray(_overlap_matrix(n_cmp, n_sb).T)
    keys = pl.BlockSpec((1, NSA_GROUPS, t, LANES), lambda bi, i: (bi, 0, 0, 0))
    values = pl.BlockSpec((1, KV_W, t), lambda bi, i: (bi, 1, 0))
    return pl.pallas_call(
        functools.partial(_nsa_prompt_kernel, tq=tq, tk=tk),
        grid=(b, t // tq),
        in_specs=[pl.BlockSpec((1, W_A, tq), lambda bi, i: (bi, 0, i)),
                  pl.BlockSpec((1, LANES, tq), lambda bi, i: (bi, 0, i)),
                  pl.BlockSpec((1, n_cmp, 2 * KV_W), lambda bi, i: (bi, 0, 0)),
                  keys, values, keys, values, _const_spec((n_sb, n_cmp))],
        out_specs=pl.BlockSpec((1, tq, W_A), lambda bi, i: (bi, i, 0)),
        out_shape=jax.ShapeDtypeStruct((b, t, W_A), F32),
        compiler_params=pltpu.CompilerParams(dimension_semantics=("parallel", "arbitrary"),
                                             vmem_limit_bytes=VMEM_LIMIT),
        name="nsa_prompt",
    )(q_t, gates_t, kc, ksel, sel_t, kwin, win_t, ovt)


def _nsa_sample_kernel(pt_ref, q_ref, gate_ref, kc_ref, pool_ref, snew_ref, wcache_ref, wnew_ref, ov_ref, ex_ref,
                       o_ref, buf, sem, sel_scr, oc_scr, m_scr, l_scr, acc_scr, *, n_pages, past):
    slot = _paged_tile(pt_ref, pool_ref, buf, sem, n_pages)
    t = pl.program_id(1)
    nt = pl.num_programs(1)
    ds = q_ref.shape[1]
    rows = HPG * ds
    n_cmp = kc_ref.shape[1]
    n_sbp = ov_ref.shape[1]
    blocks_per_tile = n_pages * PAGE_SIZE // L_SEL
    n_past_blocks = past // L_SEL
    keys = n_pages * PAGE_SIZE
    q = q_ref[0]
    tok4 = lax.broadcasted_iota(jnp.int32, (rows, 1), 0) % ds

    @pl.when(t == 0)
    def _():
        tpos = past + lax.broadcasted_iota(jnp.int32, (ds, 1), 0)
        for g in range(NSA_GROUPS):
            qs = _stack_heads(q, g)
            ridx = lax.broadcasted_iota(jnp.int32, (rows, n_cmp), 1)
            valid = (ridx >= 1) & (D_CMP * ridx + (L_CMP - D_CMP - 1) <= past + tok4)
            o_c, psum = _compressed_branch(qs, kc_ref[0, :, g * HD:(g + 1) * HD],
                                           kc_ref[0, :, KV_W + g * HD:KV_W + (g + 1) * HD], valid, ds)
            oc_scr[g] = o_c
            sel = _select_blocks(_hdot(psum, ov_ref[...]), tpos, n_past_blocks + 1)
            sel4 = jnp.concatenate([sel] * HPG, axis=0)
            for tt in range(sel_scr.shape[0]):
                sel_scr[tt, g] = sel4[:, tt * LANES:(tt + 1) * LANES]
            m_scr[g], l_scr[g], acc_scr[g] = _online_init(rows)

    kv_t = buf[slot]
    first_blk = t * blocks_per_tile
    ex = ex_ref[(first_blk % LANES) // blocks_per_tile]
    for g in range(NSA_GROUPS):
        qsb = _stack_heads(q, g).astype(BF16)
        bias = ((sel_scr[first_blk // LANES, g] - 1.0) * (-NEG_INF)).astype(BF16)
        qa = jnp.concatenate([qsb, bias], axis=1)
        ka = jnp.concatenate([kv_t[g * HD:(g + 1) * HD].astype(BF16), ex], axis=0)
        vt = kv_t[KV_W + g * HD:KV_W + (g + 1) * HD].astype(BF16)
        m_scr[g], l_scr[g], acc_scr[g] = _online((m_scr[g], l_scr[g], acc_scr[g]), _dg(qa, ka), None, vt, NT)

    @pl.when(t == nt - 1)
    def _():
        gates = gate_ref[0]
        knew_pos = lax.broadcasted_iota(jnp.int32, (rows, ds), 1)
        snew = snew_ref[0]
        wnew = wnew_ref[0]
        wc_t = wcache_ref[0]
        n_wc = wc_t.shape[1]
        for g in range(NSA_GROUPS):
            qsb = _stack_heads(q, g).astype(BF16)
            kcol = slice(g * HD, (g + 1) * HD)
            vcol = slice(KV_W + g * HD, KV_W + (g + 1) * HD)
            sel_new = sel_scr[n_past_blocks // LANES, g][:, n_past_blocks % LANES:n_past_blocks % LANES + 1]
            st = _online((m_scr[g], l_scr[g], acc_scr[g]), _dg(qsb, snew[:, kcol].astype(BF16), NT),
                         (sel_new > 0.5) & (knew_pos <= tok4), snew[:, vcol].astype(BF16))
            o_s = _online_out(st)
            dist_c = (n_wc + tok4) - lax.broadcasted_iota(jnp.int32, (rows, n_wc), 1)
            st = _online(_online_init(rows), _dg(qsb, wc_t[kcol].astype(BF16)),
                         (dist_c >= 0) & (dist_c < WIN), wc_t[vcol].astype(BF16), NT)
            dist_n = tok4 - knew_pos
            st = _online(st, _dg(qsb, wnew[:, kcol].astype(BF16), NT), (dist_n >= 0) & (dist_n < WIN),
                         wnew[:, vcol].astype(BF16))
            o_w = _online_out(st)
            o_c = oc_scr[g]
            for h in range(HPG):
                hd = HPG * g + h
                r = slice(h * ds, (h + 1) * ds)
                o_ref[0, :, hd * HD:(hd + 1) * HD] = (gates[:, 3 * hd:3 * hd + 1] * o_c[r]
                                                      + gates[:, 3 * hd + 1:3 * hd + 2] * o_s[r]
                                                      + gates[:, 3 * hd + 2:3 * hd + 3] * o_w[r])


def _nsa_sample(page_table, q, gates, kc, pool, sel_new, win_cache, win_new, n_pages):
    nb, ds = q.shape[:2]
    pages_per_seq = page_table.shape[1]
    past = pages_per_seq * PAGE_SIZE
    nt = pages_per_seq // n_pages
    n_cmp = kc.shape[1]
    n_sb = past // L_SEL + 1
    n_sbp = -(-n_sb // LANES) * LANES
    blocks_per_tile = n_pages * PAGE_SIZE // L_SEL
    keys = n_pages * PAGE_SIZE
    rows = HPG * ds
    ov = jnp.asarray(_overlap_matrix(n_cmp, n_sbp) * (np.arange(n_sbp) < n_sb)[None, :])
    ex = (np.arange(LANES)[None, :, None]
          == (np.arange(LANES // blocks_per_tile)[:, None, None] * blocks_per_tile + np.arange(keys)[None, None, :] // L_SEL))
    ex = jnp.asarray(ex, dtype=BF16)
    seq = lambda r, w: pl.BlockSpec((1, r, w), lambda b, t, pt: (b, 0, 0))
    cst = lambda shape: pl.BlockSpec(shape, lambda b, t, pt: (0,) * len(shape))
    return pl.pallas_call(
        functools.partial(_nsa_sample_kernel, n_pages=n_pages, past=past),
        grid_spec=pltpu.PrefetchScalarGridSpec(
            num_scalar_prefetch=1,
            grid=(nb, nt),
            in_specs=[seq(ds, W_A), seq(ds, LANES), seq(n_cmp, 2 * KV_W), pl.BlockSpec(memory_space=pl.ANY),
                      seq(ds, 2 * KV_W), seq(2 * KV_W, win_cache.shape[2]), seq(ds, 2 * KV_W),
                      cst((n_cmp, n_sbp)), cst(ex.shape)],
            out_specs=seq(ds, W_A),
            scratch_shapes=[pltpu.VMEM((2, 2 * KV_W, keys), F32),
                            pltpu.SemaphoreType.DMA((2,)),
                            pltpu.VMEM((n_sbp // LANES, NSA_GROUPS, rows, LANES), F32),
                            pltpu.VMEM((NSA_GROUPS, rows, HD), F32),
                            pltpu.VMEM((NSA_GROUPS, rows, 1), F32),
                            pltpu.VMEM((NSA_GROUPS, rows, 1), F32),
                            pltpu.VMEM((NSA_GROUPS, rows, HD), F32)]),
        out_shape=jax.ShapeDtypeStruct((nb, ds, W_A), F32),
        compiler_params=pltpu.CompilerParams(dimension_semantics=("arbitrary", "arbitrary"),
                                             vmem_limit_bytes=VMEM_LIMIT),
        name="nsa_sample",
    )(page_table, q, gates, kc, pool, sel_new, win_cache, win_new, ov, ex)


def _rwkv_kernel(rz_ref, shift_ref, s0_ref, mu_ref, w0_ref, wup_ref, a0_ref, aup_ref, kk_ref, ka_ref, rk_ref,
                 gng_ref, gnb_ref, y_ref, sout_ref, st, carry, *, t_valid):
    c = pl.program_id(1)
    cl_rows = rz_ref.shape[1]

    @pl.when(c == 0)
    def _():
        st[...] = s0_ref[0]
        carry[...] = shift_ref[0]

    z = rz_ref[0]
    ridx = lax.broadcasted_iota(jnp.int32, (cl_rows, 1), 0)
    prev = jnp.where(ridx == 0, carry[...], pltpu.roll(z, 1, 0))
    carry[...] = z[cl_rows - 1:cl_rows]
    zs = z + (prev - z) * mu_ref[...]
    r = zs[:, 0:W_R]
    k = zs[:, W_R:2 * W_R]
    v = zs[:, 2 * W_R:3 * W_R]
    wd = zs[:, 3 * W_R:3 * W_R + R_W]
    ad = zs[:, 3 * W_R + R_W:C_R]
    wpre = w0_ref[...] + _hdot(jnp.tanh(wd), wup_ref[...])
    w = -(jnp.maximum(-wpre, 0.0) + jnp.log(1.0 + jnp.exp(-jnp.abs(wpre)))) - 0.5
    ld = -jnp.exp(w)
    a = _sigmoid(a0_ref[...] + _hdot(ad, aup_ref[...]))
    kkn = k * kk_ref[...]
    k2 = k * (1.0 + (a - 1.0) * ka_ref[...])
    if t_valid < cl_rows:
        live = ridx < t_valid
        ld = jnp.where(live, ld, 0.0)
        kkn = jnp.where(live, kkn, 0.0)
        k2 = jnp.where(live, k2, 0.0)
    ti = lax.broadcasted_iota(jnp.int32, (cl_rows, cl_rows), 0)
    tj = lax.broadcasted_iota(jnp.int32, (cl_rows, cl_rows), 1)
    incl = tj <= ti
    strict = tj < ti
    tri = jnp.where(incl, 1.0, 0.0).astype(BF16)
    ld_h = ld.astype(BF16)
    ld_m = (ld - ld_h.astype(F32)).astype(BF16)
    ld_l = (ld - ld_h.astype(F32) - ld_m.astype(F32)).astype(BF16)
    cum = _dg(tri, ld_h) + (_dg(tri, ld_m) + _dg(tri, ld_l))
    n_levels = max(1, int(np.ceil(np.log2(cl_rows))))
    heads = range(R_HEADS)
    hsl = [slice(h * R_HD, (h + 1) * R_HD) for h in heads]
    ar, bk, bk_end, v_s, s_old, r_l, k_l, v_l, dec_end = [], [], [], [], [], [], [], [], []
    for h in heads:
        hs = hsl[h]
        cum_h = cum[:, hs]
        kk_h = kkn[:, hs]
        kk_h = kk_h / jnp.maximum(jnp.sqrt(jnp.sum(kk_h * kk_h, axis=-1, keepdims=True)), 1e-12)
        r_h, k_h, v_h, a_h = r[:, hs], k2[:, hs], v[:, hs], a[:, hs]
        p_in = jnp.exp(cum_h)
        p_ex = jnp.exp(cum_h - ld[:, hs])
        p_inv = jnp.exp(-cum_h)
        cum_end = cum_h[cl_rows - 1:cl_rows]
        p_end = jnp.exp(cum_end - cum_h)
        ar.append(_split(jnp.concatenate([-kk_h * p_ex, r_h * p_in], axis=0)))
        bk.append(_split(jnp.concatenate([kk_h * a_h * p_inv, k_h * p_inv], axis=0)))
        bk_end.append(_split(jnp.concatenate([kk_h * a_h * p_end, k_h * p_end], axis=0)))
        v_s.append(_split(v_h))
        s_old.append(st[h])
        r_l.append(r_h)
        k_l.append(k_h)
        v_l.append(v_h)
        dec_end.append(jnp.exp(cum_end))
    mm = [_hdot_s(ar[h], bk[h], NT) for h in heads]
    gs = [_hdot_s(ar[h], _split(s_old[h]), NT) for h in heads]
    lp = [jnp.where(strict, mm[h][0:cl_rows, 0:cl_rows], 0.0) for h in heads]
    m_ak = [_split(jnp.where(strict, mm[h][0:cl_rows, cl_rows:], 0.0)) for h in heads]
    m_rb = [_split(jnp.where(incl, mm[h][cl_rows:, 0:cl_rows], 0.0)) for h in heads]
    m_rk = [_split(jnp.where(incl, mm[h][cl_rows:, cl_rows:], 0.0)) for h in heads]
    u = [gs[h][0:cl_rows] + _hdot_s(m_ak[h], v_s[h]) for h in heads]
    for lvl in range(n_levels):
        lp_s = [_split(lp[h]) for h in heads]
        u = [u[h] + _hdot_s(lp_s[h], _split(u[h])) for h in heads]
        if lvl + 1 < n_levels:
            lp = [_hdot_s(lp_s[h], lp_s[h]) for h in heads]
    u_s = [_split(u[h]) for h in heads]
    y_l = [gs[h][cl_rows:] + _hdot_s(m_rb[h], u_s[h]) + _hdot_s(m_rk[h], v_s[h]) for h in heads]
    for h in heads:
        uv = _split(jnp.concatenate([u[h], v_l[h]], axis=0))
        st[h] = s_old[h] * dec_end[h] + _hdot_s(uv, bk_end[h], TN)
    for h in heads:
        hs = hsl[h]
        y = y_l[h]
        mean = jnp.mean(y, axis=-1, keepdims=True)
        var = jnp.mean(jnp.square(y - mean), axis=-1, keepdims=True)
        yn = (y - mean) * lax.rsqrt(var + GN_EPS) * gng_ref[:, hs] + gnb_ref[:, hs]
        bonus = jnp.sum(r_l[h] * k_l[h] * rk_ref[:, hs], axis=-1, keepdims=True) * v_l[h]
        y_ref[0, :, hs] = yn + bonus

    @pl.when(c == pl.num_programs(1) - 1)
    def _():
        sout_ref[0] = st[...]


def _rwkv(rz, shift0, state0, params, chunk, t_valid):
    b, t = rz.shape[:2]
    vec = lambda n: _const_spec((1, n))
    return pl.pallas_call(
        functools.partial(_rwkv_kernel, t_valid=t_valid),
        grid=(b, t // chunk),
        in_specs=[pl.BlockSpec((1, chunk, C_R), lambda bi, c: (bi, c, 0)),
                  pl.BlockSpec((1, 1, C_R), lambda bi, c: (bi, 0, 0)),
                  pl.BlockSpec((1, R_HEADS, R_HD, R_HD), lambda bi, c: (bi, 0, 0, 0)),
                  vec(C_R), vec(W_R), _const_spec((R_W, W_R)), vec(W_R), _const_spec((R_A, W_R)),
                  vec(W_R), vec(W_R), vec(W_R), vec(W_R), vec(W_R)],
        out_specs=[pl.BlockSpec((1, chunk, W_R), lambda bi, c: (bi, c, 0)),
                   pl.BlockSpec((1, R_HEADS, R_HD, R_HD), lambda bi, c: (bi, 0, 0, 0))],
        out_shape=[jax.ShapeDtypeStruct((b, t, W_R), F32),
                   jax.ShapeDtypeStruct((b, R_HEADS, R_HD, R_HD), F32)],
        scratch_shapes=[pltpu.VMEM((R_HEADS, R_HD, R_HD), F32), pltpu.VMEM((1, C_R), F32)],
        compiler_params=pltpu.CompilerParams(dimension_semantics=("parallel", "arbitrary"),
                                             vmem_limit_bytes=VMEM_LIMIT),
        name="rwkv",
    )(rz, shift0.reshape(b, 1, C_R), state0, *params)


def _rope_tables(pos):
    half = HD // 2
    inv = ROPE_THETA ** (-jnp.arange(half, dtype=F32) / half)
    ang = pos.astype(F32)[:, None] * inv[None, :]
    cos, sin = jnp.cos(ang), jnp.sin(ang)
    reps = LANES // HD
    return (jnp.tile(jnp.concatenate([cos, cos], axis=-1), (1, reps)),
            jnp.tile(jnp.concatenate([-sin, sin], axis=-1), (1, reps)))


def _pick_tile(n, cap):
    t = cap
    while n % t:
        t //= 2
    return t


def kernel(x_prompt, x_sample, p_prompt, p_sample, cache_cmp_kv, cache_sel_kv, cache_win_kv, state_wkv, state_shift, page_table, norm_g, w_in, cmp_pe, cmp_w1, cmp_b1, cmp_w2, cmp_b2, w_pa, rwkv_mu, rwkv_w0, rwkv_w_up, rwkv_a0, rwkv_a_up, rwkv_k_k, rwkv_k_a, rwkv_r_k, rwkv_gn_g, rwkv_gn_b, w_pb, w_out, ple_norm_g, w_ple_gate, w_ple_proj, final_norm_g):
    b, s_len = x_prompt.shape[:2]
    db, ds = x_sample.shape[:2]
    depth = norm_g.shape[0]
    assert depth == 1, "single-layer trunk"
    n_pages = page_table.shape[1]
    past = n_pages * PAGE_SIZE
    win_buf = cache_win_kv.shape[2]
    n_pool = cache_cmp_kv.shape[1]
    i = 0

    w = w_in[i]
    o_q, o_kv, o_ng, o_ag = 0, W_A, W_A + 6 * KV_W, W_A + 6 * KV_W + 3 * NSA_HEADS
    w_pack = jnp.concatenate([w[:, o_q:o_ng], w[:, o_ag:], w[:, o_ng:o_ag],
                              jnp.zeros((D_MODEL, C_PACK - w.shape[1]), w.dtype)], axis=1).astype(BF16)
    cw = _compress_weights(cmp_pe[i], cmp_w1[i], cmp_b1[i], cmp_w2[i], cmp_b2[i])
    row = lambda u: u.reshape(1, -1)
    rw = (row(rwkv_mu[i]), row(rwkv_w0[i]), rwkv_w_up[i], row(rwkv_a0[i]), rwkv_a_up[i], row(rwkv_k_k[i]),
          row(rwkv_k_a[i]), row(rwkv_r_k[i]), row(rwkv_gn_g[i]), row(rwkv_gn_b[i]))
    ow = (w_pa[i].astype(BF16), w_pb[i].astype(BF16), w_out[i].astype(BF16), row(ple_norm_g[i]),
          w_ple_gate[i].astype(BF16), w_ple_proj[i].astype(BF16), row(final_norm_g))

    tm = _pick_tile(s_len, 256)
    cos_p, sin_p = _rope_tables(jnp.arange(s_len, dtype=jnp.int32))
    xp2 = x_prompt.reshape(b * s_len, D_MODEL)
    q, cmp_p, sel_p, win_p, gates, sa, rz, sr, mg, ksel, kwin = _project(xp2, row(norm_g[i]), w_pack, cos_p, sin_p,
                                                                        tm, True)
    kc_p = _compress_seq(cmp_p, cw)
    seq = lambda u: u.reshape(b, s_len, u.shape[-1])
    o_a = _nsa_prompt(q, gates, kc_p, ksel, sel_p, kwin, win_p, tq=128, tk=min(512, s_len))
    chunk = 64
    o_b, st_p = _rwkv(seq(rz), jnp.zeros((b, C_R), F32), jnp.zeros((b, R_HEADS, R_HD, R_HD), F32), rw, chunk, chunk)
    y_p = _merge(xp2, o_a.reshape(b * s_len, W_A), o_b.reshape(b * s_len, W_R), sa, sr, mg,
                 p_prompt[i].reshape(b * s_len, D_PLE), *ow, tm=tm)
    kv_shape = lambda u, n, t: u.reshape(n, t, 2, NSA_GROUPS, HD)
    from_cm = lambda u: jnp.transpose(u.reshape(u.shape[0], 2, NSA_GROUPS, HD, u.shape[2]), (0, 4, 1, 2, 3))
    to_cm = lambda u: jnp.transpose(u, (0, 2, 3, 4, 1)).reshape(u.shape[0], 2 * KV_W, u.shape[1])
    cmp_kv_p = from_cm(cmp_p)
    sel_kv_p = from_cm(sel_p)
    win_kv_p = from_cm(win_p[:, :, s_len - min(WIN, s_len):])
    shift_p = seq(rz)[:, -1]

    n_s = db * ds
    cos_s, sin_s = _rope_tables(jnp.tile(past + jnp.arange(ds, dtype=jnp.int32), db))
    xs2 = x_sample.reshape(n_s, D_MODEL)
    q, cmp_s, sel_s, win_s, gates, sa, rz, sr, mg = _project(xs2, row(norm_g[i]), w_pack, cos_s, sin_s, n_s, False)
    kc_s = _compress_paged(page_table, to_cm(cache_cmp_kv[i]), cw, n_pages=_pick_tile(n_pages, 32))
    seqs = lambda u: u.reshape(db, ds, u.shape[-1])
    o_a = _nsa_sample(page_table, seqs(q), seqs(gates), kc_s, to_cm(cache_sel_kv[i]),
                      seqs(sel_s), to_cm(cache_win_kv[i]), seqs(win_s), n_pages=_pick_tile(n_pages, 32))
    pad_t = 16
    rz_pad = jnp.pad(seqs(rz), ((0, 0), (0, pad_t - ds), (0, 0)))
    o_b, st_s = _rwkv(rz_pad, state_shift[i], state_wkv[i], rw, pad_t, ds)
    y_s = _merge(xs2, o_a.reshape(n_s, W_A), o_b[:, :ds].reshape(n_s, W_R), sa, sr, mg,
                 p_sample[i].reshape(n_s, D_PLE), *ow, tm=n_s)
    win_kv_s = jnp.concatenate([cache_win_kv[i], kv_shape(win_s, db, ds)], axis=1)[:, ds:]
    shift_s = seqs(rz)[:, -1]

    return (y_p.reshape(b, s_len, D_MODEL), y_s.reshape(db, ds, D_MODEL),
            cmp_kv_p[None], kv_shape(cmp_s, db, ds)[None], sel_kv_p[None], kv_shape(sel_s, db, ds)[None],
            win_kv_p[None], win_kv_s[None], st_p[None], st_s[None], shift_p[None], shift_s[None])
```

```python
import functools

import numpy as np
import jax
import jax.numpy as jnp
from jax import lax
from jax.experimental import pallas as pl
from jax.experimental.pallas import tpu as pltpu

F32 = jnp.float32
BF16 = jnp.bfloat16

D_MODEL = 1024
D_PLE = 256
NSA_HEADS = 8
NSA_GROUPS = 2
HD = 64
HPG = NSA_HEADS // NSA_GROUPS
W_A = NSA_HEADS * HD
KV_W = NSA_GROUPS * HD
L_CMP = 32
D_CMP = 16
CMP_HID = 128
L_SEL = 64
N_SEL = 16
WIN = 512
FORCE_SCORE = 1e4
NEG_INF = -1e30
R_HEADS = 8
R_HD = 64
W_R = R_HEADS * R_HD
R_W = 64
R_A = 64
C_R = 3 * W_R + R_W + R_A
GN_EPS = 64e-5
ROPE_THETA = 10000.0
NORM_EPS = 1e-6
PAGE_SIZE = 128

LANES = 128
CHUNKS_PER_PAGE = PAGE_SIZE // D_CMP
CHUNK_W = D_CMP * 2 * KV_W
VMEM_LIMIT = 56 * 1024 * 1024

C_Q = 0
C_KV = 512
C_AG = 1280
C_RZ = 1792
C_RG = 3456
C_MG = 3968
C_NG = 6016
C_PACK = 6144

NN = ((1,), (0,))
NT = ((1,), (1,))
TN = ((0,), (0,))


def _dg(a, b, dims=NN):
    return lax.dot_general(a, b, (dims, ((), ())), preferred_element_type=F32)


def _split(x):
    hi = x.astype(BF16)
    lo = (x - hi.astype(F32)).astype(BF16)
    return hi, lo


def _hdot_s(a, b, dims=NN):
    (ah, al), (bh, bl) = a, b
    return _dg(ah, bh, dims) + (_dg(ah, bl, dims) + _dg(al, bh, dims))


def _hdot(a, b, dims=NN):
    return _hdot_s(_split(a), _split(b), dims)


def _sigmoid(x):
    return 1.0 / (1.0 + jnp.exp(-x))


def _silu(x):
    return x * _sigmoid(x)


def _rms(x, g):
    return x * lax.rsqrt(jnp.mean(x * x, axis=-1, keepdims=True) + NORM_EPS) * g


def _const_spec(shape):
    zeros = (0,) * len(shape)
    return pl.BlockSpec(shape, lambda *_: zeros)


def _rope128(z, cos, sin, first):
    partner = jnp.where(first, pltpu.roll(z, LANES - HD // 2, 1), pltpu.roll(z, HD // 2, 1))
    return z * cos + partner * sin


def _proj_kernel(x_ref, g_ref, w_ref, cos_ref, sin_ref,
                 q_ref, cmp_ref, sel_ref, win_ref, gate_ref, sa_ref, rz_ref, sr_ref, mg_ref, *aug_refs,
                 seq_major, n_pos_blocks):
    hb = _rms(x_ref[...], g_ref[...]).astype(BF16)
    cos = cos_ref[...]
    sin = sin_ref[...]
    tm = cos.shape[0]
    lane = lax.broadcasted_iota(jnp.int32, cos.shape, 1)
    first = (lane % HD) < (HD // 2)

    def mm(a, b):
        return jnp.dot(hb, w_ref[:, a:b], preferred_element_type=F32)

    zq = mm(C_Q, C_Q + W_A)
    q = jnp.concatenate([_rope128(zq[:, c * LANES:(c + 1) * LANES], cos, sin, first) for c in range(W_A // LANES)],
                        axis=1)
    gates = _sigmoid(mm(C_NG, C_PACK))
    if seq_major:
        q_ref[0] = q.T
        gate_ref[0] = gates.T
    else:
        q_ref[...] = q
        gate_ref[...] = gates
    for i, ref in enumerate((cmp_ref, sel_ref, win_ref)):
        zkv = mm(C_KV + 2 * KV_W * i, C_KV + 2 * KV_W * (i + 1))
        k_rot = _rope128(zkv[:, 0:KV_W], cos, sin, first)
        if seq_major:
            ref[0] = jnp.concatenate([k_rot, zkv[:, KV_W:2 * KV_W]], axis=1).T
            if i > 0:
                aug_ref = aug_refs[i - 1]
                if i == 1:
                    t_in_seq = ((pl.program_id(0) % n_pos_blocks) * tm
                                + lax.broadcasted_iota(jnp.int32, cos.shape, 0))
                    tail = jnp.where(jnp.right_shift(t_in_seq, int(np.log2(L_SEL))) == lane - HD, 1.0, 0.0)
                else:
                    tail = jnp.zeros(cos.shape, F32)
                for g in range(NSA_GROUPS):
                    k_g = k_rot if g == 0 else pltpu.roll(k_rot, LANES - g * HD, 1)
                    aug_ref[0, g] = jnp.where(lane < HD, k_g, tail).astype(BF16)
        else:
            ref[:, 0:KV_W] = k_rot
            ref[:, KV_W:2 * KV_W] = zkv[:, KV_W:2 * KV_W]
    sa_ref[...] = _silu(mm(C_AG, C_AG + W_A))
    rz_ref[...] = mm(C_RZ, C_RZ + C_R)
    sr_ref[...] = _silu(mm(C_RG, C_RG + W_R))
    mg_ref[...] = _sigmoid(mm(C_MG, C_MG + 2 * D_MODEL))


def _project(x2d, norm_g, w_pack, cos_t, sin_t, tm, seq_major):
    n = x2d.shape[0]
    seq_len = cos_t.shape[0]
    n_pos_blocks = seq_len // tm
    n_seq = n // seq_len
    widths = (W_A, 2 * KV_W, 2 * KV_W, 2 * KV_W, LANES, W_A, C_R, W_R, 2 * D_MODEL)
    row = lambda w: pl.BlockSpec((tm, w), lambda i: (i, 0))
    pos = pl.BlockSpec((tm, LANES), lambda i: (i % n_pos_blocks, 0))
    out_specs = [row(w) for w in widths]
    out_shape = [jax.ShapeDtypeStruct((n, w), F32) for w in widths]
    in_specs = [row(D_MODEL), _const_spec((1, D_MODEL)), _const_spec((D_MODEL, C_PACK)), pos, pos]
    args = [x2d, norm_g, w_pack, cos_t, sin_t]
    if seq_major:
        assert seq_len // L_SEL <= LANES - HD, "selection-block one-hot must fit beside the key"
        for o in range(5):
            out_specs[o] = pl.BlockSpec((1, widths[o], tm), lambda i: (i // n_pos_blocks, 0, i % n_pos_blocks))
            out_shape[o] = jax.ShapeDtypeStruct((n_seq, widths[o], seq_len), F32)
        for _ in range(2):
            out_specs.append(pl.BlockSpec((1, NSA_GROUPS, tm, LANES),
                                          lambda i: (i // n_pos_blocks, 0, i % n_pos_blocks, 0)))
            out_shape.append(jax.ShapeDtypeStruct((n_seq, NSA_GROUPS, seq_len, LANES), BF16))
    return pl.pallas_call(
        functools.partial(_proj_kernel, seq_major=seq_major, n_pos_blocks=n_pos_blocks),
        grid=(n // tm,),
        in_specs=in_specs,
        out_specs=out_specs,
        out_shape=out_shape,
        compiler_params=pltpu.CompilerParams(dimension_semantics=("parallel",), vmem_limit_bytes=VMEM_LIMIT),
        name="project",
    )(*args)


def _merge_kernel(x_ref, oa_ref, ob_ref, sa_ref, sr_ref, mg_ref, p_ref,
                  wpa_ref, wpb_ref, wout_ref, pg_ref, wgate_ref, wproj_ref, fg_ref, y_ref):
    bdot = lambda a, w_ref: jnp.dot(a.astype(BF16), w_ref[...], preferred_element_type=F32)
    ya = bdot(oa_ref[...] * sa_ref[...], wpa_ref)
    yb = bdot(ob_ref[...] * sr_ref[...], wpb_ref)
    m = mg_ref[:, 0:D_MODEL] * ya + mg_ref[:, D_MODEL:2 * D_MODEL] * yb
    x2 = x_ref[...] + bdot(m, wout_ref)
    gate = _sigmoid(bdot(_rms(x2, pg_ref[...]), wgate_ref))
    x3 = x2 + gate * bdot(p_ref[...], wproj_ref)
    y_ref[...] = _rms(x3, fg_ref[...])


def _merge(x2d, oa, ob, sa, sr, mg, p2d, wpa, wpb, wout, pg, wgate, wproj, fg, tm):
    n = x2d.shape[0]
    row = lambda w: pl.BlockSpec((tm, w), lambda i: (i, 0))
    return pl.pallas_call(
        _merge_kernel,
        grid=(n // tm,),
        in_specs=[row(D_MODEL), row(W_A), row(W_R), row(W_A), row(W_R), row(2 * D_MODEL), row(D_PLE),
                  _const_spec((W_A, D_MODEL)), _const_spec((W_R, D_MODEL)), _const_spec((D_MODEL, D_MODEL)),
                  _const_spec((1, D_MODEL)), _const_spec((D_MODEL, D_MODEL)), _const_spec((D_PLE, D_MODEL)),
                  _const_spec((1, D_MODEL))],
        out_specs=row(D_MODEL),
        out_shape=jax.ShapeDtypeStruct((n, D_MODEL), F32),
        compiler_params=pltpu.CompilerParams(dimension_semantics=("parallel",), vmem_limit_bytes=VMEM_LIMIT),
        name="merge",
    )(x2d, oa, ob, sa, sr, mg, p2d, wpa, wpb, wout, pg, wgate, wproj, fg)


def _page_copies(pt_ref, pool_ref, buf, sem, b, t, slot, n_pages):
    return [pltpu.make_async_copy(pool_ref.at[pt_ref[b, t * n_pages + p]],
                                  buf.at[slot, :, pl.ds(p * PAGE_SIZE, PAGE_SIZE)], sem.at[slot])
            for p in range(n_pages)]


def _paged_tile(pt_ref, pool_ref, buf, sem, n_pages):
    b, t = pl.program_id(0), pl.program_id(1)
    nt = pl.num_programs(1)
    step = b * nt + t
    slot = step % 2

    @pl.when(step == 0)
    def _():
        for cp in _page_copies(pt_ref, pool_ref, buf, sem, b, t, slot, n_pages):
            cp.start()

    nxt = step + 1

    @pl.when(nxt < pl.num_programs(0) * nt)
    def _():
        for cp in _page_copies(pt_ref, pool_ref, buf, sem, nxt // nt, nxt % nt, 1 - slot, n_pages):
            cp.start()

    for cp in _page_copies(pt_ref, pool_ref, buf, sem, b, t, slot, n_pages):
        cp.wait()
    return slot


CMP_HALF = NSA_GROUPS * CMP_HID


def _compress_core(page_t, xs, perm_ref, w1_ref, pe_ref, b1_ref, w2_ref, b2_ref, carry, out_ref):
    n_pages = xs.shape[0]
    rows = n_pages * CHUNKS_PER_PAGE
    for p in range(n_pages):
        xp = _dg(perm_ref[...], page_t(p).astype(BF16), NT)
        xs[p] = xp.reshape(D_CMP, CHUNKS_PER_PAGE, 2 * KV_W)
    ridx = lax.broadcasted_iota(jnp.int32, (rows, CMP_HALF), 0)
    for kv in range(2):
        cs = slice(kv * KV_W, (kv + 1) * KV_W)
        x = jnp.concatenate([xs[:, j, :, cs].reshape(rows, KV_W) for j in range(D_CMP)], axis=1)
        x = jnp.concatenate([x, pe_ref[kv]], axis=0).astype(BF16)
        z = jnp.dot(x, w1_ref[kv], preferred_element_type=F32)
        pos = z[rows:rows + 1, 0:CMP_HALF] + z[rows + 1:rows + 2, CMP_HALF:]
        lo = z[0:rows, 0:CMP_HALF]
        lo_prev = jnp.where(ridx == 0, carry[kv], pltpu.roll(lo, 1, 0))
        carry[kv] = lo[rows - 1:rows]
        hid = lo_prev + z[0:rows, CMP_HALF:] + pos + b1_ref[kv]
        out_ref[0, :, cs] = jnp.dot(_silu(hid).astype(BF16), w2_ref[kv], preferred_element_type=F32) + b2_ref[kv]


def _compress_paged_kernel(pt_ref, pool_ref, perm_ref, w1_ref, pe_ref, b1_ref, w2_ref, b2_ref, out_ref,
                           buf, sem, xs, carry, *, n_pages):
    slot = _paged_tile(pt_ref, pool_ref, buf, sem, n_pages)

    @pl.when(pl.program_id(1) == 0)
    def _():
        carry[...] = jnp.zeros_like(carry)

    _compress_core(lambda p: buf[slot, :, p * PAGE_SIZE:(p + 1) * PAGE_SIZE], xs, perm_ref, w1_ref, pe_ref, b1_ref,
                   w2_ref, b2_ref, carry, out_ref)


def _compress_seq_kernel(x_ref, perm_ref, w1_ref, pe_ref, b1_ref, w2_ref, b2_ref, out_ref, xs, carry):
    carry[...] = jnp.zeros_like(carry)
    _compress_core(lambda p: x_ref[0, :, p * PAGE_SIZE:(p + 1) * PAGE_SIZE], xs, perm_ref, w1_ref, pe_ref, b1_ref,
                   w2_ref, b2_ref, carry, out_ref)


def _compress_specs(cst):
    return [cst((PAGE_SIZE, PAGE_SIZE)), cst((2, D_CMP * KV_W, 2 * CMP_HALF)), cst((2, 8, D_CMP * KV_W)),
            cst((2, 1, CMP_HALF)), cst((2, CMP_HALF, KV_W)), cst((2, 1, KV_W))]


def _compress_scratch(n_pages):
    return [pltpu.VMEM((n_pages, D_CMP, CHUNKS_PER_PAGE, 2 * KV_W), F32), pltpu.VMEM((2, 1, CMP_HALF), F32)]


def _compress_paged(page_table, pool, cw, n_pages):
    nb, pages_per_seq = page_table.shape
    nt = pages_per_seq // n_pages
    toks = n_pages * PAGE_SIZE
    rows = toks // D_CMP
    cst = lambda shape: pl.BlockSpec(shape, lambda b, t, pt: (0,) * len(shape))
    return pl.pallas_call(
        functools.partial(_compress_paged_kernel, n_pages=n_pages),
        grid_spec=pltpu.PrefetchScalarGridSpec(
            num_scalar_prefetch=1,
            grid=(nb, nt),
            in_specs=[pl.BlockSpec(memory_space=pl.ANY)] + _compress_specs(cst),
            out_specs=pl.BlockSpec((1, rows, 2 * KV_W), lambda b, t, pt: (b, t, 0)),
            scratch_shapes=[pltpu.VMEM((2, 2 * KV_W, toks), F32), pltpu.SemaphoreType.DMA((2,))]
            + _compress_scratch(n_pages)),
        out_shape=jax.ShapeDtypeStruct((nb, nt * rows, 2 * KV_W), F32),
        compiler_params=pltpu.CompilerParams(dimension_semantics=("arbitrary", "arbitrary"),
                                             vmem_limit_bytes=VMEM_LIMIT),
        name="compress_paged",
    )(page_table, pool, *cw)


def _compress_seq(x_t, cw):
    nb, _, toks = x_t.shape
    rows = toks // D_CMP
    return pl.pallas_call(
        _compress_seq_kernel,
        grid=(nb,),
        in_specs=[pl.BlockSpec((1, 2 * KV_W, toks), lambda b: (b, 0, 0))] + _compress_specs(_const_spec),
        out_specs=pl.BlockSpec((1, rows, 2 * KV_W), lambda b: (b, 0, 0)),
        out_shape=jax.ShapeDtypeStruct((nb, rows, 2 * KV_W), F32),
        scratch_shapes=_compress_scratch(toks // PAGE_SIZE),
        compiler_params=pltpu.CompilerParams(dimension_semantics=("parallel",), vmem_limit_bytes=VMEM_LIMIT),
        name="compress_seq",
    )(x_t, *cw)


def _compress_weights(cmp_pe, cmp_w1, cmp_b1, cmp_w2, cmp_b2):
    eye_g = jnp.eye(NSA_GROUPS, dtype=F32)
    w1 = cmp_w1.reshape(2, 2, D_CMP, HD, CMP_HID)
    w1j = jnp.einsum('kljdf,gG->kjgdlGf', w1, eye_g).reshape(2, D_CMP * KV_W, 2 * CMP_HALF)
    pe = cmp_pe.reshape(2, 2, D_CMP, HD)
    pe = jnp.broadcast_to(pe[:, :, :, None, :], (2, 2, D_CMP, NSA_GROUPS, HD))
    pej = jnp.pad(pe.reshape(2, 2, D_CMP * KV_W), ((0, 0), (0, 6), (0, 0)))
    b1big = jnp.broadcast_to(cmp_b1[:, None, :], (2, NSA_GROUPS, CMP_HID)).reshape(2, 1, CMP_HALF)
    w2big = jnp.einsum('kfd,gG->kgfGd', cmp_w2, eye_g).reshape(2, CMP_HALF, KV_W)
    b2big = jnp.broadcast_to(cmp_b2[:, None, :], (2, NSA_GROUPS, HD)).reshape(2, 1, KV_W)
    dst = np.arange(PAGE_SIZE)
    perm = (dst[:, None] % CHUNKS_PER_PAGE) * D_CMP + dst[:, None] // CHUNKS_PER_PAGE == dst[None, :]
    return jnp.asarray(perm, dtype=BF16), w1j.astype(BF16), pej, b1big, w2big.astype(BF16), b2big


def _overlap_matrix(n_rows, n_cols):
    c_start = (np.arange(n_rows) - 1) * D_CMP
    s_start = np.arange(n_cols) * L_SEL
    ov = (c_start[:, None] <= s_start[None, :] + L_SEL - 1) & (c_start[:, None] + L_CMP - 1 >= s_start[None, :])
    ov &= (np.arange(n_rows) >= 1)[:, None]
    return ov.astype(np.float32)


def _online(state, s, msk, v_bf, v_dims=NN):
    m, l, acc = state
    if msk is not None:
        s = jnp.where(msk, s, NEG_INF)
    m_new = jnp.maximum(m, jnp.max(s, axis=-1, keepdims=True))
    alpha = jnp.exp(m - m_new)
    p = jnp.exp(s - m_new)
    if msk is not None:
        p = jnp.where(msk, p, 0.0)
    l = alpha * l + jnp.sum(p, axis=-1, keepdims=True)
    acc = alpha * acc + _dg(p.astype(BF16), v_bf, v_dims)
    return m_new, l, acc


def _online_init(rows):
    return jnp.full((rows, 1), NEG_INF, F32), jnp.zeros((rows, 1), F32), jnp.zeros((rows, HD), F32)


def _online_out(state):
    _, l, acc = state
    return acc / jnp.maximum(l, 1e-30)


def _stack_heads(q, g):
    return jnp.concatenate([q[:, (HPG * g + h) * HD:(HPG * g + h + 1) * HD] for h in range(HPG)], axis=0) * (HD ** -0.5)


def _compressed_branch(qs, kc, vc, valid, n_tok):
    s = jnp.where(valid, _hdot(qs, kc, NT), NEG_INF)
    m = jnp.max(s, axis=-1, keepdims=True)
    p = jnp.where(valid, jnp.exp(s - m), 0.0)
    p = p / jnp.maximum(jnp.sum(p, axis=-1, keepdims=True), 1e-30)
    o = jnp.dot(p.astype(BF16), vc.astype(BF16), preferred_element_type=F32)
    psum = p[0:n_tok]
    for h in range(1, HPG):
        psum = psum + p[h * n_tok:(h + 1) * n_tok]
    return o, psum


def _select_blocks(imp, tpos, n_real):
    blk = lax.broadcasted_iota(jnp.int32, imp.shape, 1)
    cur = tpos // L_SEL
    forced = (blk == 0) | (blk == cur) | (blk == cur - 1)
    imp = jnp.where(forced, FORCE_SCORE, imp)
    imp = jnp.where(blk * L_SEL > tpos, -1.0, imp)
    imp = jnp.where(blk >= n_real, -2.0, imp)
    cnt = jnp.zeros(imp.shape, F32)
    for mcol in range(n_real):
        col = imp[:, mcol:mcol + 1]
        ahead = (col > imp) | ((col == imp) & (blk > mcol))
        cnt = cnt + jnp.where(ahead, 1.0, 0.0)
    return jnp.where(cnt < N_SEL, 1.0, 0.0)


def _select_blocks_t(imp_t, tpos_row, n_real):
    sub = 8
    blk = lax.broadcasted_iota(jnp.int32, imp_t.shape, 0)
    cur = jnp.right_shift(tpos_row, int(np.log2(L_SEL)))
    forced = (blk == 0) | (blk == cur) | (blk == cur - 1)
    imp_t = jnp.where(forced, FORCE_SCORE, imp_t)
    imp_t = jnp.where(blk * L_SEL > tpos_row, -1.0, imp_t)
    slabs = [imp_t[s0:s0 + sub] for s0 in range(0, n_real, sub)]
    cnt = [jnp.zeros(s.shape, F32) for s in slabs]
    sub_iota = lax.broadcasted_iota(jnp.int32, slabs[0].shape, 0)
    for m in range(n_real):
        row = jnp.broadcast_to(imp_t[m:m + 1], slabs[0].shape)
        for si, slab in enumerate(slabs):
            ge = jnp.where(row >= slab, 1.0, 0.0)
            gt = jnp.where(row > slab, 1.0, 0.0)
            if si * sub > m:
                ahead = ge
            elif si * sub + sub - 1 < m:
                ahead = gt
            else:
                ahead = jnp.where(sub_iota > m - si * sub, ge, gt)
            cnt[si] = cnt[si] + ahead
    return jnp.where(jnp.concatenate(cnt, axis=0) < N_SEL, 1.0, 0.0)


def _online_t(state, s_t, msk, v_t):
    m, l, acc = state
    if msk is not None:
        s_t = jnp.where(msk, s_t, NEG_INF)
    m_new = jnp.maximum(m, jnp.max(s_t, axis=0, keepdims=True))
    alpha = jnp.exp(m - m_new)
    p = jnp.exp(s_t - m_new)
    if msk is not None:
        p = jnp.where(msk, p, 0.0)
    l = alpha * l + jnp.sum(p, axis=0, keepdims=True)
    acc = alpha * acc + _dg(v_t, p.astype(BF16))
    return m_new, l, acc


def _sweep_t(units, states):
    states = list(states)
    s_t = {0: units[0][1]()}
    pending = None

    def finish(k, alpha, p, values):
        m, l, acc = states[k]
        states[k] = (m, l, alpha * acc + _dg(values(), p))

    for u, (k, _, msk, values) in enumerate(units):
        if u + 1 < len(units):
            s_t[u + 1] = units[u + 1][1]()
        m, l, acc = states[k]
        s = s_t.pop(u)
        if msk is not None:
            s = jnp.where(msk, s, NEG_INF)
        m_new = jnp.maximum(m, jnp.max(s, axis=0, keepdims=True))
        alpha = jnp.exp(m - m_new)
        p = jnp.exp(s - m_new)
        states[k] = (m_new, alpha * l + jnp.sum(p, axis=0, keepdims=True), acc)
        if pending is not None:
            finish(*pending)
        pending = (k, alpha, p.astype(BF16), values)
    finish(*pending)
    return tuple(states)


def _online_t_init(cols):
    return jnp.full((1, cols), NEG_INF, F32), jnp.zeros((1, cols), F32), jnp.zeros((HD, cols), F32)


def _nsa_prompt_kernel(qt_ref, gatet_ref, kc_ref, ksel_ref, vsel_ref, kwin_ref, vwin_ref, ovt_ref, o_ref, *, tq, tk):
    i = pl.program_id(1)
    n_cmp = kc_ref.shape[1]
    n_sb = ovt_ref.shape[0]
    cols = HPG * tq
    aug = LANES - HD
    t0 = i * tq
    tpos_row = t0 + lax.broadcasted_iota(jnp.int32, (1, tq), 1)
    tpos4 = jnp.concatenate([tpos_row] * HPG, axis=1)
    n_full = t0 // tk
    w_start = pl.multiple_of(jnp.maximum(t0 - WIN, 0), tq)
    gates_t = gatet_ref[0]
    groups = range(NSA_GROUPS)
    n_w = WIN + tq
    dist = tpos4 - (w_start + lax.broadcasted_iota(jnp.int32, (n_w, 1), 0))
    ridx = lax.broadcasted_iota(jnp.int32, (n_cmp, cols), 0)
    valid = (ridx >= 1) & (D_CMP * ridx + (L_CMP - D_CMP - 1) <= tpos4)
    qa, o_c = [], []
    for g in groups:
        q_t = jnp.concatenate([qt_ref[0, (HPG * g + h) * HD:(HPG * g + h + 1) * HD, :] for h in range(HPG)],
                              axis=1) * (HD ** -0.5)
        s = jnp.where(valid, _hdot(kc_ref[0, :, g * HD:(g + 1) * HD], q_t), NEG_INF)
        p = jnp.where(valid, jnp.exp(s - jnp.max(s, axis=0, keepdims=True)), 0.0)
        p = p / jnp.maximum(jnp.sum(p, axis=0, keepdims=True), 1e-30)
        o_c.append(_dg(kc_ref[0, :, KV_W + g * HD:KV_W + (g + 1) * HD].astype(BF16), p.astype(BF16), TN))
        psum = p[:, 0:tq]
        for h in range(1, HPG):
            psum = psum + p[:, h * tq:(h + 1) * tq]
        sel_t = _select_blocks_t(_hdot(ovt_ref[...], psum), tpos_row, n_sb)
        bias = jnp.concatenate([(sel_t - 1.0) * (-NEG_INF)] * HPG, axis=1)
        if n_sb < aug:
            bias = jnp.concatenate([bias, jnp.zeros((aug - n_sb, cols), F32)], axis=0)
        qa.append(jnp.concatenate([q_t, bias], axis=0).astype(BF16))

    def sel_unit(g, j, causal):
        k0 = pl.multiple_of(j * tk, tk)
        kpos = k0 + lax.broadcasted_iota(jnp.int32, (tk, 1), 0)
        return (g, lambda: _dg(ksel_ref[0, g, pl.ds(k0, tk), :], qa[g]), (kpos <= tpos4) if causal else None,
                lambda: vsel_ref[0, g * HD:(g + 1) * HD, pl.ds(k0, tk)].astype(BF16))

    def win_unit(g):
        return (NSA_GROUPS + g, lambda: _dg(kwin_ref[0, g, pl.ds(w_start, n_w), :], qa[g]),
                (dist >= 0) & (dist < WIN),
                lambda: vwin_ref[0, g * HD:(g + 1) * HD, pl.ds(w_start, n_w)].astype(BF16))

    states = lax.fori_loop(
        0, n_full // 2,
        lambda jj, st: _sweep_t([sel_unit(g, 2 * jj + d, False) for d in range(2) for g in groups], st),
        tuple(_online_t_init(cols) for g in groups))
    states = lax.cond(n_full % 2 == 1, lambda st: _sweep_t([sel_unit(g, n_full - 1, False) for g in groups], st),
                      lambda st: st, states)
    states = _sweep_t([u for g in groups for u in (sel_unit(g, n_full, True), win_unit(g))],
                      states + tuple(_online_t_init(cols) for g in groups))
    heads_out = []
    for g in groups:
        o_s = _online_out(states[g])
        o_w = _online_out(states[NSA_GROUPS + g])
        for h in range(HPG):
            hd = HPG * g + h
            c = slice(h * tq, (h + 1) * tq)
            heads_out.append(gates_t[3 * hd:3 * hd + 1] * o_c[g][:, c] + gates_t[3 * hd + 1:3 * hd + 2] * o_s[:, c]
                             + gates_t[3 * hd + 2:3 * hd + 3] * o_w[:, c])
    o_ref[0] = jnp.concatenate(heads_out, axis=0).T


def _nsa_prompt(q_t, gates_t, kc, ksel, sel_t, kwin, win_t, tq, tk):
    b, _, t = q_t.shape
    n_cmp = kc.shape[1]
    n_sb = t // L_SEL
    ovt = jnp.asarray(_overlap_matrix(n_cmp, n_sb).T)
    keys = pl.BlockSpec((1, NSA_GROUPS, t, LANES), lambda bi, i: (bi, 0, 0, 0))
    values = pl.BlockSpec((1, KV_W, t), lambda bi, i: (bi, 1, 0))
    return pl.pallas_call(
        functools.partial(_nsa_prompt_kernel, tq=tq, tk=tk),
        grid=(b, t // tq),
        in_specs=[pl.BlockSpec((1, W_A, tq), lambda bi, i: (bi, 0, i)),
                  pl.BlockSpec((1, LANES, tq), lambda bi, i: (bi, 0, i)),
                  pl.BlockSpec((1, n_cmp, 2 * KV_W), lambda bi, i: (bi, 0, 0)),
                  keys, values, keys, values, _const_spec((n_sb, n_cmp))],
        out_specs=pl.BlockSpec((1, tq, W_A), lambda bi, i: (bi, i, 0)),
        out_shape=jax.ShapeDtypeStruct((b, t, W_A), F32),
        compiler_params=pltpu.CompilerParams(dimension_semantics=("parallel", "arbitrary"),
                                             vmem_limit_bytes=VMEM_LIMIT),
        name="nsa_prompt",
    )(q_t, gates_t, kc, ksel, sel_t, kwin, win_t, ovt)


def _nsa_sample_kernel(pt_ref, q_ref, gate_ref, kc_ref, pool_ref, snew_ref, wcache_ref, wnew_ref, ov_ref, ex_ref,
                       o_ref, buf, sem, sel_scr, oc_scr, m_scr, l_scr, acc_scr, *, n_pages, past):
    slot = _paged_tile(pt_ref, pool_ref, buf, sem, n_pages)
    t = pl.program_id(1)
    nt = pl.num_programs(1)
    ds = q_ref.shape[1]
    rows = HPG * ds
    n_cmp = kc_ref.shape[1]
    n_sbp = ov_ref.shape[1]
    blocks_per_tile = n_pages * PAGE_SIZE // L_SEL
    n_past_blocks = past // L_SEL
    keys = n_pages * PAGE_SIZE
    q = q_ref[0]
    tok4 = lax.broadcasted_iota(jnp.int32, (rows, 1), 0) % ds

    @pl.when(t == 0)
    def _():
        tpos = past + lax.broadcasted_iota(jnp.int32, (ds, 1), 0)
        for g in range(NSA_GROUPS):
            qs = _stack_heads(q, g)
            ridx = lax.broadcasted_iota(jnp.int32, (rows, n_cmp), 1)
            valid = (ridx >= 1) & (D_CMP * ridx + (L_CMP - D_CMP - 1) <= past + tok4)
            o_c, psum = _compressed_branch(qs, kc_ref[0, :, g * HD:(g + 1) * HD],
                                           kc_ref[0, :, KV_W + g * HD:KV_W + (g + 1) * HD], valid, ds)
            oc_scr[g] = o_c
            sel = _select_blocks(_hdot(psum, ov_ref[...]), tpos, n_past_blocks + 1)
            sel4 = jnp.concatenate([sel] * HPG, axis=0)
            for tt in range(sel_scr.shape[0]):
                sel_scr[tt, g] = sel4[:, tt * LANES:(tt + 1) * LANES]
            m_scr[g], l_scr[g], acc_scr[g] = _online_init(rows)

    kv_t = buf[slot]
    first_blk = t * blocks_per_tile
    ex = ex_ref[(first_blk % LANES) // blocks_per_tile]
    for g in range(NSA_GROUPS):
        qsb = _stack_heads(q, g).astype(BF16)
        bias = ((sel_scr[first_blk // LANES, g] - 1.0) * (-NEG_INF)).astype(BF16)
        qa = jnp.concatenate([qsb, bias], axis=1)
        ka = jnp.concatenate([kv_t[g * HD:(g + 1) * HD].astype(BF16), ex], axis=0)
        vt = kv_t[KV_W + g * HD:KV_W + (g + 1) * HD].astype(BF16)
        m_scr[g], l_scr[g], acc_scr[g] = _online((m_scr[g], l_scr[g], acc_scr[g]), _dg(qa, ka), None, vt, NT)

    @pl.when(t == nt - 1)
    def _():
        gates = gate_ref[0]
        knew_pos = lax.broadcasted_iota(jnp.int32, (rows, ds), 1)
        snew = snew_ref[0]
        wnew = wnew_ref[0]
        wc_t = wcache_ref[0]
        n_wc = wc_t.shape[1]
        for g in range(NSA_GROUPS):
            qsb = _stack_heads(q, g).astype(BF16)
            kcol = slice(g * HD, (g + 1) * HD)
            vcol = slice(KV_W + g * HD, KV_W + (g + 1) * HD)
            sel_new = sel_scr[n_past_blocks // LANES, g][:, n_past_blocks % LANES:n_past_blocks % LANES + 1]
            st = _online((m_scr[g], l_scr[g], acc_scr[g]), _dg(qsb, snew[:, kcol].astype(BF16), NT),
                         (sel_new > 0.5) & (knew_pos <= tok4), snew[:, vcol].astype(BF16))
            o_s = _online_out(st)
            dist_c = (n_wc + tok4) - lax.broadcasted_iota(jnp.int32, (rows, n_wc), 1)
            st = _online(_online_init(rows), _dg(qsb, wc_t[kcol].astype(BF16)),
                         (dist_c >= 0) & (dist_c < WIN), wc_t[vcol].astype(BF16), NT)
            dist_n = tok4 - knew_pos
            st = _online(st, _dg(qsb, wnew[:, kcol].astype(BF16), NT), (dist_n >= 0) & (dist_n < WIN),
                         wnew[:, vcol].astype(BF16))
            o_w = _online_out(st)
            o_c = oc_scr[g]
            for h in range(HPG):
                hd = HPG * g + h
                r = slice(h * ds, (h + 1) * ds)
                o_ref[0, :, hd * HD:(hd + 1) * HD] = (gates[:, 3 * hd:3 * hd + 1] * o_c[r]
                                                      + gates[:, 3 * hd + 1:3 * hd + 2] * o_s[r]
                                                      + gates[:, 3 * hd + 2:3 * hd + 3] * o_w[r])


def _nsa_sample(page_table, q, gates, kc, pool, sel_new, win_cache, win_new, n_pages):
    nb, ds = q.shape[:2]
    pages_per_seq = page_table.shape[1]
    past = pages_per_seq * PAGE_SIZE
    nt = pages_per_seq // n_pages
    n_cmp = kc.shape[1]
    n_sb = past // L_SEL + 1
    n_sbp = -(-n_sb // LANES) * LANES
    blocks_per_tile = n_pages * PAGE_SIZE // L_SEL
    keys = n_pages * PAGE_SIZE
    rows = HPG * ds
    ov = jnp.asarray(_overlap_matrix(n_cmp, n_sbp) * (np.arange(n_sbp) < n_sb)[None, :])
    ex = (np.arange(LANES)[None, :, None]
          == (np.arange(LANES // blocks_per_tile)[:, None, None] * blocks_per_tile + np.arange(keys)[None, None, :] // L_SEL))
    ex = jnp.asarray(ex, dtype=BF16)
    seq = lambda r, w: pl.BlockSpec((1, r, w), lambda b, t, pt: (b, 0, 0))
    cst = lambda shape: pl.BlockSpec(shape, lambda b, t, pt: (0,) * len(shape))
    return pl.pallas_call(
        functools.partial(_nsa_sample_kernel, n_pages=n_pages, past=past),
        grid_spec=pltpu.PrefetchScalarGridSpec(
            num_scalar_prefetch=1,
            grid=(nb, nt),
            in_specs=[seq(ds, W_A), seq(ds, LANES), seq(n_cmp, 2 * KV_W), pl.BlockSpec(memory_space=pl.ANY),
                      seq(ds, 2 * KV_W), seq(2 * KV_W, win_cache.shape[2]), seq(ds, 2 * KV_W),
                      cst((n_cmp, n_sbp)), cst(ex.shape)],
            out_specs=seq(ds, W_A),
            scratch_shapes=[pltpu.VMEM((2, 2 * KV_W, keys), F32),
                            pltpu.SemaphoreType.DMA((2,)),
                            pltpu.VMEM((n_sbp // LANES, NSA_GROUPS, rows, LANES), F32),
                            pltpu.VMEM((NSA_GROUPS, rows, HD), F32),
                            pltpu.VMEM((NSA_GROUPS, rows, 1), F32),
                            pltpu.VMEM((NSA_GROUPS, rows, 1), F32),
                            pltpu.VMEM((NSA_GROUPS, rows, HD), F32)]),
        out_shape=jax.ShapeDtypeStruct((nb, ds, W_A), F32),
        compiler_params=pltpu.CompilerParams(dimension_semantics=("arbitrary", "arbitrary"),
                                             vmem_limit_bytes=VMEM_LIMIT),
        name="nsa_sample",
    )(page_table, q, gates, kc, pool, sel_new, win_cache, win_new, ov, ex)


def _sum_heads(x, ones_ref):
    hi, lo = _split(x)
    return _dg(hi, ones_ref[...]) + _dg(lo, ones_ref[...])


def _rwkv_prep_kernel(rz_ref, prev_ref, mu_ref, w0_ref, wup_ref, a0_ref, aup_ref, kk_ref, ka_ref, rk_ref, gnb_ref,
                      ones_ref, a_ref, o_ref, y1_ref, y2_ref, yadd_ref, *, chunk, t_valid):
    rows = rz_ref.shape[1]
    n_chunks = rows // chunk
    shift = int(np.log2(chunk))
    z = rz_ref[0]
    ridx = lax.broadcasted_iota(jnp.int32, (rows, 1), 0)
    prev = jnp.where(ridx == 0, prev_ref[0, 0], pltpu.roll(z, 1, 0))
    zs = z + (prev - z) * mu_ref[...]
    r = zs[:, 0:W_R]
    k = zs[:, W_R:2 * W_R]
    v = zs[:, 2 * W_R:3 * W_R]
    wd = zs[:, 3 * W_R:3 * W_R + R_W]
    ad = zs[:, 3 * W_R + R_W:C_R]
    wpre = w0_ref[...] + _hdot(jnp.tanh(wd), wup_ref[...])
    w = -(jnp.maximum(-wpre, 0.0) + jnp.log(1.0 + jnp.exp(-jnp.abs(wpre)))) - 0.5
    ld = -jnp.exp(w)
    a = _sigmoid(a0_ref[...] + _hdot(ad, aup_ref[...]))
    kkn = k * kk_ref[...]
    kk = kkn / jnp.maximum(jnp.sqrt(_sum_heads(kkn * kkn, ones_ref)), 1e-12)
    k2 = k * (1.0 + (a - 1.0) * ka_ref[...])
    yadd_ref[0] = _sum_heads(r * k2 * rk_ref[...], ones_ref) * v + gnb_ref[...]
    if t_valid < chunk:
        live = (ridx & (chunk - 1)) < t_valid
        ld = jnp.where(live, ld, 0.0)
        kk = jnp.where(live, kk, 0.0)
        k2 = jnp.where(live, k2, 0.0)
    ti = lax.broadcasted_iota(jnp.int32, (rows, rows), 0)
    tj = lax.broadcasted_iota(jnp.int32, (rows, rows), 1)
    tri = jnp.where((jnp.right_shift(ti, shift) == jnp.right_shift(tj, shift)) & (tj <= ti), 1.0, 0.0).astype(BF16)
    ld_h = ld.astype(BF16)
    ld_m = (ld - ld_h.astype(F32)).astype(BF16)
    ld_l = (ld - ld_h.astype(F32) - ld_m.astype(F32)).astype(BF16)
    cum = _dg(tri, ld_h) + (_dg(tri, ld_m) + _dg(tri, ld_l))
    cum_end = jnp.concatenate([jnp.broadcast_to(cum[(c + 1) * chunk - 1:(c + 1) * chunk], (chunk, W_R))
                               for c in range(n_chunks)], axis=0)
    p_inv = jnp.exp(-cum)
    p_end = jnp.exp(cum_end - cum)
    kka = kk * a
    at = -kk * jnp.exp(cum - ld)
    rt = r * jnp.exp(cum)
    at_b, rt_b, v_b = at.astype(BF16), rt.astype(BF16), v.astype(BF16)
    bt_b, kt_b = (kka * p_inv).astype(BF16), (k2 * p_inv).astype(BF16)
    be_b, ke_b = (kka * p_end).astype(BF16), (k2 * p_end).astype(BF16)
    dec_end = jnp.exp(cum_end)

    row = lax.broadcasted_iota(jnp.int32, (chunk, PAIR_W), 0)
    col = lax.broadcasted_iota(jnp.int32, (chunk, PAIR_W), 1) & (R_HD - 1)
    strict, incl = col < row, col <= row
    eye_pair = jnp.where(col == row, 1.0, 0.0)
    n_levels = max(1, int(np.ceil(np.log2(chunk))))
    units = [(c, p) for c in range(n_chunks) for p in range(R_HEADS // 2)]
    idx = range(len(units))
    cut = lambda x, u: x[u[0] * chunk:(u[0] + 1) * chunk, u[1] * PAIR_W:(u[1] + 1) * PAIR_W]
    mm = [_dg(jnp.concatenate([cut(at_b, u), cut(rt_b, u)], axis=0),
              jnp.concatenate([_pair_diag(cut(bt_b, u)), _pair_diag(cut(kt_b, u))], axis=0), NT) for u in units]
    low = [jnp.where(strict, m[0:chunk, 0:PAIR_W], 0.0) for m in mm]
    m_k = [jnp.concatenate([jnp.where(strict, m[0:chunk, PAIR_W:], 0.0), jnp.where(incl, m[chunk:, PAIR_W:], 0.0)],
                           axis=0).astype(BF16) for m in mm]
    m_rb = [jnp.where(incl, m[chunk:, 0:PAIR_W], 0.0).astype(BF16) for m in mm]
    mv = [_dg(m_k[i], _pair_diag(cut(v_b, u))) for i, u in enumerate(units)]
    inv = [eye_pair + x for x in low]
    pw = [x.astype(BF16) for x in low]
    for lvl in range(1, n_levels + 1):
        if lvl == 1:
            pw = [_dg(x, _pair_diag(x)).astype(BF16) for x in pw]
        elif lvl <= n_levels:
            last = lvl == n_levels
            nxt = [_dg(pw[i], _pair_diag(inv[i].astype(BF16)) if last else
                       jnp.concatenate([_pair_diag(inv[i].astype(BF16)), _pair_diag(pw[i])], axis=1)) for i in idx]
            inv = [inv[i] + nxt[i][:, 0:PAIR_W] for i in idx]
            if not last:
                pw = [nxt[i][:, PAIR_W:].astype(BF16) for i in idx]
    inv_b = [x.astype(BF16) for x in inv]
    w = [_dg(inv_b[i], jnp.concatenate([_pair_diag(cut(at_b, u)), _pair_diag(mv[i][0:chunk].astype(BF16))], axis=1))
         for i, u in enumerate(units)]
    w_b = [x.astype(BF16) for x in w]
    y12 = [_dg(m_rb[i], jnp.concatenate([_pair_diag(w_b[i][:, 0:PAIR_W]), _pair_diag(w_b[i][:, PAIR_W:])], axis=1))
           for i in idx]
    prow = lax.broadcasted_iota(jnp.int32, (PAIR_W, PAIR_W), 0)
    pcol = lax.broadcasted_iota(jnp.int32, (PAIR_W, PAIR_W), 1)
    same_head = (prow < R_HD) == (pcol < R_HD)
    col_first = lax.broadcasted_iota(jnp.int32, (R_HD, PAIR_W), 1) < R_HD
    for i, u in enumerate(units):
        c, p = u
        rs = slice(c * chunk, (c + 1) * chunk)
        ls = slice(p * PAIR_W, (p + 1) * PAIR_W)
        a_mat = jnp.where(same_head, _dg(w_b[i][:, 0:PAIR_W], cut(be_b, u), TN), 0.0)
        a_mat = a_mat + jnp.where(prow == pcol, jnp.broadcast_to(dec_end[rs, ls][0:1], (PAIR_W, PAIR_W)), 0.0)
        a_ref[0, c, p] = a_mat.astype(BF16)
        o_full = _dg(jnp.concatenate([w_b[i][:, PAIR_W:], cut(v_b, u)], axis=0),
                     jnp.concatenate([cut(be_b, u), cut(ke_b, u)], axis=0), TN)
        o_ref[0, c, p] = jnp.where(col_first, o_full[0:R_HD], o_full[R_HD:])
        y1_ref[0, rs, ls] = (y12[i][:, 0:PAIR_W] + cut(rt, u)).astype(BF16)
        y2_ref[0, rs, ls] = y12[i][:, PAIR_W:] + mv[i][chunk:]


PAIR_W = 2 * R_HD


def _pair_diag(x):
    first = jnp.where(lax.broadcasted_iota(jnp.int32, x.shape, 1) < R_HD, 1.0, 0.0).astype(x.dtype)
    return jnp.concatenate([x * first, x * (1 - first)], axis=0)


def _rwkv_scan_kernel(a_ref, o_ref, y1_ref, y2_ref, yadd_ref, s0_ref, gng_ref, ones_ref, y_ref, st):
    @pl.when(pl.program_id(1) == 0)
    def _():
        st[...] = s0_ref[...]

    units = [(b, p) for b in range(st.shape[0]) for p in range(R_HEADS // 2)]
    lsl = lambda p: slice(p * PAIR_W, (p + 1) * PAIR_W)
    s_b = [st[b, p].astype(BF16) for b, p in units]
    ys = [_dg(y1_ref[b, :, lsl(p)], _pair_diag(s_b[i]), NT) + y2_ref[b, :, lsl(p)] for i, (b, p) in enumerate(units)]
    for i, (b, p) in enumerate(units):
        st[b, p] = _dg(s_b[i], a_ref[b, 0, p]) + o_ref[b, 0, p]
    mean = [_sum_heads(y, ones_ref) * (1.0 / R_HD) for y in ys]
    dev = [ys[i] - mean[i] for i in range(len(units))]
    var = [_sum_heads(d * d, ones_ref) * (1.0 / R_HD) for d in dev]
    for i, (b, p) in enumerate(units):
        y_ref[b, :, lsl(p)] = dev[i] * lax.rsqrt(var[i] + GN_EPS) * gng_ref[:, lsl(p)] + yadd_ref[b, :, lsl(p)]


def _rwkv(rz, shift0, state0, params, chunk, chunks_per_step, seqs_per_step, t_valid):
    mu, w0, wup, a0, aup, k_k, k_a, r_k, gn_g, gn_b = params
    b, t = rz.shape[:2]
    assert chunk == R_HD, "pair packing puts the chunk's time index on a head's 64 lanes"
    rows = chunk * chunks_per_step
    n_chunks = t // chunk
    n_pairs = R_HEADS // 2
    prev = jnp.concatenate([shift0[:, None, :], rz[:, rows - 1:t - 1:rows, :]], axis=1)[:, :, None, :]
    ones = jnp.asarray(np.kron(np.eye(R_HEADS), np.ones((R_HD, R_HD))), dtype=BF16)
    vec = lambda n: _const_spec((1, n))
    rowsd = lambda dt: jax.ShapeDtypeStruct((b, t, W_R), dt)
    row_spec = pl.BlockSpec((1, rows, W_R), lambda bi, c: (bi, c, 0))
    a_m, o_m, y1, y2, yadd = pl.pallas_call(
        functools.partial(_rwkv_prep_kernel, chunk=chunk, t_valid=t_valid),
        grid=(b, t // rows),
        in_specs=[pl.BlockSpec((1, rows, C_R), lambda bi, c: (bi, c, 0)),
                  pl.BlockSpec((1, 1, 1, C_R), lambda bi, c: (bi, c, 0, 0)),
                  vec(C_R), vec(W_R), _const_spec((R_W, W_R)), vec(W_R), _const_spec((R_A, W_R)),
                  vec(W_R), vec(W_R), vec(W_R), vec(W_R), _const_spec((W_R, W_R))],
        out_specs=[pl.BlockSpec((1, chunks_per_step, n_pairs, PAIR_W, PAIR_W), lambda bi, c: (bi, c, 0, 0, 0)),
                   pl.BlockSpec((1, chunks_per_step, n_pairs, R_HD, PAIR_W), lambda bi, c: (bi, c, 0, 0, 0)),
                   row_spec, row_spec, row_spec],
        out_shape=[jax.ShapeDtypeStruct((b, n_chunks, n_pairs, PAIR_W, PAIR_W), BF16),
                   jax.ShapeDtypeStruct((b, n_chunks, n_pairs, R_HD, PAIR_W), F32),
                   rowsd(BF16), rowsd(F32), rowsd(F32)],
        compiler_params=pltpu.CompilerParams(dimension_semantics=("parallel", "parallel"),
                                             vmem_limit_bytes=VMEM_LIMIT),
        name="rwkv_prep",
    )(rz, prev, mu, w0, wup, a0, aup, k_k, k_a, r_k, gn_b, ones)
    nb = seqs_per_step
    row1 = pl.BlockSpec((nb, chunk, W_R), lambda bi, c: (bi, c, 0))
    state_spec = pl.BlockSpec((nb, n_pairs, R_HD, PAIR_W), lambda bi, c: (bi, 0, 0, 0))
    pack = lambda s: jnp.transpose(s.reshape(b, n_pairs, 2, R_HD, R_HD), (0, 1, 3, 2, 4)).reshape(b, n_pairs, R_HD, PAIR_W)
    unpack = lambda s: jnp.transpose(s.reshape(b, n_pairs, R_HD, 2, R_HD), (0, 1, 3, 2, 4)).reshape(b, R_HEADS, R_HD, R_HD)
    y, state = pl.pallas_call(
        _rwkv_scan_kernel,
        grid=(b // nb, n_chunks),
        in_specs=[pl.BlockSpec((nb, 1, n_pairs, PAIR_W, PAIR_W), lambda bi, c: (bi, c, 0, 0, 0)),
                  pl.BlockSpec((nb, 1, n_pairs, R_HD, PAIR_W), lambda bi, c: (bi, c, 0, 0, 0)),
                  row1, row1, row1, state_spec, vec(W_R), _const_spec((PAIR_W, PAIR_W))],
        out_specs=[row1, state_spec],
        out_shape=[rowsd(F32), jax.ShapeDtypeStruct((b, n_pairs, R_HD, PAIR_W), F32)],
        compiler_params=pltpu.CompilerParams(dimension_semantics=("parallel", "arbitrary"),
                                             vmem_limit_bytes=VMEM_LIMIT),
        name="rwkv_scan",
    )(a_m, o_m, y1, y2, yadd, pack(state0), gn_g, ones[:PAIR_W, :PAIR_W])
    return y, unpack(state)


def _rope_tables(pos):
    half = HD // 2
    inv = ROPE_THETA ** (-jnp.arange(half, dtype=F32) / half)
    ang = pos.astype(F32)[:, None] * inv[None, :]
    cos, sin = jnp.cos(ang), jnp.sin(ang)
    reps = LANES // HD
    return (jnp.tile(jnp.concatenate([cos, cos], axis=-1), (1, reps)),
            jnp.tile(jnp.concatenate([-sin, sin], axis=-1), (1, reps)))


def _pick_tile(n, cap):
    t = cap
    while n % t:
        t //= 2
    return t


def kernel(x_prompt, x_sample, p_prompt, p_sample, cache_cmp_kv, cache_sel_kv, cache_win_kv, state_wkv, state_shift, page_table, norm_g, w_in, cmp_pe, cmp_w1, cmp_b1, cmp_w2, cmp_b2, w_pa, rwkv_mu, rwkv_w0, rwkv_w_up, rwkv_a0, rwkv_a_up, rwkv_k_k, rwkv_k_a, rwkv_r_k, rwkv_gn_g, rwkv_gn_b, w_pb, w_out, ple_norm_g, w_ple_gate, w_ple_proj, final_norm_g):
    b, s_len = x_prompt.shape[:2]
    db, ds = x_sample.shape[:2]
    depth = norm_g.shape[0]
    assert depth == 1, "single-layer trunk"
    n_pages = page_table.shape[1]
    past = n_pages * PAGE_SIZE
    win_buf = cache_win_kv.shape[2]
    n_pool = cache_cmp_kv.shape[1]
    i = 0

    w = w_in[i]
    o_q, o_kv, o_ng, o_ag = 0, W_A, W_A + 6 * KV_W, W_A + 6 * KV_W + 3 * NSA_HEADS
    w_pack = jnp.concatenate([w[:, o_q:o_ng], w[:, o_ag:], w[:, o_ng:o_ag],
                              jnp.zeros((D_MODEL, C_PACK - w.shape[1]), w.dtype)], axis=1).astype(BF16)
    cw = _compress_weights(cmp_pe[i], cmp_w1[i], cmp_b1[i], cmp_w2[i], cmp_b2[i])
    row = lambda u: u.reshape(1, -1)
    rw = (row(rwkv_mu[i]), row(rwkv_w0[i]), rwkv_w_up[i], row(rwkv_a0[i]), rwkv_a_up[i], row(rwkv_k_k[i]),
          row(rwkv_k_a[i]), row(rwkv_r_k[i]), row(rwkv_gn_g[i]), row(rwkv_gn_b[i]))
    ow = (w_pa[i].astype(BF16), w_pb[i].astype(BF16), w_out[i].astype(BF16), row(ple_norm_g[i]),
          w_ple_gate[i].astype(BF16), w_ple_proj[i].astype(BF16), row(final_norm_g))

    tm = _pick_tile(s_len, 256)
    cos_p, sin_p = _rope_tables(jnp.arange(s_len, dtype=jnp.int32))
    xp2 = x_prompt.reshape(b * s_len, D_MODEL)
    q, cmp_p, sel_p, win_p, gates, sa, rz, sr, mg, ksel, kwin = _project(xp2, row(norm_g[i]), w_pack, cos_p, sin_p,
                                                                        tm, True)
    kc_p = _compress_seq(cmp_p, cw)
    seq = lambda u: u.reshape(b, s_len, u.shape[-1])
    o_a = _nsa_prompt(q, gates, kc_p, ksel, sel_p, kwin, win_p, tq=128, tk=min(512, s_len))
    chunk = 64
    o_b, st_p = _rwkv(seq(rz), jnp.zeros((b, C_R), F32), jnp.zeros((b, R_HEADS, R_HD, R_HD), F32), rw, chunk,
                      chunks_per_step=2, seqs_per_step=_pick_tile(b, 4), t_valid=chunk)
    y_p = _merge(xp2, o_a.reshape(b * s_len, W_A), o_b.reshape(b * s_len, W_R), sa, sr, mg,
                 p_prompt[i].reshape(b * s_len, D_PLE), *ow, tm=tm)
    kv_shape = lambda u, n, t: u.reshape(n, t, 2, NSA_GROUPS, HD)
    from_cm = lambda u: jnp.transpose(u.reshape(u.shape[0], 2, NSA_GROUPS, HD, u.shape[2]), (0, 4, 1, 2, 3))
    to_cm = lambda u: jnp.transpose(u, (0, 2, 3, 4, 1)).reshape(u.shape[0], 2 * KV_W, u.shape[1])
    cmp_kv_p = from_cm(cmp_p)
    sel_kv_p = from_cm(sel_p)
    win_kv_p = from_cm(win_p[:, :, s_len - min(WIN, s_len):])
    shift_p = seq(rz)[:, -1]

    n_s = db * ds
    cos_s, sin_s = _rope_tables(jnp.tile(past + jnp.arange(ds, dtype=jnp.int32), db))
    xs2 = x_sample.reshape(n_s, D_MODEL)
    q, cmp_s, sel_s, win_s, gates, sa, rz, sr, mg = _project(xs2, row(norm_g[i]), w_pack, cos_s, sin_s, n_s, False)
    kc_s = _compress_paged(page_table, to_cm(cache_cmp_kv[i]), cw, n_pages=_pick_tile(n_pages, 64))
    seqs = lambda u: u.reshape(db, ds, u.shape[-1])
    o_a = _nsa_sample(page_table, seqs(q), seqs(gates), kc_s, to_cm(cache_sel_kv[i]),
                      seqs(sel_s), to_cm(cache_win_kv[i]), seqs(win_s), n_pages=_pick_tile(n_pages, 32))
    pad_t = chunk
    rz_pad = jnp.pad(seqs(rz), ((0, 0), (0, pad_t - ds), (0, 0)))
    o_b, st_s = _rwkv(rz_pad, state_shift[i], state_wkv[i], rw, pad_t, chunks_per_step=1,
                      seqs_per_step=_pick_tile(db, 4), t_valid=ds)
    y_s = _merge(xs2, o_a.reshape(n_s, W_A), o_b[:, :ds].reshape(n_s, W_R), sa, sr, mg,
                 p_sample[i].reshape(n_s, D_PLE), *ow, tm=n_s)
    win_kv_s = jnp.concatenate([cache_win_kv[i], kv_shape(win_s, db, ds)], axis=1)[:, ds:]
    shift_s = seqs(rz)[:, -1]

    return (y_p.reshape(b, s_len, D_MODEL), y_s.reshape(db, ds, D_MODEL),
            cmp_kv_p[None], kv_shape(cmp_s, db, ds)[None], sel_kv_p[None], kv_shape(sel_s, db, ds)[None],
            win_kv_p[None], win_kv_s[None], st_p[None], st_s[None], shift_p[None], shift_s[None])
```

```python
import functools

import numpy as np
import jax
import jax.numpy as jnp
from jax import lax
from jax.experimental import pallas as pl
from jax.experimental.pallas import tpu as pltpu

F32 = jnp.float32
BF16 = jnp.bfloat16

D_MODEL = 1024
D_PLE = 256
NSA_HEADS = 8
NSA_GROUPS = 2
HD = 64
HPG = NSA_HEADS // NSA_GROUPS
W_A = NSA_HEADS * HD
KV_W = NSA_GROUPS * HD
L_CMP = 32
D_CMP = 16
CMP_HID = 128
L_SEL = 64
N_SEL = 16
WIN = 512
FORCE_SCORE = 1e4
NEG_INF = -1e30
R_HEADS = 8
R_HD = 64
W_R = R_HEADS * R_HD
R_W = 64
R_A = 64
C_R = 3 * W_R + R_W + R_A
GN_EPS = 64e-5
ROPE_THETA = 10000.0
NORM_EPS = 1e-6
PAGE_SIZE = 128

LANES = 128
CHUNKS_PER_PAGE = PAGE_SIZE // D_CMP
CHUNK_W = D_CMP * 2 * KV_W
VMEM_LIMIT = 56 * 1024 * 1024

C_Q = 0
C_KV = 512
C_AG = 1280
C_RZ = 1792
C_RG = 3456
C_MG = 3968
C_NG = 6016
C_PACK = 6144

NN = ((1,), (0,))
NT = ((1,), (1,))
TN = ((0,), (0,))


def _dg(a, b, dims=NN):
    return lax.dot_general(a, b, (dims, ((), ())), preferred_element_type=F32)


def _split(x):
    hi = x.astype(BF16)
    lo = (x - hi.astype(F32)).astype(BF16)
    return hi, lo


def _hdot_s(a, b, dims=NN):
    (ah, al), (bh, bl) = a, b
    return _dg(ah, bh, dims) + (_dg(ah, bl, dims) + _dg(al, bh, dims))


def _hdot(a, b, dims=NN):
    return _hdot_s(_split(a), _split(b), dims)


def _sigmoid(x):
    return 1.0 / (1.0 + jnp.exp(-x))


def _silu(x):
    return x * _sigmoid(x)


def _rms(x, g):
    return x * lax.rsqrt(jnp.mean(x * x, axis=-1, keepdims=True) + NORM_EPS) * g


def _const_spec(shape):
    zeros = (0,) * len(shape)
    return pl.BlockSpec(shape, lambda *_: zeros)


def _rope128(z, cos, sin, first):
    partner = jnp.where(first, pltpu.roll(z, LANES - HD // 2, 1), pltpu.roll(z, HD // 2, 1))
    return z * cos + partner * sin


def _proj_kernel(x_ref, g_ref, w_ref, cos_ref, sin_ref,
                 q_ref, cmp_ref, sel_ref, win_ref, gate_ref, sa_ref, rz_ref, sr_ref, mg_ref, *aug_refs,
                 seq_major, n_pos_blocks):
    hb = _rms(x_ref[...], g_ref[...]).astype(BF16)
    cos = cos_ref[...]
    sin = sin_ref[...]
    tm = cos.shape[0]
    lane = lax.broadcasted_iota(jnp.int32, cos.shape, 1)
    first = (lane % HD) < (HD // 2)

    def mm(a, b):
        return jnp.dot(hb, w_ref[:, a:b], preferred_element_type=F32)

    zq = mm(C_Q, C_Q + W_A)
    q = jnp.concatenate([_rope128(zq[:, c * LANES:(c + 1) * LANES], cos, sin, first) for c in range(W_A // LANES)],
                        axis=1)
    gates = _sigmoid(mm(C_NG, C_PACK))
    if seq_major:
        q_ref[0] = q.T
        gate_ref[0] = gates.T
    else:
        q_ref[...] = q
        gate_ref[...] = gates
    for i, ref in enumerate((cmp_ref, sel_ref, win_ref)):
        zkv = mm(C_KV + 2 * KV_W * i, C_KV + 2 * KV_W * (i + 1))
        k_rot = _rope128(zkv[:, 0:KV_W], cos, sin, first)
        if seq_major:
            ref[0] = jnp.concatenate([k_rot, zkv[:, KV_W:2 * KV_W]], axis=1).T
            if i > 0:
                aug_ref = aug_refs[i - 1]
                if i == 1:
                    t_in_seq = ((pl.program_id(0) % n_pos_blocks) * tm
                                + lax.broadcasted_iota(jnp.int32, cos.shape, 0))
                    tail = jnp.where(jnp.right_shift(t_in_seq, int(np.log2(L_SEL))) == lane - HD, 1.0, 0.0)
                else:
                    tail = jnp.zeros(cos.shape, F32)
                for g in range(NSA_GROUPS):
                    k_g = k_rot if g == 0 else pltpu.roll(k_rot, LANES - g * HD, 1)
                    aug_ref[0, g] = jnp.where(lane < HD, k_g, tail).astype(BF16)
        else:
            ref[:, 0:KV_W] = k_rot
            ref[:, KV_W:2 * KV_W] = zkv[:, KV_W:2 * KV_W]
    sa_ref[...] = _silu(mm(C_AG, C_AG + W_A))
    rz_ref[...] = mm(C_RZ, C_RZ + C_R)
    sr_ref[...] = _silu(mm(C_RG, C_RG + W_R))
    mg_ref[...] = _sigmoid(mm(C_MG, C_MG + 2 * D_MODEL))


def _project(x2d, norm_g, w_pack, cos_t, sin_t, tm, seq_major):
    n = x2d.shape[0]
    seq_len = cos_t.shape[0]
    n_pos_blocks = seq_len // tm
    n_seq = n // seq_len
    widths = (W_A, 2 * KV_W, 2 * KV_W, 2 * KV_W, LANES, W_A, C_R, W_R, 2 * D_MODEL)
    row = lambda w: pl.BlockSpec((tm, w), lambda i: (i, 0))
    pos = pl.BlockSpec((tm, LANES), lambda i: (i % n_pos_blocks, 0))
    out_specs = [row(w) for w in widths]
    out_shape = [jax.ShapeDtypeStruct((n, w), F32) for w in widths]
    in_specs = [row(D_MODEL), _const_spec((1, D_MODEL)), _const_spec((D_MODEL, C_PACK)), pos, pos]
    args = [x2d, norm_g, w_pack, cos_t, sin_t]
    if seq_major:
        assert seq_len // L_SEL <= LANES - HD, "selection-block one-hot must fit beside the key"
        for o in range(5):
            out_specs[o] = pl.BlockSpec((1, widths[o], tm), lambda i: (i // n_pos_blocks, 0, i % n_pos_blocks))
            out_shape[o] = jax.ShapeDtypeStruct((n_seq, widths[o], seq_len), F32)
        for _ in range(2):
            out_specs.append(pl.BlockSpec((1, NSA_GROUPS, tm, LANES),
                                          lambda i: (i // n_pos_blocks, 0, i % n_pos_blocks, 0)))
            out_shape.append(jax.ShapeDtypeStruct((n_seq, NSA_GROUPS, seq_len, LANES), BF16))
    return pl.pallas_call(
        functools.partial(_proj_kernel, seq_major=seq_major, n_pos_blocks=n_pos_blocks),
        grid=(n // tm,),
        in_specs=in_specs,
        out_specs=out_specs,
        out_shape=out_shape,
        compiler_params=pltpu.CompilerParams(dimension_semantics=("parallel",), vmem_limit_bytes=VMEM_LIMIT),
        name="project",
    )(*args)


def _merge_kernel(x_ref, oa_ref, ob_ref, sa_ref, sr_ref, mg_ref, p_ref,
                  wpa_ref, wpb_ref, wout_ref, pg_ref, wgate_ref, wproj_ref, fg_ref, y_ref):
    bdot = lambda a, w_ref: jnp.dot(a.astype(BF16), w_ref[...], preferred_element_type=F32)
    ya = bdot(oa_ref[...] * sa_ref[...], wpa_ref)
    yb = bdot(ob_ref[...] * sr_ref[...], wpb_ref)
    m = mg_ref[:, 0:D_MODEL] * ya + mg_ref[:, D_MODEL:2 * D_MODEL] * yb
    x2 = x_ref[...] + bdot(m, wout_ref)
    gate = _sigmoid(bdot(_rms(x2, pg_ref[...]), wgate_ref))
    x3 = x2 + gate * bdot(p_ref[...], wproj_ref)
    y_ref[...] = _rms(x3, fg_ref[...])


def _merge(x2d, oa, ob, sa, sr, mg, p2d, wpa, wpb, wout, pg, wgate, wproj, fg, tm):
    n = x2d.shape[0]
    row = lambda w: pl.BlockSpec((tm, w), lambda i: (i, 0))
    return pl.pallas_call(
        _merge_kernel,
        grid=(n // tm,),
        in_specs=[row(D_MODEL), row(W_A), row(W_R), row(W_A), row(W_R), row(2 * D_MODEL), row(D_PLE),
                  _const_spec((W_A, D_MODEL)), _const_spec((W_R, D_MODEL)), _const_spec((D_MODEL, D_MODEL)),
                  _const_spec((1, D_MODEL)), _const_spec((D_MODEL, D_MODEL)), _const_spec((D_PLE, D_MODEL)),
                  _const_spec((1, D_MODEL))],
        out_specs=row(D_MODEL),
        out_shape=jax.ShapeDtypeStruct((n, D_MODEL), F32),
        compiler_params=pltpu.CompilerParams(dimension_semantics=("parallel",), vmem_limit_bytes=VMEM_LIMIT),
        name="merge",
    )(x2d, oa, ob, sa, sr, mg, p2d, wpa, wpb, wout, pg, wgate, wproj, fg)


def _page_copies(pt_ref, pool_ref, buf, sem, b, t, slot, n_pages):
    return [pltpu.make_async_copy(pool_ref.at[pt_ref[b, t * n_pages + p]],
                                  buf.at[slot, :, pl.ds(p * PAGE_SIZE, PAGE_SIZE)], sem.at[slot])
            for p in range(n_pages)]


def _paged_tile(pt_ref, pool_ref, buf, sem, n_pages):
    b, t = pl.program_id(0), pl.program_id(1)
    nt = pl.num_programs(1)
    step = b * nt + t
    slot = step % 2

    @pl.when(step == 0)
    def _():
        for cp in _page_copies(pt_ref, pool_ref, buf, sem, b, t, slot, n_pages):
            cp.start()

    nxt = step + 1

    @pl.when(nxt < pl.num_programs(0) * nt)
    def _():
        for cp in _page_copies(pt_ref, pool_ref, buf, sem, nxt // nt, nxt % nt, 1 - slot, n_pages):
            cp.start()

    for cp in _page_copies(pt_ref, pool_ref, buf, sem, b, t, slot, n_pages):
        cp.wait()
    return slot


CMP_HALF = NSA_GROUPS * CMP_HID


def _compress_core(page_t, xs, perm_ref, w1_ref, pe_ref, b1_ref, w2_ref, b2_ref, carry, out_ref):
    n_pages = xs.shape[0]
    rows = n_pages * CHUNKS_PER_PAGE
    for p in range(n_pages):
        xp = _dg(perm_ref[...], page_t(p).astype(BF16), NT)
        xs[p] = xp.reshape(D_CMP, CHUNKS_PER_PAGE, 2 * KV_W)
    ridx = lax.broadcasted_iota(jnp.int32, (rows, CMP_HALF), 0)
    for kv in range(2):
        cs = slice(kv * KV_W, (kv + 1) * KV_W)
        x = jnp.concatenate([xs[:, j, :, cs].reshape(rows, KV_W) for j in range(D_CMP)], axis=1)
        x = jnp.concatenate([x, pe_ref[kv]], axis=0).astype(BF16)
        z = jnp.dot(x, w1_ref[kv], preferred_element_type=F32)
        pos = z[rows:rows + 1, 0:CMP_HALF] + z[rows + 1:rows + 2, CMP_HALF:]
        lo = z[0:rows, 0:CMP_HALF]
        lo_prev = jnp.where(ridx == 0, carry[kv], pltpu.roll(lo, 1, 0))
        carry[kv] = lo[rows - 1:rows]
        hid = lo_prev + z[0:rows, CMP_HALF:] + pos + b1_ref[kv]
        out_ref[0, :, cs] = jnp.dot(_silu(hid).astype(BF16), w2_ref[kv], preferred_element_type=F32) + b2_ref[kv]


def _compress_paged_kernel(pt_ref, pool_ref, perm_ref, w1_ref, pe_ref, b1_ref, w2_ref, b2_ref, out_ref,
                           buf, sem, xs, carry, *, n_pages):
    slot = _paged_tile(pt_ref, pool_ref, buf, sem, n_pages)

    @pl.when(pl.program_id(1) == 0)
    def _():
        carry[...] = jnp.zeros_like(carry)

    _compress_core(lambda p: buf[slot, :, p * PAGE_SIZE:(p + 1) * PAGE_SIZE], xs, perm_ref, w1_ref, pe_ref, b1_ref,
                   w2_ref, b2_ref, carry, out_ref)


def _compress_seq_kernel(x_ref, perm_ref, w1_ref, pe_ref, b1_ref, w2_ref, b2_ref, out_ref, xs, carry):
    carry[...] = jnp.zeros_like(carry)
    _compress_core(lambda p: x_ref[0, :, p * PAGE_SIZE:(p + 1) * PAGE_SIZE], xs, perm_ref, w1_ref, pe_ref, b1_ref,
                   w2_ref, b2_ref, carry, out_ref)


def _compress_specs(cst):
    return [cst((PAGE_SIZE, PAGE_SIZE)), cst((2, D_CMP * KV_W, 2 * CMP_HALF)), cst((2, 8, D_CMP * KV_W)),
            cst((2, 1, CMP_HALF)), cst((2, CMP_HALF, KV_W)), cst((2, 1, KV_W))]


def _compress_scratch(n_pages):
    return [pltpu.VMEM((n_pages, D_CMP, CHUNKS_PER_PAGE, 2 * KV_W), F32), pltpu.VMEM((2, 1, CMP_HALF), F32)]


def _compress_paged(page_table, pool, cw, n_pages):
    nb, pages_per_seq = page_table.shape
    nt = pages_per_seq // n_pages
    toks = n_pages * PAGE_SIZE
    rows = toks // D_CMP
    cst = lambda shape: pl.BlockSpec(shape, lambda b, t, pt: (0,) * len(shape))
    return pl.pallas_call(
        functools.partial(_compress_paged_kernel, n_pages=n_pages),
        grid_spec=pltpu.PrefetchScalarGridSpec(
            num_scalar_prefetch=1,
            grid=(nb, nt),
            in_specs=[pl.BlockSpec(memory_space=pl.ANY)] + _compress_specs(cst),
            out_specs=pl.BlockSpec((1, rows, 2 * KV_W), lambda b, t, pt: (b, t, 0)),
            scratch_shapes=[pltpu.VMEM((2, 2 * KV_W, toks), F32), pltpu.SemaphoreType.DMA((2,))]
            + _compress_scratch(n_pages)),
        out_shape=jax.ShapeDtypeStruct((nb, nt * rows, 2 * KV_W), F32),
        compiler_params=pltpu.CompilerParams(dimension_semantics=("arbitrary", "arbitrary"),
                                             vmem_limit_bytes=VMEM_LIMIT),
        name="compress_paged",
    )(page_table, pool, *cw)


def _compress_seq(x_t, cw):
    nb, _, toks = x_t.shape
    rows = toks // D_CMP
    return pl.pallas_call(
        _compress_seq_kernel,
        grid=(nb,),
        in_specs=[pl.BlockSpec((1, 2 * KV_W, toks), lambda b: (b, 0, 0))] + _compress_specs(_const_spec),
        out_specs=pl.BlockSpec((1, rows, 2 * KV_W), lambda b: (b, 0, 0)),
        out_shape=jax.ShapeDtypeStruct((nb, rows, 2 * KV_W), F32),
        scratch_shapes=_compress_scratch(toks // PAGE_SIZE),
        compiler_params=pltpu.CompilerParams(dimension_semantics=("parallel",), vmem_limit_bytes=VMEM_LIMIT),
        name="compress_seq",
    )(x_t, *cw)


def _compress_weights(cmp_pe, cmp_w1, cmp_b1, cmp_w2, cmp_b2):
    eye_g = jnp.eye(NSA_GROUPS, dtype=F32)
    w1 = cmp_w1.reshape(2, 2, D_CMP, HD, CMP_HID)
    w1j = jnp.einsum('kljdf,gG->kjgdlGf', w1, eye_g).reshape(2, D_CMP * KV_W, 2 * CMP_HALF)
    pe = cmp_pe.reshape(2, 2, D_CMP, HD)
    pe = jnp.broadcast_to(pe[:, :, :, None, :], (2, 2, D_CMP, NSA_GROUPS, HD))
    pej = jnp.pad(pe.reshape(2, 2, D_CMP * KV_W), ((0, 0), (0, 6), (0, 0)))
    b1big = jnp.broadcast_to(cmp_b1[:, None, :], (2, NSA_GROUPS, CMP_HID)).reshape(2, 1, CMP_HALF)
    w2big = jnp.einsum('kfd,gG->kgfGd', cmp_w2, eye_g).reshape(2, CMP_HALF, KV_W)
    b2big = jnp.broadcast_to(cmp_b2[:, None, :], (2, NSA_GROUPS, HD)).reshape(2, 1, KV_W)
    dst = np.arange(PAGE_SIZE)
    perm = (dst[:, None] % CHUNKS_PER_PAGE) * D_CMP + dst[:, None] // CHUNKS_PER_PAGE == dst[None, :]
    return jnp.asarray(perm, dtype=BF16), w1j.astype(BF16), pej, b1big, w2big.astype(BF16), b2big


def _overlap_matrix(n_rows, n_cols):
    c_start = (np.arange(n_rows) - 1) * D_CMP
    s_start = np.arange(n_cols) * L_SEL
    ov = (c_start[:, None] <= s_start[None, :] + L_SEL - 1) & (c_start[:, None] + L_CMP - 1 >= s_start[None, :])
    ov &= (np.arange(n_rows) >= 1)[:, None]
    return ov.astype(np.float32)


def _online(state, s, msk, v_bf, v_dims=NN):
    m, l, acc = state
    if msk is not None:
        s = jnp.where(msk, s, NEG_INF)
    m_new = jnp.maximum(m, jnp.max(s, axis=-1, keepdims=True))
    alpha = jnp.exp(m - m_new)
    p = jnp.exp(s - m_new)
    if msk is not None:
        p = jnp.where(msk, p, 0.0)
    l = alpha * l + jnp.sum(p, axis=-1, keepdims=True)
    acc = alpha * acc + _dg(p.astype(BF16), v_bf, v_dims)
    return m_new, l, acc


def _online_init(rows):
    return jnp.full((rows, 1), NEG_INF, F32), jnp.zeros((rows, 1), F32), jnp.zeros((rows, HD), F32)


def _online_out(state):
    _, l, acc = state
    return acc / jnp.maximum(l, 1e-30)


def _stack_heads(q, g):
    return jnp.concatenate([q[:, (HPG * g + h) * HD:(HPG * g + h + 1) * HD] for h in range(HPG)], axis=0) * (HD ** -0.5)


def _compressed_branch(qs, kc, vc, valid, n_tok):
    s = jnp.where(valid, _hdot(qs, kc, NT), NEG_INF)
    m = jnp.max(s, axis=-1, keepdims=True)
    p = jnp.where(valid, jnp.exp(s - m), 0.0)
    p = p / jnp.maximum(jnp.sum(p, axis=-1, keepdims=True), 1e-30)
    o = jnp.dot(p.astype(BF16), vc.astype(BF16), preferred_element_type=F32)
    psum = p[0:n_tok]
    for h in range(1, HPG):
        psum = psum + p[h * n_tok:(h + 1) * n_tok]
    return o, psum


def _select_blocks(imp, tpos, n_real):
    blk = lax.broadcasted_iota(jnp.int32, imp.shape, 1)
    cur = tpos // L_SEL
    forced = (blk == 0) | (blk == cur) | (blk == cur - 1)
    imp = jnp.where(forced, FORCE_SCORE, imp)
    imp = jnp.where(blk * L_SEL > tpos, -1.0, imp)
    imp = jnp.where(blk >= n_real, -2.0, imp)
    cnt = jnp.zeros(imp.shape, F32)
    for mcol in range(n_real):
        col = imp[:, mcol:mcol + 1]
        ahead = (col > imp) | ((col == imp) & (blk > mcol))
        cnt = cnt + jnp.where(ahead, 1.0, 0.0)
    return jnp.where(cnt < N_SEL, 1.0, 0.0)


def _select_blocks_t(imp_t, tpos_row, n_real):
    sub = 8
    blk = lax.broadcasted_iota(jnp.int32, imp_t.shape, 0)
    cur = jnp.right_shift(tpos_row, int(np.log2(L_SEL)))
    forced = (blk == 0) | (blk == cur) | (blk == cur - 1)
    imp_t = jnp.where(forced, FORCE_SCORE, imp_t)
    imp_t = jnp.where(blk * L_SEL > tpos_row, -1.0, imp_t)
    slabs = [imp_t[s0:s0 + sub] for s0 in range(0, n_real, sub)]
    cnt = [jnp.zeros(s.shape, F32) for s in slabs]
    sub_iota = lax.broadcasted_iota(jnp.int32, slabs[0].shape, 0)
    for m in range(n_real):
        row = jnp.broadcast_to(imp_t[m:m + 1], slabs[0].shape)
        for si, slab in enumerate(slabs):
            ge = jnp.where(row >= slab, 1.0, 0.0)
            gt = jnp.where(row > slab, 1.0, 0.0)
            if si * sub > m:
                ahead = ge
            elif si * sub + sub - 1 < m:
                ahead = gt
            else:
                ahead = jnp.where(sub_iota > m - si * sub, ge, gt)
            cnt[si] = cnt[si] + ahead
    return jnp.where(jnp.concatenate(cnt, axis=0) < N_SEL, 1.0, 0.0)


ACC_ROWS = HD + 16


def _with_ones(v_t):
    return jnp.concatenate([v_t, jnp.ones((ACC_ROWS - HD, v_t.shape[1]), BF16)], axis=0)


def _sweep_t(units, states):
    states = list(states)
    s_t = {0: units[0][1]()}
    pending = None

    def finish(k, alpha, p, values):
        m, acc = states[k]
        states[k] = (m, alpha * acc + _dg(values(), p))

    for u, (k, _, msk, values) in enumerate(units):
        if u + 1 < len(units):
            s_t[u + 1] = units[u + 1][1]()
        m, acc = states[k]
        s = s_t.pop(u)
        if msk is not None:
            s = jnp.where(msk, s, NEG_INF)
        m_new = jnp.maximum(m, jnp.max(s, axis=0, keepdims=True))
        alpha = jnp.exp(m - m_new)
        p = jnp.exp(s - m_new)
        states[k] = (m_new, acc)
        if pending is not None:
            finish(*pending)
        pending = (k, alpha, p.astype(BF16), values)
    finish(*pending)
    return tuple(states)


def _sweep_t_init(cols):
    return jnp.full((1, cols), NEG_INF, F32), jnp.zeros((ACC_ROWS, cols), F32)


def _sweep_t_out(state):
    _, acc = state
    return acc[0:HD] / jnp.maximum(acc[HD:HD + 1], 1e-30)


def _nsa_prompt_kernel(qt_ref, gatet_ref, kc_ref, ksel_ref, vsel_ref, kwin_ref, vwin_ref, ovt_ref, o_ref, *, tq, tk):
    i = pl.program_id(1)
    n_cmp = kc_ref.shape[1]
    n_sb = ovt_ref.shape[0]
    cols = HPG * tq
    aug = LANES - HD
    t0 = i * tq
    tpos_row = t0 + lax.broadcasted_iota(jnp.int32, (1, tq), 1)
    tpos4 = jnp.concatenate([tpos_row] * HPG, axis=1)
    n_full = t0 // tk
    w_start = pl.multiple_of(jnp.maximum(t0 - WIN, 0), tq)
    gates_t = gatet_ref[0]
    groups = range(NSA_GROUPS)
    n_w = WIN + tq
    dist = tpos4 - (w_start + lax.broadcasted_iota(jnp.int32, (n_w, 1), 0))
    ridx = lax.broadcasted_iota(jnp.int32, (n_cmp, cols), 0)
    valid = (ridx >= 1) & (D_CMP * ridx + (L_CMP - D_CMP - 1) <= tpos4)
    qa, o_c = [], []
    for g in groups:
        q_t = jnp.concatenate([qt_ref[0, (HPG * g + h) * HD:(HPG * g + h + 1) * HD, :] for h in range(HPG)],
                              axis=1) * (HD ** -0.5)
        s = jnp.where(valid, _hdot(kc_ref[0, :, g * HD:(g + 1) * HD], q_t), NEG_INF)
        p = jnp.where(valid, jnp.exp(s - jnp.max(s, axis=0, keepdims=True)), 0.0)
        p = p / jnp.maximum(jnp.sum(p, axis=0, keepdims=True), 1e-30)
        o_c.append(_dg(kc_ref[0, :, KV_W + g * HD:KV_W + (g + 1) * HD].astype(BF16), p.astype(BF16), TN))
        psum = p[:, 0:tq]
        for h in range(1, HPG):
            psum = psum + p[:, h * tq:(h + 1) * tq]
        sel_t = _select_blocks_t(_hdot(ovt_ref[...], psum), tpos_row, n_sb)
        bias = jnp.concatenate([(sel_t - 1.0) * (-NEG_INF)] * HPG, axis=1)
        if n_sb < aug:
            bias = jnp.concatenate([bias, jnp.zeros((aug - n_sb, cols), F32)], axis=0)
        qa.append(jnp.concatenate([q_t, bias], axis=0).astype(BF16))

    def sel_unit(g, j, causal):
        k0 = pl.multiple_of(j * tk, tk)
        kpos = k0 + lax.broadcasted_iota(jnp.int32, (tk, 1), 0)
        return (g, lambda: _dg(ksel_ref[0, g, pl.ds(k0, tk), :], qa[g]), (kpos <= tpos4) if causal else None,
                lambda: _with_ones(vsel_ref[0, g * HD:(g + 1) * HD, pl.ds(k0, tk)].astype(BF16)))

    def win_unit(g):
        return (NSA_GROUPS + g, lambda: _dg(kwin_ref[0, g, pl.ds(w_start, n_w), :], qa[g]),
                (dist >= 0) & (dist < WIN),
                lambda: _with_ones(vwin_ref[0, g * HD:(g + 1) * HD, pl.ds(w_start, n_w)].astype(BF16)))

    states = lax.fori_loop(
        0, n_full // 2,
        lambda jj, st: _sweep_t([sel_unit(g, 2 * jj + d, False) for d in range(2) for g in groups], st),
        tuple(_sweep_t_init(cols) for g in groups))
    states = lax.cond(n_full % 2 == 1, lambda st: _sweep_t([sel_unit(g, n_full - 1, False) for g in groups], st),
                      lambda st: st, states)
    states = _sweep_t([u for g in groups for u in (sel_unit(g, n_full, True), win_unit(g))],
                      states + tuple(_sweep_t_init(cols) for g in groups))
    heads_out = []
    for g in groups:
        o_s = _sweep_t_out(states[g])
        o_w = _sweep_t_out(states[NSA_GROUPS + g])
        for h in range(HPG):
            hd = HPG * g + h
            c = slice(h * tq, (h + 1) * tq)
            heads_out.append(gates_t[3 * hd:3 * hd + 1] * o_c[g][:, c] + gates_t[3 * hd + 1:3 * hd + 2] * o_s[:, c]
                             + gates_t[3 * hd + 2:3 * hd + 3] * o_w[:, c])
    o_ref[0] = jnp.concatenate(heads_out, axis=0).T


def _nsa_prompt(q_t, gates_t, kc, ksel, sel_t, kwin, win_t, tq, tk):
    b, _, t = q_t.shape
    n_cmp = kc.shape[1]
    n_sb = t // L_SEL
    ovt = jnp.asarray(_overlap_matrix(n_cmp, n_sb).T)
    keys = pl.BlockSpec((1, NSA_GROUPS, t, LANES), lambda bi, i: (bi, 0, 0, 0))
    values = pl.BlockSpec((1, KV_W, t), lambda bi, i: (bi, 1, 0))
    return pl.pallas_call(
        functools.partial(_nsa_prompt_kernel, tq=tq, tk=tk),
        grid=(b, t // tq),
        in_specs=[pl.BlockSpec((1, W_A, tq), lambda bi, i: (bi, 0, i)),
                  pl.BlockSpec((1, LANES, tq), lambda bi, i: (bi, 0, i)),
                  pl.BlockSpec((1, n_cmp, 2 * KV_W), lambda bi, i: (bi, 0, 0)),
                  keys, values, keys, values, _const_spec((n_sb, n_cmp))],
        out_specs=pl.BlockSpec((1, tq, W_A), lambda bi, i: (bi, i, 0)),
        out_shape=jax.ShapeDtypeStruct((b, t, W_A), F32),
        compiler_params=pltpu.CompilerParams(dimension_semantics=("parallel", "arbitrary"),
                                             vmem_limit_bytes=VMEM_LIMIT),
        name="nsa_prompt",
    )(q_t, gates_t, kc, ksel, sel_t, kwin, win_t, ovt)


def _nsa_sample_kernel(pt_ref, q_ref, gate_ref, kc_ref, pool_ref, snew_ref, wcache_ref, wnew_ref, ov_ref, ex_ref,
                       o_ref, buf, sem, sel_scr, oc_scr, m_scr, l_scr, acc_scr, *, n_pages, past):
    slot = _paged_tile(pt_ref, pool_ref, buf, sem, n_pages)
    t = pl.program_id(1)
    nt = pl.num_programs(1)
    ds = q_ref.shape[1]
    rows = HPG * ds
    n_cmp = kc_ref.shape[1]
    n_sbp = ov_ref.shape[1]
    blocks_per_tile = n_pages * PAGE_SIZE // L_SEL
    n_past_blocks = past // L_SEL
    keys = n_pages * PAGE_SIZE
    q = q_ref[0]
    tok4 = lax.broadcasted_iota(jnp.int32, (rows, 1), 0) % ds

    @pl.when(t == 0)
    def _():
        tpos = past + lax.broadcasted_iota(jnp.int32, (ds, 1), 0)
        for g in range(NSA_GROUPS):
            qs = _stack_heads(q, g)
            ridx = lax.broadcasted_iota(jnp.int32, (rows, n_cmp), 1)
            valid = (ridx >= 1) & (D_CMP * ridx + (L_CMP - D_CMP - 1) <= past + tok4)
            o_c, psum = _compressed_branch(qs, kc_ref[0, :, g * HD:(g + 1) * HD],
                                           kc_ref[0, :, KV_W + g * HD:KV_W + (g + 1) * HD], valid, ds)
            oc_scr[g] = o_c
            sel = _select_blocks(_hdot(psum, ov_ref[...]), tpos, n_past_blocks + 1)
            sel4 = jnp.concatenate([sel] * HPG, axis=0)
            for tt in range(sel_scr.shape[0]):
                sel_scr[tt, g] = sel4[:, tt * LANES:(tt + 1) * LANES]
            m_scr[g], l_scr[g], acc_scr[g] = _online_init(rows)

    kv_t = buf[slot]
    first_blk = t * blocks_per_tile
    ex = ex_ref[(first_blk % LANES) // blocks_per_tile]
    for g in range(NSA_GROUPS):
        qsb = _stack_heads(q, g).astype(BF16)
        bias = ((sel_scr[first_blk // LANES, g] - 1.0) * (-NEG_INF)).astype(BF16)
        qa = jnp.concatenate([qsb, bias], axis=1)
        ka = jnp.concatenate([kv_t[g * HD:(g + 1) * HD].astype(BF16), ex], axis=0)
        vt = kv_t[KV_W + g * HD:KV_W + (g + 1) * HD].astype(BF16)
        m_scr[g], l_scr[g], acc_scr[g] = _online((m_scr[g], l_scr[g], acc_scr[g]), _dg(qa, ka), None, vt, NT)

    @pl.when(t == nt - 1)
    def _():
        gates = gate_ref[0]
        knew_pos = lax.broadcasted_iota(jnp.int32, (rows, ds), 1)
        snew = snew_ref[0]
        wnew = wnew_ref[0]
        wc_t = wcache_ref[0]
        n_wc = wc_t.shape[1]
        for g in range(NSA_GROUPS):
            qsb = _stack_heads(q, g).astype(BF16)
            kcol = slice(g * HD, (g + 1) * HD)
            vcol = slice(KV_W + g * HD, KV_W + (g + 1) * HD)
            sel_new = sel_scr[n_past_blocks // LANES, g][:, n_past_blocks % LANES:n_past_blocks % LANES + 1]
            st = _online((m_scr[g], l_scr[g], acc_scr[g]), _dg(qsb, snew[:, kcol].astype(BF16), NT),
                         (sel_new > 0.5) & (knew_pos <= tok4), snew[:, vcol].astype(BF16))
            o_s = _online_out(st)
            dist_c = (n_wc + tok4) - lax.broadcasted_iota(jnp.int32, (rows, n_wc), 1)
            st = _online(_online_init(rows), _dg(qsb, wc_t[kcol].astype(BF16)),
                         (dist_c >= 0) & (dist_c < WIN), wc_t[vcol].astype(BF16), NT)
            dist_n = tok4 - knew_pos
            st = _online(st, _dg(qsb, wnew[:, kcol].astype(BF16), NT), (dist_n >= 0) & (dist_n < WIN),
                         wnew[:, vcol].astype(BF16))
            o_w = _online_out(st)
            o_c = oc_scr[g]
            for h in range(HPG):
                hd = HPG * g + h
                r = slice(h * ds, (h + 1) * ds)
                o_ref[0, :, hd * HD:(hd + 1) * HD] = (gates[:, 3 * hd:3 * hd + 1] * o_c[r]
                                                      + gates[:, 3 * hd + 1:3 * hd + 2] * o_s[r]
                                                      + gates[:, 3 * hd + 2:3 * hd + 3] * o_w[r])


def _nsa_sample(page_table, q, gates, kc, pool, sel_new, win_cache, win_new, n_pages):
    nb, ds = q.shape[:2]
    pages_per_seq = page_table.shape[1]
    past = pages_per_seq * PAGE_SIZE
    nt = pages_per_seq // n_pages
    n_cmp = kc.shape[1]
    n_sb = past // L_SEL + 1
    n_sbp = -(-n_sb // LANES) * LANES
    blocks_per_tile = n_pages * PAGE_SIZE // L_SEL
    keys = n_pages * PAGE_SIZE
    rows = HPG * ds
    ov = jnp.asarray(_overlap_matrix(n_cmp, n_sbp) * (np.arange(n_sbp) < n_sb)[None, :])
    ex = (np.arange(LANES)[None, :, None]
          == (np.arange(LANES // blocks_per_tile)[:, None, None] * blocks_per_tile + np.arange(keys)[None, None, :] // L_SEL))
    ex = jnp.asarray(ex, dtype=BF16)
    seq = lambda r, w: pl.BlockSpec((1, r, w), lambda b, t, pt: (b, 0, 0))
    cst = lambda shape: pl.BlockSpec(shape, lambda b, t, pt: (0,) * len(shape))
    return pl.pallas_call(
        functools.partial(_nsa_sample_kernel, n_pages=n_pages, past=past),
        grid_spec=pltpu.PrefetchScalarGridSpec(
            num_scalar_prefetch=1,
            grid=(nb, nt),
            in_specs=[seq(ds, W_A), seq(ds, LANES), seq(n_cmp, 2 * KV_W), pl.BlockSpec(memory_space=pl.ANY),
                      seq(ds, 2 * KV_W), seq(2 * KV_W, win_cache.shape[2]), seq(ds, 2 * KV_W),
                      cst((n_cmp, n_sbp)), cst(ex.shape)],
            out_specs=seq(ds, W_A),
            scratch_shapes=[pltpu.VMEM((2, 2 * KV_W, keys), F32),
                            pltpu.SemaphoreType.DMA((2,)),
                            pltpu.VMEM((n_sbp // LANES, NSA_GROUPS, rows, LANES), F32),
                            pltpu.VMEM((NSA_GROUPS, rows, HD), F32),
                            pltpu.VMEM((NSA_GROUPS, rows, 1), F32),
                            pltpu.VMEM((NSA_GROUPS, rows, 1), F32),
                            pltpu.VMEM((NSA_GROUPS, rows, HD), F32)]),
        out_shape=jax.ShapeDtypeStruct((nb, ds, W_A), F32),
        compiler_params=pltpu.CompilerParams(dimension_semantics=("arbitrary", "arbitrary"),
                                             vmem_limit_bytes=VMEM_LIMIT),
        name="nsa_sample",
    )(page_table, q, gates, kc, pool, sel_new, win_cache, win_new, ov, ex)


def _sum_heads(x, ones_ref):
    hi, lo = _split(x)
    return _dg(hi, ones_ref[...]) + _dg(lo, ones_ref[...])


def _rwkv_prep_kernel(rz_ref, before_ref, shift_ref, mu_ref, w0_ref, wup_ref, a0_ref, aup_ref, kk_ref, ka_ref, rk_ref,
                      gnb_ref, ones_ref, a_ref, o_ref, y1_ref, y2_ref, yadd_ref, *, chunk, t_valid):
    rows = rz_ref.shape[1]
    n_chunks = rows // chunk
    shift = int(np.log2(chunk))
    z = rz_ref[0]
    ridx = lax.broadcasted_iota(jnp.int32, (rows, 1), 0)
    n_before = before_ref.shape[1]
    row_before = jnp.where(pl.program_id(1) == 0, shift_ref[0], before_ref[0, n_before - 1:n_before])
    prev = jnp.where(ridx == 0, row_before, pltpu.roll(z, 1, 0))
    zs = z + (prev - z) * mu_ref[...]
    r = zs[:, 0:W_R]
    k = zs[:, W_R:2 * W_R]
    v = zs[:, 2 * W_R:3 * W_R]
    wd = zs[:, 3 * W_R:3 * W_R + R_W]
    ad = zs[:, 3 * W_R + R_W:C_R]
    wpre = w0_ref[...] + _hdot(jnp.tanh(wd), wup_ref[...])
    w = -(jnp.maximum(-wpre, 0.0) + jnp.log(1.0 + jnp.exp(-jnp.abs(wpre)))) - 0.5
    ld = -jnp.exp(w)
    a = _sigmoid(a0_ref[...] + _hdot(ad, aup_ref[...]))
    kkn = k * kk_ref[...]
    kk = kkn / jnp.maximum(jnp.sqrt(_sum_heads(kkn * kkn, ones_ref)), 1e-12)
    k2 = k * (1.0 + (a - 1.0) * ka_ref[...])
    yadd_ref[0] = _sum_heads(r * k2 * rk_ref[...], ones_ref) * v + gnb_ref[...]
    if t_valid < chunk:
        live = (ridx & (chunk - 1)) < t_valid
        ld = jnp.where(live, ld, 0.0)
        kk = jnp.where(live, kk, 0.0)
        k2 = jnp.where(live, k2, 0.0)
    ti = lax.broadcasted_iota(jnp.int32, (rows, rows), 0)
    tj = lax.broadcasted_iota(jnp.int32, (rows, rows), 1)
    tri = jnp.where((jnp.right_shift(ti, shift) == jnp.right_shift(tj, shift)) & (tj <= ti), 1.0, 0.0).astype(BF16)
    ld_h = ld.astype(BF16)
    ld_m = (ld - ld_h.astype(F32)).astype(BF16)
    ld_l = (ld - ld_h.astype(F32) - ld_m.astype(F32)).astype(BF16)
    cum = _dg(tri, ld_h) + (_dg(tri, ld_m) + _dg(tri, ld_l))
    cum_end = jnp.concatenate([jnp.broadcast_to(cum[(c + 1) * chunk - 1:(c + 1) * chunk], (chunk, W_R))
                               for c in range(n_chunks)], axis=0)
    p_inv = jnp.exp(-cum)
    p_end = jnp.exp(cum_end - cum)
    kka = kk * a
    at = -kk * jnp.exp(cum - ld)
    rt = r * jnp.exp(cum)
    at_b, rt_b, v_b = at.astype(BF16), rt.astype(BF16), v.astype(BF16)
    bt_b, kt_b = (kka * p_inv).astype(BF16), (k2 * p_inv).astype(BF16)
    be_b, ke_b = (kka * p_end).astype(BF16), (k2 * p_end).astype(BF16)
    dec_end = jnp.exp(cum_end)

    row = lax.broadcasted_iota(jnp.int32, (chunk, PAIR_W), 0)
    col = lax.broadcasted_iota(jnp.int32, (chunk, PAIR_W), 1) & (R_HD - 1)
    strict, incl = col < row, col <= row
    eye_pair = jnp.where(col == row, 1.0, 0.0)
    n_levels = max(1, int(np.ceil(np.log2(chunk))))
    units = [(c, p) for c in range(n_chunks) for p in range(R_HEADS // 2)]
    idx = range(len(units))
    cut = lambda x, u: x[u[0] * chunk:(u[0] + 1) * chunk, u[1] * PAIR_W:(u[1] + 1) * PAIR_W]
    mm = [_dg(jnp.concatenate([cut(at_b, u), cut(rt_b, u)], axis=0),
              jnp.concatenate([_pair_diag(cut(bt_b, u)), _pair_diag(cut(kt_b, u))], axis=0), NT) for u in units]
    low = [jnp.where(strict, m[0:chunk, 0:PAIR_W], 0.0) for m in mm]
    m_k = [jnp.concatenate([jnp.where(strict, m[0:chunk, PAIR_W:], 0.0), jnp.where(incl, m[chunk:, PAIR_W:], 0.0)],
                           axis=0).astype(BF16) for m in mm]
    m_rb = [jnp.where(incl, m[chunk:, 0:PAIR_W], 0.0).astype(BF16) for m in mm]
    mv = [_dg(m_k[i], _pair_diag(cut(v_b, u))) for i, u in enumerate(units)]
    inv = [eye_pair + x for x in low]
    pw = [x.astype(BF16) for x in low]
    for lvl in range(1, n_levels + 1):
        if lvl == 1:
            pw = [_dg(x, _pair_diag(x)).astype(BF16) for x in pw]
        elif lvl <= n_levels:
            last = lvl == n_levels
            nxt = [_dg(pw[i], _pair_diag(inv[i].astype(BF16)) if last else
                       jnp.concatenate([_pair_diag(inv[i].astype(BF16)), _pair_diag(pw[i])], axis=1)) for i in idx]
            inv = [inv[i] + nxt[i][:, 0:PAIR_W] for i in idx]
            if not last:
                pw = [nxt[i][:, PAIR_W:].astype(BF16) for i in idx]
    inv_b = [x.astype(BF16) for x in inv]
    w = [_dg(inv_b[i], jnp.concatenate([_pair_diag(cut(at_b, u)), _pair_diag(mv[i][0:chunk].astype(BF16))], axis=1))
         for i, u in enumerate(units)]
    w_b = [x.astype(BF16) for x in w]
    y12 = [_dg(m_rb[i], jnp.concatenate([_pair_diag(w_b[i][:, 0:PAIR_W]), _pair_diag(w_b[i][:, PAIR_W:])], axis=1))
           for i in idx]
    prow = lax.broadcasted_iota(jnp.int32, (PAIR_W, PAIR_W), 0)
    pcol = lax.broadcasted_iota(jnp.int32, (PAIR_W, PAIR_W), 1)
    same_head = (prow < R_HD) == (pcol < R_HD)
    col_first = lax.broadcasted_iota(jnp.int32, (R_HD, PAIR_W), 1) < R_HD
    for i, u in enumerate(units):
        c, p = u
        rs = slice(c * chunk, (c + 1) * chunk)
        ls = slice(p * PAIR_W, (p + 1) * PAIR_W)
        a_mat = jnp.where(same_head, _dg(w_b[i][:, 0:PAIR_W], cut(be_b, u), TN), 0.0)
        a_mat = a_mat + jnp.where(prow == pcol, jnp.broadcast_to(dec_end[rs, ls][0:1], (PAIR_W, PAIR_W)), 0.0)
        a_ref[0, c, p] = a_mat.astype(BF16)
        o_full = _dg(jnp.concatenate([w_b[i][:, PAIR_W:], cut(v_b, u)], axis=0),
                     jnp.concatenate([cut(be_b, u), cut(ke_b, u)], axis=0), TN)
        o_ref[0, c, p] = jnp.where(col_first, o_full[0:R_HD], o_full[R_HD:])
        y1_ref[0, rs, ls] = (y12[i][:, 0:PAIR_W] + cut(rt, u)).astype(BF16)
        y2_ref[0, rs, ls] = y12[i][:, PAIR_W:] + mv[i][chunk:]


PAIR_W = 2 * R_HD


def _pair_diag(x):
    first = jnp.where(lax.broadcasted_iota(jnp.int32, x.shape, 1) < R_HD, 1.0, 0.0).astype(x.dtype)
    return jnp.concatenate([x * first, x * (1 - first)], axis=0)


def _rwkv_scan_kernel(a_ref, o_ref, y1_ref, y2_ref, yadd_ref, s0_ref, gng_ref, ones_ref, y_ref, st, sout_ref):
    units = [(b, p) for b in range(st.shape[0]) for p in range(R_HEADS // 2)]

    @pl.when(pl.program_id(1) == 0)
    def _():
        for b, p in units:
            st[b, p] = jnp.concatenate([s0_ref[b, 2 * p], s0_ref[b, 2 * p + 1]], axis=1)

    lsl = lambda p: slice(p * PAIR_W, (p + 1) * PAIR_W)
    s_b = [st[b, p].astype(BF16) for b, p in units]
    ys = [_dg(y1_ref[b, :, lsl(p)], _pair_diag(s_b[i]), NT) + y2_ref[b, :, lsl(p)] for i, (b, p) in enumerate(units)]
    for i, (b, p) in enumerate(units):
        st[b, p] = _dg(s_b[i], a_ref[b, 0, p]) + o_ref[b, 0, p]
    mean = [_sum_heads(y, ones_ref) * (1.0 / R_HD) for y in ys]
    dev = [ys[i] - mean[i] for i in range(len(units))]
    var = [_sum_heads(d * d, ones_ref) * (1.0 / R_HD) for d in dev]
    for i, (b, p) in enumerate(units):
        y_ref[b, :, lsl(p)] = dev[i] * lax.rsqrt(var[i] + GN_EPS) * gng_ref[:, lsl(p)] + yadd_ref[b, :, lsl(p)]

    for b, p in units:
        s = st[b, p]
        sout_ref[b, 2 * p] = s[:, 0:R_HD]
        sout_ref[b, 2 * p + 1] = s[:, R_HD:]


def _rwkv(rz, shift0, state0, params, chunk, chunks_per_step, seqs_per_step, t_valid):
    mu, w0, wup, a0, aup, k_k, k_a, r_k, gn_g, gn_b = params
    b, t = rz.shape[:2]
    assert chunk == R_HD, "pair packing puts the chunk's time index on a head's 64 lanes"
    rows = chunk * chunks_per_step
    n_chunks = t // chunk
    n_pairs = R_HEADS // 2
    n_before = 8
    ones = jnp.asarray(np.kron(np.eye(R_HEADS), np.ones((R_HD, R_HD))), dtype=BF16)
    vec = lambda n: _const_spec((1, n))
    rowsd = lambda dt: jax.ShapeDtypeStruct((b, t, W_R), dt)
    row_spec = pl.BlockSpec((1, rows, W_R), lambda bi, c: (bi, c, 0))
    a_m, o_m, y1, y2, yadd = pl.pallas_call(
        functools.partial(_rwkv_prep_kernel, chunk=chunk, t_valid=t_valid),
        grid=(b, t // rows),
        in_specs=[pl.BlockSpec((1, rows, C_R), lambda bi, c: (bi, c, 0)),
                  pl.BlockSpec((1, n_before, C_R),
                               lambda bi, c: (bi, jnp.maximum(c * (rows // n_before) - 1, 0), 0)),
                  pl.BlockSpec((1, 1, C_R), lambda bi, c: (bi, 0, 0)),
                  vec(C_R), vec(W_R), _const_spec((R_W, W_R)), vec(W_R), _const_spec((R_A, W_R)),
                  vec(W_R), vec(W_R), vec(W_R), vec(W_R), _const_spec((W_R, W_R))],
        out_specs=[pl.BlockSpec((1, chunks_per_step, n_pairs, PAIR_W, PAIR_W), lambda bi, c: (bi, c, 0, 0, 0)),
                   pl.BlockSpec((1, chunks_per_step, n_pairs, R_HD, PAIR_W), lambda bi, c: (bi, c, 0, 0, 0)),
                   row_spec, row_spec, row_spec],
        out_shape=[jax.ShapeDtypeStruct((b, n_chunks, n_pairs, PAIR_W, PAIR_W), BF16),
                   jax.ShapeDtypeStruct((b, n_chunks, n_pairs, R_HD, PAIR_W), F32),
                   rowsd(BF16), rowsd(F32), rowsd(F32)],
        compiler_params=pltpu.CompilerParams(dimension_semantics=("parallel", "parallel"),
                                             vmem_limit_bytes=VMEM_LIMIT),
        name="rwkv_prep",
    )(rz, rz, shift0.reshape(b, 1, C_R), mu, w0, wup, a0, aup, k_k, k_a, r_k, gn_b, ones)
    nb = seqs_per_step
    row1 = pl.BlockSpec((nb, chunk, W_R), lambda bi, c: (bi, c, 0))
    pair_state = pl.BlockSpec((nb, n_pairs, R_HD, PAIR_W), lambda bi, c: (bi, 0, 0, 0))
    head_state = pl.BlockSpec((nb, R_HEADS, R_HD, R_HD), lambda bi, c: (bi, 0, 0, 0))
    y, _, state = pl.pallas_call(
        _rwkv_scan_kernel,
        grid=(b // nb, n_chunks),
        in_specs=[pl.BlockSpec((nb, 1, n_pairs, PAIR_W, PAIR_W), lambda bi, c: (bi, c, 0, 0, 0)),
                  pl.BlockSpec((nb, 1, n_pairs, R_HD, PAIR_W), lambda bi, c: (bi, c, 0, 0, 0)),
                  row1, row1, row1, head_state, vec(W_R), _const_spec((PAIR_W, PAIR_W))],
        out_specs=[row1, pair_state, head_state],
        out_shape=[rowsd(F32), jax.ShapeDtypeStruct((b, n_pairs, R_HD, PAIR_W), F32),
                   jax.ShapeDtypeStruct((b, R_HEADS, R_HD, R_HD), F32)],
        compiler_params=pltpu.CompilerParams(dimension_semantics=("parallel", "arbitrary"),
                                             vmem_limit_bytes=VMEM_LIMIT),
        name="rwkv_scan",
    )(a_m, o_m, y1, y2, yadd, state0, gn_g, ones[:PAIR_W, :PAIR_W])
    return y, state


def _rope_tables(pos):
    half = HD // 2
    inv = ROPE_THETA ** (-jnp.arange(half, dtype=F32) / half)
    ang = pos.astype(F32)[:, None] * inv[None, :]
    cos, sin = jnp.cos(ang), jnp.sin(ang)
    reps = LANES // HD
    return (jnp.tile(jnp.concatenate([cos, cos], axis=-1), (1, reps)),
            jnp.tile(jnp.concatenate([-sin, sin], axis=-1), (1, reps)))


def _pick_tile(n, cap):
    t = cap
    while n % t:
        t //= 2
    return t


def kernel(x_prompt, x_sample, p_prompt, p_sample, cache_cmp_kv, cache_sel_kv, cache_win_kv, state_wkv, state_shift, page_table, norm_g, w_in, cmp_pe, cmp_w1, cmp_b1, cmp_w2, cmp_b2, w_pa, rwkv_mu, rwkv_w0, rwkv_w_up, rwkv_a0, rwkv_a_up, rwkv_k_k, rwkv_k_a, rwkv_r_k, rwkv_gn_g, rwkv_gn_b, w_pb, w_out, ple_norm_g, w_ple_gate, w_ple_proj, final_norm_g):
    b, s_len = x_prompt.shape[:2]
    db, ds = x_sample.shape[:2]
    depth = norm_g.shape[0]
    assert depth == 1, "single-layer trunk"
    n_pages = page_table.shape[1]
    past = n_pages * PAGE_SIZE
    win_buf = cache_win_kv.shape[2]
    n_pool = cache_cmp_kv.shape[1]
    i = 0

    w = w_in[i]
    o_q, o_kv, o_ng, o_ag = 0, W_A, W_A + 6 * KV_W, W_A + 6 * KV_W + 3 * NSA_HEADS
    w_pack = jnp.concatenate([w[:, o_q:o_ng], w[:, o_ag:], w[:, o_ng:o_ag],
                              jnp.zeros((D_MODEL, C_PACK - w.shape[1]), w.dtype)], axis=1).astype(BF16)
    cw = _compress_weights(cmp_pe[i], cmp_w1[i], cmp_b1[i], cmp_w2[i], cmp_b2[i])
    row = lambda u: u.reshape(1, -1)
    rw = (row(rwkv_mu[i]), row(rwkv_w0[i]), rwkv_w_up[i], row(rwkv_a0[i]), rwkv_a_up[i], row(rwkv_k_k[i]),
          row(rwkv_k_a[i]), row(rwkv_r_k[i]), row(rwkv_gn_g[i]), row(rwkv_gn_b[i]))
    ow = (w_pa[i].astype(BF16), w_pb[i].astype(BF16), w_out[i].astype(BF16), row(ple_norm_g[i]),
          w_ple_gate[i].astype(BF16), w_ple_proj[i].astype(BF16), row(final_norm_g))

    tm = _pick_tile(s_len, 256)
    cos_p, sin_p = _rope_tables(jnp.arange(s_len, dtype=jnp.int32))
    xp2 = x_prompt.reshape(b * s_len, D_MODEL)
    q, cmp_p, sel_p, win_p, gates, sa, rz, sr, mg, ksel, kwin = _project(xp2, row(norm_g[i]), w_pack, cos_p, sin_p,
                                                                        tm, True)
    kc_p = _compress_seq(cmp_p, cw)
    seq = lambda u: u.reshape(b, s_len, u.shape[-1])
    o_a = _nsa_prompt(q, gates, kc_p, ksel, sel_p, kwin, win_p, tq=128, tk=min(512, s_len))
    chunk = 64
    o_b, st_p = _rwkv(seq(rz), jnp.zeros((b, C_R), F32), jnp.zeros((b, R_HEADS, R_HD, R_HD), F32), rw, chunk,
                      chunks_per_step=2, seqs_per_step=_pick_tile(b, 4), t_valid=chunk)
    y_p = _merge(xp2, o_a.reshape(b * s_len, W_A), o_b.reshape(b * s_len, W_R), sa, sr, mg,
                 p_prompt[i].reshape(b * s_len, D_PLE), *ow, tm=tm)
    kv_shape = lambda u, n, t: u.reshape(n, t, 2, NSA_GROUPS, HD)
    from_cm = lambda u: jnp.transpose(u.reshape(u.shape[0], 2, NSA_GROUPS, HD, u.shape[2]), (0, 4, 1, 2, 3))
    to_cm = lambda u: jnp.transpose(u, (0, 2, 3, 4, 1)).reshape(u.shape[0], 2 * KV_W, u.shape[1])
    cmp_kv_p = from_cm(cmp_p)
    sel_kv_p = from_cm(sel_p)
    win_kv_p = from_cm(win_p[:, :, s_len - min(WIN, s_len):])
    shift_p = seq(rz)[:, -1]

    n_s = db * ds
    cos_s, sin_s = _rope_tables(jnp.tile(past + jnp.arange(ds, dtype=jnp.int32), db))
    xs2 = x_sample.reshape(n_s, D_MODEL)
    q, cmp_s, sel_s, win_s, gates, sa, rz, sr, mg = _project(xs2, row(norm_g[i]), w_pack, cos_s, sin_s, n_s, False)
    kc_s = _compress_paged(page_table, to_cm(cache_cmp_kv[i]), cw, n_pages=_pick_tile(n_pages, 64))
    seqs = lambda u: u.reshape(db, ds, u.shape[-1])
    o_a = _nsa_sample(page_table, seqs(q), seqs(gates), kc_s, to_cm(cache_sel_kv[i]),
                      seqs(sel_s), to_cm(cache_win_kv[i]), seqs(win_s), n_pages=_pick_tile(n_pages, 32))
    pad_t = chunk
    rz_pad = jnp.pad(seqs(rz), ((0, 0), (0, pad_t - ds), (0, 0)))
    o_b, st_s = _rwkv(rz_pad, state_shift[i], state_wkv[i], rw, pad_t, chunks_per_step=1,
                      seqs_per_step=_pick_tile(db, 4), t_valid=ds)
    y_s = _merge(xs2, o_a.reshape(n_s, W_A), o_b[:, :ds].reshape(n_s, W_R), sa, sr, mg,
                 p_sample[i].reshape(n_s, D_PLE), *ow, tm=n_s)
    win_kv_s = jnp.concatenate([cache_win_kv[i], kv_shape(win_s, db, ds)], axis=1)[:, ds:]
    shift_s = seqs(rz)[:, -1]

    return (y_p.reshape(b, s_len, D_MODEL), y_s.reshape(db, ds, D_MODEL),
            cmp_kv_p[None], kv_shape(cmp_s, db, ds)[None], sel_kv_p[None], kv_shape(sel_s, db, ds)[None],
            win_kv_p[None], win_kv_s[None], st_p[None], st_s[None], shift_p[None], shift_s[None])
```

```python
import functools

import numpy as np
import jax
import jax.numpy as jnp
from jax import lax
from jax.experimental import pallas as pl
from jax.experimental.pallas import tpu as pltpu

F32 = jnp.float32
BF16 = jnp.bfloat16

D_MODEL = 1024
D_PLE = 256
NSA_HEADS = 8
NSA_GROUPS = 2
HD = 64
HPG = NSA_HEADS // NSA_GROUPS
W_A = NSA_HEADS * HD
KV_W = NSA_GROUPS * HD
L_CMP = 32
D_CMP = 16
CMP_HID = 128
L_SEL = 64
N_SEL = 16
WIN = 512
FORCE_SCORE = 1e4
NEG_INF = -1e30
R_HEADS = 8
R_HD = 64
W_R = R_HEADS * R_HD
R_W = 64
R_A = 64
C_R = 3 * W_R + R_W + R_A
GN_EPS = 64e-5
ROPE_THETA = 10000.0
NORM_EPS = 1e-6
PAGE_SIZE = 128

LANES = 128
CHUNKS_PER_PAGE = PAGE_SIZE // D_CMP
CHUNK_W = D_CMP * 2 * KV_W
VMEM_LIMIT = 56 * 1024 * 1024

C_Q = 0
C_KV = 512
C_AG = 1280
C_RZ = 1792
C_RG = 3456
C_MG = 3968
C_NG = 6016
C_PACK = 6144

NN = ((1,), (0,))
NT = ((1,), (1,))
TN = ((0,), (0,))


def _dg(a, b, dims=NN):
    return lax.dot_general(a, b, (dims, ((), ())), preferred_element_type=F32)


def _split(x):
    hi = x.astype(BF16)
    lo = (x - hi.astype(F32)).astype(BF16)
    return hi, lo


def _hdot_s(a, b, dims=NN):
    (ah, al), (bh, bl) = a, b
    return _dg(ah, bh, dims) + (_dg(ah, bl, dims) + _dg(al, bh, dims))


def _hdot(a, b, dims=NN):
    return _hdot_s(_split(a), _split(b), dims)


def _sigmoid(x):
    return 1.0 / (1.0 + jnp.exp(-x))


def _silu(x):
    return x * _sigmoid(x)


def _rms(x, g):
    return x * lax.rsqrt(jnp.mean(x * x, axis=-1, keepdims=True) + NORM_EPS) * g


def _const_spec(shape):
    zeros = (0,) * len(shape)
    return pl.BlockSpec(shape, lambda *_: zeros)


def _rope128(z, cos, sin, first):
    partner = jnp.where(first, pltpu.roll(z, LANES - HD // 2, 1), pltpu.roll(z, HD // 2, 1))
    return z * cos + partner * sin


def _proj_kernel(x_ref, g_ref, w_ref, cos_ref, sin_ref,
                 q_ref, cmp_ref, sel_ref, win_ref, gate_ref, sa_ref, rz_ref, sr_ref, mg_ref, *aug_refs,
                 seq_major, n_pos_blocks):
    hb = _rms(x_ref[...], g_ref[...]).astype(BF16)
    cos = cos_ref[...]
    sin = sin_ref[...]
    tm = cos.shape[0]
    lane = lax.broadcasted_iota(jnp.int32, cos.shape, 1)
    first = (lane % HD) < (HD // 2)

    def mm(a, b):
        return jnp.dot(hb, w_ref[:, a:b], preferred_element_type=F32)

    zq = mm(C_Q, C_Q + W_A)
    q = jnp.concatenate([_rope128(zq[:, c * LANES:(c + 1) * LANES], cos, sin, first) for c in range(W_A // LANES)],
                        axis=1)
    gates = _sigmoid(mm(C_NG, C_PACK))
    if seq_major:
        q_ref[0] = q.T
        gate_ref[0] = gates.T
    else:
        q_ref[...] = q
        gate_ref[...] = gates
    for i, ref in enumerate((cmp_ref, sel_ref, win_ref)):
        zkv = mm(C_KV + 2 * KV_W * i, C_KV + 2 * KV_W * (i + 1))
        k_rot = _rope128(zkv[:, 0:KV_W], cos, sin, first)
        if seq_major:
            ref[0] = jnp.concatenate([k_rot, zkv[:, KV_W:2 * KV_W]], axis=1).T
            if i > 0:
                aug_ref = aug_refs[i - 1]
                if i == 1:
                    t_in_seq = ((pl.program_id(0) % n_pos_blocks) * tm
                                + lax.broadcasted_iota(jnp.int32, cos.shape, 0))
                    tail = jnp.where(jnp.right_shift(t_in_seq, int(np.log2(L_SEL))) == lane - HD, 1.0, 0.0)
                else:
                    tail = jnp.zeros(cos.shape, F32)
                for g in range(NSA_GROUPS):
                    k_g = k_rot if g == 0 else pltpu.roll(k_rot, LANES - g * HD, 1)
                    aug_ref[0, g] = jnp.where(lane < HD, k_g, tail).astype(BF16)
        else:
            ref[:, 0:KV_W] = k_rot
            ref[:, KV_W:2 * KV_W] = zkv[:, KV_W:2 * KV_W]
    sa_ref[...] = _silu(mm(C_AG, C_AG + W_A))
    rz_ref[...] = mm(C_RZ, C_RZ + C_R)
    sr_ref[...] = _silu(mm(C_RG, C_RG + W_R))
    mg_ref[...] = _sigmoid(mm(C_MG, C_MG + 2 * D_MODEL))


def _project(x2d, norm_g, w_pack, cos_t, sin_t, tm, seq_major):
    n = x2d.shape[0]
    seq_len = cos_t.shape[0]
    n_pos_blocks = seq_len // tm
    n_seq = n // seq_len
    widths = (W_A, 2 * KV_W, 2 * KV_W, 2 * KV_W, LANES, W_A, C_R, W_R, 2 * D_MODEL)
    row = lambda w: pl.BlockSpec((tm, w), lambda i: (i, 0))
    pos = pl.BlockSpec((tm, LANES), lambda i: (i % n_pos_blocks, 0))
    out_specs = [row(w) for w in widths]
    out_shape = [jax.ShapeDtypeStruct((n, w), F32) for w in widths]
    in_specs = [row(D_MODEL), _const_spec((1, D_MODEL)), _const_spec((D_MODEL, C_PACK)), pos, pos]
    args = [x2d, norm_g, w_pack, cos_t, sin_t]
    if seq_major:
        assert seq_len // L_SEL <= LANES - HD, "selection-block one-hot must fit beside the key"
        for o in range(5):
            out_specs[o] = pl.BlockSpec((1, widths[o], tm), lambda i: (i // n_pos_blocks, 0, i % n_pos_blocks))
            out_shape[o] = jax.ShapeDtypeStruct((n_seq, widths[o], seq_len), F32)
        for _ in range(2):
            out_specs.append(pl.BlockSpec((1, NSA_GROUPS, tm, LANES),
                                          lambda i: (i // n_pos_blocks, 0, i % n_pos_blocks, 0)))
            out_shape.append(jax.ShapeDtypeStruct((n_seq, NSA_GROUPS, seq_len, LANES), BF16))
    return pl.pallas_call(
        functools.partial(_proj_kernel, seq_major=seq_major, n_pos_blocks=n_pos_blocks),
        grid=(n // tm,),
        in_specs=in_specs,
        out_specs=out_specs,
        out_shape=out_shape,
        compiler_params=pltpu.CompilerParams(dimension_semantics=("parallel",), vmem_limit_bytes=VMEM_LIMIT),
        name="project",
    )(*args)


def _merge_kernel(x_ref, oa_ref, ob_ref, sa_ref, sr_ref, mg_ref, p_ref,
                  wpa_ref, wpb_ref, wout_ref, pg_ref, wgate_ref, wproj_ref, fg_ref, y_ref):
    bdot = lambda a, w_ref: jnp.dot(a.astype(BF16), w_ref[...], preferred_element_type=F32)
    ya = bdot(oa_ref[...] * sa_ref[...], wpa_ref)
    yb = bdot(ob_ref[...] * sr_ref[...], wpb_ref)
    m = mg_ref[:, 0:D_MODEL] * ya + mg_ref[:, D_MODEL:2 * D_MODEL] * yb
    x2 = x_ref[...] + bdot(m, wout_ref)
    gate = _sigmoid(bdot(_rms(x2, pg_ref[...]), wgate_ref))
    x3 = x2 + gate * bdot(p_ref[...], wproj_ref)
    y_ref[...] = _rms(x3, fg_ref[...])


def _merge(x2d, oa, ob, sa, sr, mg, p2d, wpa, wpb, wout, pg, wgate, wproj, fg, tm):
    n = x2d.shape[0]
    row = lambda w: pl.BlockSpec((tm, w), lambda i: (i, 0))
    return pl.pallas_call(
        _merge_kernel,
        grid=(n // tm,),
        in_specs=[row(D_MODEL), row(W_A), row(W_R), row(W_A), row(W_R), row(2 * D_MODEL), row(D_PLE),
                  _const_spec((W_A, D_MODEL)), _const_spec((W_R, D_MODEL)), _const_spec((D_MODEL, D_MODEL)),
                  _const_spec((1, D_MODEL)), _const_spec((D_MODEL, D_MODEL)), _const_spec((D_PLE, D_MODEL)),
                  _const_spec((1, D_MODEL))],
        out_specs=row(D_MODEL),
        out_shape=jax.ShapeDtypeStruct((n, D_MODEL), F32),
        compiler_params=pltpu.CompilerParams(dimension_semantics=("parallel",), vmem_limit_bytes=VMEM_LIMIT),
        name="merge",
    )(x2d, oa, ob, sa, sr, mg, p2d, wpa, wpb, wout, pg, wgate, wproj, fg)


def _page_copies(pt_ref, pool_ref, buf, sem, b, t, slot, n_pages):
    return [pltpu.make_async_copy(pool_ref.at[pt_ref[b, t * n_pages + p]],
                                  buf.at[slot, :, pl.ds(p * PAGE_SIZE, PAGE_SIZE)], sem.at[slot])
            for p in range(n_pages)]


def _paged_tile(pt_ref, pool_ref, buf, sem, n_pages):
    b, t = pl.program_id(0), pl.program_id(1)
    nt = pl.num_programs(1)
    step = b * nt + t
    slot = step % 2

    @pl.when(step == 0)
    def _():
        for cp in _page_copies(pt_ref, pool_ref, buf, sem, b, t, slot, n_pages):
            cp.start()

    nxt = step + 1

    @pl.when(nxt < pl.num_programs(0) * nt)
    def _():
        for cp in _page_copies(pt_ref, pool_ref, buf, sem, nxt // nt, nxt % nt, 1 - slot, n_pages):
            cp.start()

    for cp in _page_copies(pt_ref, pool_ref, buf, sem, b, t, slot, n_pages):
        cp.wait()
    return slot


CMP_HALF = NSA_GROUPS * CMP_HID


def _compress_core(page_t, xs, perm_ref, w1_ref, pe_ref, b1_ref, w2_ref, b2_ref, carry, out_ref):
    n_pages = xs.shape[0]
    rows = n_pages * CHUNKS_PER_PAGE
    for p in range(n_pages):
        xp = _dg(perm_ref[...], page_t(p).astype(BF16), NT)
        xs[p] = xp.reshape(D_CMP, CHUNKS_PER_PAGE, 2 * KV_W)
    ridx = lax.broadcasted_iota(jnp.int32, (rows, CMP_HALF), 0)
    for kv in range(2):
        cs = slice(kv * KV_W, (kv + 1) * KV_W)
        x = jnp.concatenate([xs[:, j, :, cs].reshape(rows, KV_W) for j in range(D_CMP)], axis=1)
        x = jnp.concatenate([x, pe_ref[kv]], axis=0).astype(BF16)
        z = jnp.dot(x, w1_ref[kv], preferred_element_type=F32)
        pos = z[rows:rows + 1, 0:CMP_HALF] + z[rows + 1:rows + 2, CMP_HALF:]
        lo = z[0:rows, 0:CMP_HALF]
        lo_prev = jnp.where(ridx == 0, carry[kv], pltpu.roll(lo, 1, 0))
        carry[kv] = lo[rows - 1:rows]
        hid = lo_prev + z[0:rows, CMP_HALF:] + pos + b1_ref[kv]
        out_ref[0, :, cs] = jnp.dot(_silu(hid).astype(BF16), w2_ref[kv], preferred_element_type=F32) + b2_ref[kv]


def _compress_paged_kernel(pt_ref, pool_ref, perm_ref, w1_ref, pe_ref, b1_ref, w2_ref, b2_ref, out_ref,
                           buf, sem, xs, carry, *, n_pages):
    slot = _paged_tile(pt_ref, pool_ref, buf, sem, n_pages)

    @pl.when(pl.program_id(1) == 0)
    def _():
        carry[...] = jnp.zeros_like(carry)

    _compress_core(lambda p: buf[slot, :, p * PAGE_SIZE:(p + 1) * PAGE_SIZE], xs, perm_ref, w1_ref, pe_ref, b1_ref,
                   w2_ref, b2_ref, carry, out_ref)


def _compress_seq_kernel(x_ref, perm_ref, w1_ref, pe_ref, b1_ref, w2_ref, b2_ref, out_ref, xs, carry):
    carry[...] = jnp.zeros_like(carry)
    _compress_core(lambda p: x_ref[0, :, p * PAGE_SIZE:(p + 1) * PAGE_SIZE], xs, perm_ref, w1_ref, pe_ref, b1_ref,
                   w2_ref, b2_ref, carry, out_ref)


def _compress_specs(cst):
    return [cst((PAGE_SIZE, PAGE_SIZE)), cst((2, D_CMP * KV_W, 2 * CMP_HALF)), cst((2, 8, D_CMP * KV_W)),
            cst((2, 1, CMP_HALF)), cst((2, CMP_HALF, KV_W)), cst((2, 1, KV_W))]


def _compress_scratch(n_pages):
    return [pltpu.VMEM((n_pages, D_CMP, CHUNKS_PER_PAGE, 2 * KV_W), F32), pltpu.VMEM((2, 1, CMP_HALF), F32)]


def _compress_paged(page_table, pool, cw, n_pages):
    nb, pages_per_seq = page_table.shape
    nt = pages_per_seq // n_pages
    toks = n_pages * PAGE_SIZE
    rows = toks // D_CMP
    cst = lambda shape: pl.BlockSpec(shape, lambda b, t, pt: (0,) * len(shape))
    return pl.pallas_call(
        functools.partial(_compress_paged_kernel, n_pages=n_pages),
        grid_spec=pltpu.PrefetchScalarGridSpec(
            num_scalar_prefetch=1,
            grid=(nb, nt),
            in_specs=[pl.BlockSpec(memory_space=pl.ANY)] + _compress_specs(cst),
            out_specs=pl.BlockSpec((1, rows, 2 * KV_W), lambda b, t, pt: (b, t, 0)),
            scratch_shapes=[pltpu.VMEM((2, 2 * KV_W, toks), F32), pltpu.SemaphoreType.DMA((2,))]
            + _compress_scratch(n_pages)),
        out_shape=jax.ShapeDtypeStruct((nb, nt * rows, 2 * KV_W), F32),
        compiler_params=pltpu.CompilerParams(dimension_semantics=("arbitrary", "arbitrary"),
                                             vmem_limit_bytes=VMEM_LIMIT),
        name="compress_paged",
    )(page_table, pool, *cw)


def _compress_seq(x_t, cw):
    nb, _, toks = x_t.shape
    rows = toks // D_CMP
    return pl.pallas_call(
        _compress_seq_kernel,
        grid=(nb,),
        in_specs=[pl.BlockSpec((1, 2 * KV_W, toks), lambda b: (b, 0, 0))] + _compress_specs(_const_spec),
        out_specs=pl.BlockSpec((1, rows, 2 * KV_W), lambda b: (b, 0, 0)),
        out_shape=jax.ShapeDtypeStruct((nb, rows, 2 * KV_W), F32),
        scratch_shapes=_compress_scratch(toks // PAGE_SIZE),
        compiler_params=pltpu.CompilerParams(dimension_semantics=("parallel",), vmem_limit_bytes=VMEM_LIMIT),
        name="compress_seq",
    )(x_t, *cw)


def _compress_weights(cmp_pe, cmp_w1, cmp_b1, cmp_w2, cmp_b2):
    eye_g = jnp.eye(NSA_GROUPS, dtype=F32)
    w1 = cmp_w1.reshape(2, 2, D_CMP, HD, CMP_HID)
    w1j = jnp.einsum('kljdf,gG->kjgdlGf', w1, eye_g).reshape(2, D_CMP * KV_W, 2 * CMP_HALF)
    pe = cmp_pe.reshape(2, 2, D_CMP, HD)
    pe = jnp.broadcast_to(pe[:, :, :, None, :], (2, 2, D_CMP, NSA_GROUPS, HD))
    pej = jnp.pad(pe.reshape(2, 2, D_CMP * KV_W), ((0, 0), (0, 6), (0, 0)))
    b1big = jnp.broadcast_to(cmp_b1[:, None, :], (2, NSA_GROUPS, CMP_HID)).reshape(2, 1, CMP_HALF)
    w2big = jnp.einsum('kfd,gG->kgfGd', cmp_w2, eye_g).reshape(2, CMP_HALF, KV_W)
    b2big = jnp.broadcast_to(cmp_b2[:, None, :], (2, NSA_GROUPS, HD)).reshape(2, 1, KV_W)
    dst = np.arange(PAGE_SIZE)
    perm = (dst[:, None] % CHUNKS_PER_PAGE) * D_CMP + dst[:, None] // CHUNKS_PER_PAGE == dst[None, :]
    return jnp.asarray(perm, dtype=BF16), w1j.astype(BF16), pej, b1big, w2big.astype(BF16), b2big


def _overlap_matrix(n_rows, n_cols):
    c_start = (np.arange(n_rows) - 1) * D_CMP
    s_start = np.arange(n_cols) * L_SEL
    ov = (c_start[:, None] <= s_start[None, :] + L_SEL - 1) & (c_start[:, None] + L_CMP - 1 >= s_start[None, :])
    ov &= (np.arange(n_rows) >= 1)[:, None]
    return ov.astype(np.float32)


def _online(state, s, msk, v_bf, v_dims=NN):
    m, l, acc = state
    if msk is not None:
        s = jnp.where(msk, s, NEG_INF)
    m_new = jnp.maximum(m, jnp.max(s, axis=-1, keepdims=True))
    alpha = jnp.exp(m - m_new)
    p = jnp.exp(s - m_new)
    if msk is not None:
        p = jnp.where(msk, p, 0.0)
    l = alpha * l + jnp.sum(p, axis=-1, keepdims=True)
    acc = alpha * acc + _dg(p.astype(BF16), v_bf, v_dims)
    return m_new, l, acc


def _online_init(rows):
    return jnp.full((rows, 1), NEG_INF, F32), jnp.zeros((rows, 1), F32), jnp.zeros((rows, HD), F32)


def _online_out(state):
    _, l, acc = state
    return acc / jnp.maximum(l, 1e-30)


def _stack_heads(q, g):
    return jnp.concatenate([q[:, (HPG * g + h) * HD:(HPG * g + h + 1) * HD] for h in range(HPG)], axis=0) * (HD ** -0.5)


def _compressed_branch(qs, kc, vc, valid, n_tok):
    s = jnp.where(valid, _hdot(qs, kc, NT), NEG_INF)
    m = jnp.max(s, axis=-1, keepdims=True)
    p = jnp.where(valid, jnp.exp(s - m), 0.0)
    p = p / jnp.maximum(jnp.sum(p, axis=-1, keepdims=True), 1e-30)
    o = jnp.dot(p.astype(BF16), vc.astype(BF16), preferred_element_type=F32)
    psum = p[0:n_tok]
    for h in range(1, HPG):
        psum = psum + p[h * n_tok:(h + 1) * n_tok]
    return o, psum


def _select_blocks(imp, tpos, n_real):
    blk = lax.broadcasted_iota(jnp.int32, imp.shape, 1)
    cur = tpos // L_SEL
    forced = (blk == 0) | (blk == cur) | (blk == cur - 1)
    imp = jnp.where(forced, FORCE_SCORE, imp)
    imp = jnp.where(blk * L_SEL > tpos, -1.0, imp)
    imp = jnp.where(blk >= n_real, -2.0, imp)
    slabs = [imp[:, l0:l0 + LANES] for l0 in range(0, imp.shape[1], LANES)]
    cnt = [jnp.zeros(s.shape, F32) for s in slabs]
    lane = lax.broadcasted_iota(jnp.int32, slabs[0].shape, 1)
    for mcol in range(n_real):
        col = jnp.broadcast_to(imp[:, mcol:mcol + 1], slabs[0].shape)
        for si, slab in enumerate(slabs):
            ge = jnp.where(col >= slab, 1.0, 0.0)
            gt = jnp.where(col > slab, 1.0, 0.0)
            if si * LANES > mcol:
                ahead = ge
            elif si * LANES + LANES - 1 < mcol:
                ahead = gt
            else:
                ahead = jnp.where(lane > mcol - si * LANES, ge, gt)
            cnt[si] = cnt[si] + ahead
    return jnp.where(jnp.concatenate(cnt, axis=1) < N_SEL, 1.0, 0.0)


def _select_blocks_t(imp_t, tpos_row, n_real):
    sub = 8
    blk = lax.broadcasted_iota(jnp.int32, imp_t.shape, 0)
    cur = jnp.right_shift(tpos_row, int(np.log2(L_SEL)))
    forced = (blk == 0) | (blk == cur) | (blk == cur - 1)
    imp_t = jnp.where(forced, FORCE_SCORE, imp_t)
    imp_t = jnp.where(blk * L_SEL > tpos_row, -1.0, imp_t)
    slabs = [imp_t[s0:s0 + sub] for s0 in range(0, n_real, sub)]
    cnt = [jnp.zeros(s.shape, F32) for s in slabs]
    sub_iota = lax.broadcasted_iota(jnp.int32, slabs[0].shape, 0)
    for m in range(n_real):
        row = jnp.broadcast_to(imp_t[m:m + 1], slabs[0].shape)
        for si, slab in enumerate(slabs):
            ge = jnp.where(row >= slab, 1.0, 0.0)
            gt = jnp.where(row > slab, 1.0, 0.0)
            if si * sub > m:
                ahead = ge
            elif si * sub + sub - 1 < m:
                ahead = gt
            else:
                ahead = jnp.where(sub_iota > m - si * sub, ge, gt)
            cnt[si] = cnt[si] + ahead
    return jnp.where(jnp.concatenate(cnt, axis=0) < N_SEL, 1.0, 0.0)


ACC_ROWS = HD + 16


def _with_ones(v_t):
    return jnp.concatenate([v_t, jnp.ones((ACC_ROWS - HD, v_t.shape[1]), BF16)], axis=0)


def _sweep_t(units, states):
    states = list(states)
    s_t = {0: units[0][1]()}
    pending = None

    def finish(k, alpha, p, values):
        m, acc = states[k]
        states[k] = (m, alpha * acc + _dg(values(), p))

    for u, (k, _, msk, values) in enumerate(units):
        if u + 1 < len(units):
            s_t[u + 1] = units[u + 1][1]()
        m, acc = states[k]
        s = s_t.pop(u)
        if msk is not None:
            s = jnp.where(msk, s, NEG_INF)
        m_new = jnp.maximum(m, jnp.max(s, axis=0, keepdims=True))
        alpha = jnp.exp(m - m_new)
        p = jnp.exp(s - m_new)
        states[k] = (m_new, acc)
        if pending is not None:
            finish(*pending)
        pending = (k, alpha, p.astype(BF16), values)
    finish(*pending)
    return tuple(states)


def _sweep_t_init(cols):
    return jnp.full((1, cols), NEG_INF, F32), jnp.zeros((ACC_ROWS, cols), F32)


def _sweep_t_out(state):
    _, acc = state
    return acc[0:HD] / jnp.maximum(acc[HD:HD + 1], 1e-30)


def _nsa_prompt_kernel(qt_ref, gatet_ref, kc_ref, ksel_ref, vsel_ref, kwin_ref, vwin_ref, ovt_ref, o_ref, *, tq, tk):
    i = pl.program_id(1)
    n_cmp = kc_ref.shape[1]
    n_sb = ovt_ref.shape[0]
    cols = HPG * tq
    aug = LANES - HD
    t0 = i * tq
    tpos_row = t0 + lax.broadcasted_iota(jnp.int32, (1, tq), 1)
    tpos4 = jnp.concatenate([tpos_row] * HPG, axis=1)
    n_full = t0 // tk
    w_start = pl.multiple_of(jnp.maximum(t0 - WIN, 0), tq)
    gates_t = gatet_ref[0]
    groups = range(NSA_GROUPS)
    n_w = WIN + tq
    dist = tpos4 - (w_start + lax.broadcasted_iota(jnp.int32, (n_w, 1), 0))
    ridx = lax.broadcasted_iota(jnp.int32, (n_cmp, cols), 0)
    valid = (ridx >= 1) & (D_CMP * ridx + (L_CMP - D_CMP - 1) <= tpos4)
    q_t = [jnp.concatenate([qt_ref[0, (HPG * g + h) * HD:(HPG * g + h + 1) * HD, :] for h in range(HPG)], axis=1)
           * (HD ** -0.5) for g in groups]
    s_c = [jnp.where(valid, _hdot(kc_ref[0, :, g * HD:(g + 1) * HD], q_t[g]), NEG_INF) for g in groups]
    p_c = [jnp.where(valid, jnp.exp(s - jnp.max(s, axis=0, keepdims=True)), 0.0) for s in s_c]
    p_c = [p / jnp.maximum(jnp.sum(p, axis=0, keepdims=True), 1e-30) for p in p_c]
    o_c = [_dg(kc_ref[0, :, KV_W + g * HD:KV_W + (g + 1) * HD].astype(BF16), p_c[g].astype(BF16), TN) for g in groups]
    psum = jnp.concatenate([sum(p[:, h * tq:(h + 1) * tq] for h in range(HPG)) for p in p_c], axis=1)
    sel_t = _select_blocks_t(_hdot(ovt_ref[...], psum), jnp.concatenate([tpos_row] * NSA_GROUPS, axis=1), n_sb)
    qa = []
    for g in groups:
        bias = jnp.concatenate([(sel_t[:, g * tq:(g + 1) * tq] - 1.0) * (-NEG_INF)] * HPG, axis=1)
        if n_sb < aug:
            bias = jnp.concatenate([bias, jnp.zeros((aug - n_sb, cols), F32)], axis=0)
        qa.append(jnp.concatenate([q_t[g], bias], axis=0).astype(BF16))

    def sel_unit(g, j, causal):
        k0 = pl.multiple_of(j * tk, tk)
        kpos = k0 + lax.broadcasted_iota(jnp.int32, (tk, 1), 0)
        return (g, lambda: _dg(ksel_ref[0, g, pl.ds(k0, tk), :], qa[g]), (kpos <= tpos4) if causal else None,
                lambda: _with_ones(vsel_ref[0, g * HD:(g + 1) * HD, pl.ds(k0, tk)].astype(BF16)))

    def win_unit(g):
        return (NSA_GROUPS + g, lambda: _dg(kwin_ref[0, g, pl.ds(w_start, n_w), :], qa[g]),
                (dist >= 0) & (dist < WIN),
                lambda: _with_ones(vwin_ref[0, g * HD:(g + 1) * HD, pl.ds(w_start, n_w)].astype(BF16)))

    states = lax.fori_loop(
        0, n_full // 2,
        lambda jj, st: _sweep_t([sel_unit(g, 2 * jj + d, False) for d in range(2) for g in groups], st),
        tuple(_sweep_t_init(cols) for g in groups))
    last = lambda: [u for g in groups for u in (sel_unit(g, n_full, True), win_unit(g))]
    states = lax.cond(n_full % 2 == 1,
                      lambda st: _sweep_t([sel_unit(g, n_full - 1, False) for g in groups] + last(), st),
                      lambda st: _sweep_t(last(), st),
                      states + tuple(_sweep_t_init(cols) for g in groups))
    heads_out = []
    for g in groups:
        o_s = _sweep_t_out(states[g])
        o_w = _sweep_t_out(states[NSA_GROUPS + g])
        for h in range(HPG):
            hd = HPG * g + h
            c = slice(h * tq, (h + 1) * tq)
            heads_out.append(gates_t[3 * hd:3 * hd + 1] * o_c[g][:, c] + gates_t[3 * hd + 1:3 * hd + 2] * o_s[:, c]
                             + gates_t[3 * hd + 2:3 * hd + 3] * o_w[:, c])
    o_ref[0] = jnp.concatenate(heads_out, axis=0).T


def _nsa_prompt(q_t, gates_t, kc, ksel, sel_t, kwin, win_t, tq, tk):
    b, _, t = q_t.shape
    n_cmp = kc.shape[1]
    n_sb = t // L_SEL
    ovt = jnp.asarray(_overlap_matrix(n_cmp, n_sb).T)
    keys = pl.BlockSpec((1, NSA_GROUPS, t, LANES), lambda bi, i: (bi, 0, 0, 0))
    values = pl.BlockSpec((1, KV_W, t), lambda bi, i: (bi, 1, 0))
    return pl.pallas_call(
        functools.partial(_nsa_prompt_kernel, tq=tq, tk=tk),
        grid=(b, t // tq),
        in_specs=[pl.BlockSpec((1, W_A, tq), lambda bi, i: (bi, 0, i)),
                  pl.BlockSpec((1, LANES, tq), lambda bi, i: (bi, 0, i)),
                  pl.BlockSpec((1, n_cmp, 2 * KV_W), lambda bi, i: (bi, 0, 0)),
                  keys, values, keys, values, _const_spec((n_sb, n_cmp))],
        out_specs=pl.BlockSpec((1, tq, W_A), lambda bi, i: (bi, i, 0)),
        out_shape=jax.ShapeDtypeStruct((b, t, W_A), F32),
        compiler_params=pltpu.CompilerParams(dimension_semantics=("parallel", "arbitrary"),
                                             vmem_limit_bytes=VMEM_LIMIT),
        name="nsa_prompt",
    )(q_t, gates_t, kc, ksel, sel_t, kwin, win_t, ovt)


def _nsa_sample_kernel(pt_ref, q_ref, gate_ref, kc_ref, pool_ref, snew_ref, wcache_ref, wnew_ref, ov_ref, ex_ref,
                       o_ref, buf, sem, sel_scr, oc_scr, m_scr, l_scr, acc_scr, *, n_pages, past):
    slot = _paged_tile(pt_ref, pool_ref, buf, sem, n_pages)
    t = pl.program_id(1)
    nt = pl.num_programs(1)
    ds = q_ref.shape[1]
    rows = HPG * ds
    n_cmp = kc_ref.shape[1]
    n_sbp = ov_ref.shape[1]
    blocks_per_tile = n_pages * PAGE_SIZE // L_SEL
    n_past_blocks = past // L_SEL
    keys = n_pages * PAGE_SIZE
    q = q_ref[0]
    tok4 = lax.broadcasted_iota(jnp.int32, (rows, 1), 0) % ds

    @pl.when(t == 0)
    def _():
        tpos = past + lax.broadcasted_iota(jnp.int32, (ds, 1), 0)
        ridx = lax.broadcasted_iota(jnp.int32, (rows, n_cmp), 1)
        valid = (ridx >= 1) & (D_CMP * ridx + (L_CMP - D_CMP - 1) <= past + tok4)
        psum = []
        for g in range(NSA_GROUPS):
            o_c, ps = _compressed_branch(_stack_heads(q, g), kc_ref[0, :, g * HD:(g + 1) * HD],
                                         kc_ref[0, :, KV_W + g * HD:KV_W + (g + 1) * HD], valid, ds)
            oc_scr[g] = o_c
            psum.append(ps)
            m_scr[g], l_scr[g], acc_scr[g] = _online_init(rows)
        sel = _select_blocks(_hdot(jnp.concatenate(psum, axis=0), ov_ref[...]),
                             jnp.concatenate([tpos] * NSA_GROUPS, axis=0), n_past_blocks + 1)
        for g in range(NSA_GROUPS):
            sel4 = jnp.concatenate([sel[g * ds:(g + 1) * ds]] * HPG, axis=0)
            for tt in range(sel_scr.shape[0]):
                sel_scr[tt, g] = sel4[:, tt * LANES:(tt + 1) * LANES]

    kv_t = buf[slot]
    first_blk = t * blocks_per_tile
    ex = ex_ref[(first_blk % LANES) // blocks_per_tile]
    zero = jnp.zeros((rows, HD), F32)
    q_rows = jnp.concatenate([jnp.concatenate([_stack_heads(q, 0), zero], axis=1),
                              jnp.concatenate([zero, _stack_heads(q, 1)], axis=1)], axis=0)
    bias = jnp.concatenate([(sel_scr[first_blk // LANES, g] - 1.0) * (-NEG_INF) for g in range(NSA_GROUPS)], axis=0)
    s_all = _dg(jnp.concatenate([q_rows, bias], axis=1).astype(BF16),
                jnp.concatenate([kv_t[0:KV_W].astype(BF16), ex], axis=0))
    for g in range(NSA_GROUPS):
        vt = kv_t[KV_W + g * HD:KV_W + (g + 1) * HD].astype(BF16)
        m_scr[g], l_scr[g], acc_scr[g] = _online((m_scr[g], l_scr[g], acc_scr[g]), s_all[g * rows:(g + 1) * rows],
                                                 None, vt, NT)

    @pl.when(t == nt - 1)
    def _():
        gates = gate_ref[0]
        knew_pos = lax.broadcasted_iota(jnp.int32, (rows, ds), 1)
        snew = snew_ref[0]
        wnew = wnew_ref[0]
        wc_t = wcache_ref[0]
        n_wc = wc_t.shape[1]
        for g in range(NSA_GROUPS):
            qsb = _stack_heads(q, g).astype(BF16)
            kcol = slice(g * HD, (g + 1) * HD)
            vcol = slice(KV_W + g * HD, KV_W + (g + 1) * HD)
            sel_new = sel_scr[n_past_blocks // LANES, g][:, n_past_blocks % LANES:n_past_blocks % LANES + 1]
            st = _online((m_scr[g], l_scr[g], acc_scr[g]), _dg(qsb, snew[:, kcol].astype(BF16), NT),
                         (sel_new > 0.5) & (knew_pos <= tok4), snew[:, vcol].astype(BF16))
            o_s = _online_out(st)
            dist_c = (n_wc + tok4) - lax.broadcasted_iota(jnp.int32, (rows, n_wc), 1)
            st = _online(_online_init(rows), _dg(qsb, wc_t[kcol].astype(BF16)),
                         (dist_c >= 0) & (dist_c < WIN), wc_t[vcol].astype(BF16), NT)
            dist_n = tok4 - knew_pos
            st = _online(st, _dg(qsb, wnew[:, kcol].astype(BF16), NT), (dist_n >= 0) & (dist_n < WIN),
                         wnew[:, vcol].astype(BF16))
            o_w = _online_out(st)
            o_c = oc_scr[g]
            for h in range(HPG):
                hd = HPG * g + h
                r = slice(h * ds, (h + 1) * ds)
                o_ref[0, :, hd * HD:(hd + 1) * HD] = (gates[:, 3 * hd:3 * hd + 1] * o_c[r]
                                                      + gates[:, 3 * hd + 1:3 * hd + 2] * o_s[r]
                                                      + gates[:, 3 * hd + 2:3 * hd + 3] * o_w[r])


def _nsa_sample(page_table, q, gates, kc, pool, sel_new, win_cache, win_new, n_pages):
    nb, ds = q.shape[:2]
    pages_per_seq = page_table.shape[1]
    past = pages_per_seq * PAGE_SIZE
    nt = pages_per_seq // n_pages
    n_cmp = kc.shape[1]
    n_sb = past // L_SEL + 1
    n_sbp = -(-n_sb // LANES) * LANES
    blocks_per_tile = n_pages * PAGE_SIZE // L_SEL
    keys = n_pages * PAGE_SIZE
    rows = HPG * ds
    ov = jnp.asarray(_overlap_matrix(n_cmp, n_sbp) * (np.arange(n_sbp) < n_sb)[None, :])
    ex = (np.arange(LANES)[None, :, None]
          == (np.arange(LANES // blocks_per_tile)[:, None, None] * blocks_per_tile + np.arange(keys)[None, None, :] // L_SEL))
    ex = jnp.asarray(ex, dtype=BF16)
    seq = lambda r, w: pl.BlockSpec((1, r, w), lambda b, t, pt: (b, 0, 0))
    cst = lambda shape: pl.BlockSpec(shape, lambda b, t, pt: (0,) * len(shape))
    return pl.pallas_call(
        functools.partial(_nsa_sample_kernel, n_pages=n_pages, past=past),
        grid_spec=pltpu.PrefetchScalarGridSpec(
            num_scalar_prefetch=1,
            grid=(nb, nt),
            in_specs=[seq(ds, W_A), seq(ds, LANES), seq(n_cmp, 2 * KV_W), pl.BlockSpec(memory_space=pl.ANY),
                      seq(ds, 2 * KV_W), seq(2 * KV_W, win_cache.shape[2]), seq(ds, 2 * KV_W),
                      cst((n_cmp, n_sbp)), cst(ex.shape)],
            out_specs=seq(ds, W_A),
            scratch_shapes=[pltpu.VMEM((2, 2 * KV_W, keys), F32),
                            pltpu.SemaphoreType.DMA((2,)),
                            pltpu.VMEM((n_sbp // LANES, NSA_GROUPS, rows, LANES), F32),
                            pltpu.VMEM((NSA_GROUPS, rows, HD), F32),
                            pltpu.VMEM((NSA_GROUPS, rows, 1), F32),
                            pltpu.VMEM((NSA_GROUPS, rows, 1), F32),
                            pltpu.VMEM((NSA_GROUPS, rows, HD), F32)]),
        out_shape=jax.ShapeDtypeStruct((nb, ds, W_A), F32),
        compiler_params=pltpu.CompilerParams(dimension_semantics=("arbitrary", "arbitrary"),
                                             vmem_limit_bytes=VMEM_LIMIT),
        name="nsa_sample",
    )(page_table, q, gates, kc, pool, sel_new, win_cache, win_new, ov, ex)


def _sum_heads(x, ones_ref):
    hi, lo = _split(x)
    return _dg(hi, ones_ref[...]) + _dg(lo, ones_ref[...])


def _rwkv_prep_kernel(rz_ref, before_ref, shift_ref, mu_ref, w0_ref, wup_ref, a0_ref, aup_ref, kk_ref, ka_ref, rk_ref,
                      gnb_ref, ones_ref, a_ref, o_ref, y1_ref, y2_ref, yadd_ref, *, chunk, t_valid):
    rows = rz_ref.shape[1]
    n_chunks = rows // chunk
    shift = int(np.log2(chunk))
    z = rz_ref[0]
    ridx = lax.broadcasted_iota(jnp.int32, (rows, 1), 0)
    n_before = before_ref.shape[1]
    row_before = jnp.where(pl.program_id(1) == 0, shift_ref[0], before_ref[0, n_before - 1:n_before])
    prev = jnp.where(ridx == 0, row_before, pltpu.roll(z, 1, 0))
    zs = z + (prev - z) * mu_ref[...]
    r = zs[:, 0:W_R]
    k = zs[:, W_R:2 * W_R]
    v = zs[:, 2 * W_R:3 * W_R]
    wd = zs[:, 3 * W_R:3 * W_R + R_W]
    ad = zs[:, 3 * W_R + R_W:C_R]
    wpre = w0_ref[...] + _hdot(jnp.tanh(wd), wup_ref[...])
    w = -(jnp.maximum(-wpre, 0.0) + jnp.log(1.0 + jnp.exp(-jnp.abs(wpre)))) - 0.5
    ld = -jnp.exp(w)
    a = _sigmoid(a0_ref[...] + _hdot(ad, aup_ref[...]))
    kkn = k * kk_ref[...]
    kk = kkn / jnp.maximum(jnp.sqrt(_sum_heads(kkn * kkn, ones_ref)), 1e-12)
    k2 = k * (1.0 + (a - 1.0) * ka_ref[...])
    yadd_ref[0] = _sum_heads(r * k2 * rk_ref[...], ones_ref) * v + gnb_ref[...]
    if t_valid < chunk:
        live = (ridx & (chunk - 1)) < t_valid
        ld = jnp.where(live, ld, 0.0)
        kk = jnp.where(live, kk, 0.0)
        k2 = jnp.where(live, k2, 0.0)
    ti = lax.broadcasted_iota(jnp.int32, (rows, rows), 0)
    tj = lax.broadcasted_iota(jnp.int32, (rows, rows), 1)
    tri = jnp.where((jnp.right_shift(ti, shift) == jnp.right_shift(tj, shift)) & (tj <= ti), 1.0, 0.0).astype(BF16)
    ld_h = ld.astype(BF16)
    ld_m = (ld - ld_h.astype(F32)).astype(BF16)
    ld_l = (ld - ld_h.astype(F32) - ld_m.astype(F32)).astype(BF16)
    cum = _dg(tri, ld_h) + (_dg(tri, ld_m) + _dg(tri, ld_l))
    cum_end = jnp.concatenate([jnp.broadcast_to(cum[(c + 1) * chunk - 1:(c + 1) * chunk], (chunk, W_R))
                               for c in range(n_chunks)], axis=0)
    p_inv = jnp.exp(-cum)
    p_end = jnp.exp(cum_end - cum)
    kka = kk * a
    at = -kk * jnp.exp(cum - ld)
    rt = r * jnp.exp(cum)
    at_b, rt_b, v_b = at.astype(BF16), rt.astype(BF16), v.astype(BF16)
    bt_b, kt_b = (kka * p_inv).astype(BF16), (k2 * p_inv).astype(BF16)
    be_b, ke_b = (kka * p_end).astype(BF16), (k2 * p_end).astype(BF16)
    dec_end = jnp.exp(cum_end)

    row = lax.broadcasted_iota(jnp.int32, (chunk, PAIR_W), 0)
    col = lax.broadcasted_iota(jnp.int32, (chunk, PAIR_W), 1) & (R_HD - 1)
    strict, incl = col < row, col <= row
    eye_pair = jnp.where(col == row, 1.0, 0.0)
    n_levels = max(1, int(np.ceil(np.log2(chunk))))
    units = [(c, p) for c in range(n_chunks) for p in range(R_HEADS // 2)]
    idx = range(len(units))
    cut = lambda x, u: x[u[0] * chunk:(u[0] + 1) * chunk, u[1] * PAIR_W:(u[1] + 1) * PAIR_W]
    mm = [_dg(jnp.concatenate([cut(at_b, u), cut(rt_b, u)], axis=0),
              jnp.concatenate([_pair_diag(cut(bt_b, u)), _pair_diag(cut(kt_b, u))], axis=0), NT) for u in units]
    low = [jnp.where(strict, m[0:chunk, 0:PAIR_W], 0.0) for m in mm]
    m_k = [jnp.concatenate([jnp.where(strict, m[0:chunk, PAIR_W:], 0.0), jnp.where(incl, m[chunk:, PAIR_W:], 0.0)],
                           axis=0).astype(BF16) for m in mm]
    m_rb = [jnp.where(incl, m[chunk:, 0:PAIR_W], 0.0).astype(BF16) for m in mm]
    mv = [_dg(m_k[i], _pair_diag(cut(v_b, u))) for i, u in enumerate(units)]
    inv = [eye_pair + x for x in low]
    pw = [x.astype(BF16) for x in low]
    for lvl in range(1, n_levels + 1):
        if lvl == 1:
            pw = [_dg(x, _pair_diag(x)).astype(BF16) for x in pw]
        elif lvl <= n_levels:
            last = lvl == n_levels
            nxt = [_dg(pw[i], _pair_diag(inv[i].astype(BF16)) if last else
                       jnp.concatenate([_pair_diag(inv[i].astype(BF16)), _pair_diag(pw[i])], axis=1)) for i in idx]
            inv = [inv[i] + nxt[i][:, 0:PAIR_W] for i in idx]
            if not last:
                pw = [nxt[i][:, PAIR_W:].astype(BF16) for i in idx]
    inv_b = [x.astype(BF16) for x in inv]
    w = [_dg(inv_b[i], jnp.concatenate([_pair_diag(cut(at_b, u)), _pair_diag(mv[i][0:chunk].astype(BF16))], axis=1))
         for i, u in enumerate(units)]
    w_b = [x.astype(BF16) for x in w]
    y12 = [_dg(m_rb[i], jnp.concatenate([_pair_diag(w_b[i][:, 0:PAIR_W]), _pair_diag(w_b[i][:, PAIR_W:])], axis=1))
           for i in idx]
    prow = lax.broadcasted_iota(jnp.int32, (PAIR_W, PAIR_W), 0)
    pcol = lax.broadcasted_iota(jnp.int32, (PAIR_W, PAIR_W), 1)
    same_head = (prow < R_HD) == (pcol < R_HD)
    col_first = lax.broadcasted_iota(jnp.int32, (R_HD, PAIR_W), 1) < R_HD
    for i, u in enumerate(units):
        c, p = u
        rs = slice(c * chunk, (c + 1) * chunk)
        ls = slice(p * PAIR_W, (p + 1) * PAIR_W)
        a_mat = jnp.where(same_head, _dg(w_b[i][:, 0:PAIR_W], cut(be_b, u), TN), 0.0)
        a_mat = a_mat + jnp.where(prow == pcol, jnp.broadcast_to(dec_end[rs, ls][0:1], (PAIR_W, PAIR_W)), 0.0)
        a_ref[0, c, p] = a_mat.astype(BF16)
        o_full = _dg(jnp.concatenate([w_b[i][:, PAIR_W:], cut(v_b, u)], axis=0),
                     jnp.concatenate([cut(be_b, u), cut(ke_b, u)], axis=0), TN)
        o_ref[0, c, p] = jnp.where(col_first, o_full[0:R_HD], o_full[R_HD:])
        y1_ref[0, rs, ls] = (y12[i][:, 0:PAIR_W] + cut(rt, u)).astype(BF16)
        y2_ref[0, rs, ls] = y12[i][:, PAIR_W:] + mv[i][chunk:]


PAIR_W = 2 * R_HD


def _pair_diag(x):
    first = jnp.where(lax.broadcasted_iota(jnp.int32, x.shape, 1) < R_HD, 1.0, 0.0).astype(x.dtype)
    return jnp.concatenate([x * first, x * (1 - first)], axis=0)


def _rwkv_scan_kernel(a_ref, o_ref, y1_ref, y2_ref, yadd_ref, s0_ref, gng_ref, ones_ref, y_ref, st, sout_ref):
    units = [(b, p) for b in range(st.shape[0]) for p in range(R_HEADS // 2)]

    @pl.when(pl.program_id(1) == 0)
    def _():
        for b, p in units:
            st[b, p] = jnp.concatenate([s0_ref[b, 2 * p], s0_ref[b, 2 * p + 1]], axis=1)

    lsl = lambda p: slice(p * PAIR_W, (p + 1) * PAIR_W)
    s_b = [st[b, p].astype(BF16) for b, p in units]
    ys = [_dg(y1_ref[b, :, lsl(p)], _pair_diag(s_b[i]), NT) + y2_ref[b, :, lsl(p)] for i, (b, p) in enumerate(units)]
    for i, (b, p) in enumerate(units):
        st[b, p] = _dg(s_b[i], a_ref[b, 0, p]) + o_ref[b, 0, p]
    mean = [_sum_heads(y, ones_ref) * (1.0 / R_HD) for y in ys]
    dev = [ys[i] - mean[i] for i in range(len(units))]
    var = [_sum_heads(d * d, ones_ref) * (1.0 / R_HD) for d in dev]
    for i, (b, p) in enumerate(units):
        y_ref[b, :, lsl(p)] = dev[i] * lax.rsqrt(var[i] + GN_EPS) * gng_ref[:, lsl(p)] + yadd_ref[b, :, lsl(p)]

    for b, p in units:
        s = st[b, p]
        sout_ref[b, 2 * p] = s[:, 0:R_HD]
        sout_ref[b, 2 * p + 1] = s[:, R_HD:]


def _rwkv(rz, shift0, state0, params, chunk, chunks_per_step, seqs_per_step, t_valid):
    mu, w0, wup, a0, aup, k_k, k_a, r_k, gn_g, gn_b = params
    b, t = rz.shape[:2]
    assert chunk == R_HD, "pair packing puts the chunk's time index on a head's 64 lanes"
    rows = chunk * chunks_per_step
    n_chunks = t // chunk
    n_pairs = R_HEADS // 2
    n_before = 8
    ones = jnp.asarray(np.kron(np.eye(R_HEADS), np.ones((R_HD, R_HD))), dtype=BF16)
    vec = lambda n: _const_spec((1, n))
    rowsd = lambda dt: jax.ShapeDtypeStruct((b, t, W_R), dt)
    row_spec = pl.BlockSpec((1, rows, W_R), lambda bi, c: (bi, c, 0))
    a_m, o_m, y1, y2, yadd = pl.pallas_call(
        functools.partial(_rwkv_prep_kernel, chunk=chunk, t_valid=t_valid),
        grid=(b, t // rows),
        in_specs=[pl.BlockSpec((1, rows, C_R), lambda bi, c: (bi, c, 0)),
                  pl.BlockSpec((1, n_before, C_R),
                               lambda bi, c: (bi, jnp.maximum(c * (rows // n_before) - 1, 0), 0)),
                  pl.BlockSpec((1, 1, C_R), lambda bi, c: (bi, 0, 0)),
                  vec(C_R), vec(W_R), _const_spec((R_W, W_R)), vec(W_R), _const_spec((R_A, W_R)),
                  vec(W_R), vec(W_R), vec(W_R), vec(W_R), _const_spec((W_R, W_R))],
        out_specs=[pl.BlockSpec((1, chunks_per_step, n_pairs, PAIR_W, PAIR_W), lambda bi, c: (bi, c, 0, 0, 0)),
                   pl.BlockSpec((1, chunks_per_step, n_pairs, R_HD, PAIR_W), lambda bi, c: (bi, c, 0, 0, 0)),
                   row_spec, row_spec, row_spec],
        out_shape=[jax.ShapeDtypeStruct((b, n_chunks, n_pairs, PAIR_W, PAIR_W), BF16),
                   jax.ShapeDtypeStruct((b, n_chunks, n_pairs, R_HD, PAIR_W), F32),
                   rowsd(BF16), rowsd(F32), rowsd(F32)],
        compiler_params=pltpu.CompilerParams(dimension_semantics=("parallel", "parallel"),
                                             vmem_limit_bytes=VMEM_LIMIT),
        name="rwkv_prep",
    )(rz, rz, shift0.reshape(b, 1, C_R), mu, w0, wup, a0, aup, k_k, k_a, r_k, gn_b, ones)
    nb = seqs_per_step
    row1 = pl.BlockSpec((nb, chunk, W_R), lambda bi, c: (bi, c, 0))
    pair_state = pl.BlockSpec((nb, n_pairs, R_HD, PAIR_W), lambda bi, c: (bi, 0, 0, 0))
    head_state = pl.BlockSpec((nb, R_HEADS, R_HD, R_HD), lambda bi, c: (bi, 0, 0, 0))
    y, _, state = pl.pallas_call(
        _rwkv_scan_kernel,
        grid=(b // nb, n_chunks),
        in_specs=[pl.BlockSpec((nb, 1, n_pairs, PAIR_W, PAIR_W), lambda bi, c: (bi, c, 0, 0, 0)),
                  pl.BlockSpec((nb, 1, n_pairs, R_HD, PAIR_W), lambda bi, c: (bi, c, 0, 0, 0)),
                  row1, row1, row1, head_state, vec(W_R), _const_spec((PAIR_W, PAIR_W))],
        out_specs=[row1, pair_state, head_state],
        out_shape=[rowsd(F32), jax.ShapeDtypeStruct((b, n_pairs, R_HD, PAIR_W), F32),
                   jax.ShapeDtypeStruct((b, R_HEADS, R_HD, R_HD), F32)],
        compiler_params=pltpu.CompilerParams(dimension_semantics=("parallel", "arbitrary"),
                                             vmem_limit_bytes=VMEM_LIMIT),
        name="rwkv_scan",
    )(a_m, o_m, y1, y2, yadd, state0, gn_g, ones[:PAIR_W, :PAIR_W])
    return y, state


def _rope_tables(pos):
    half = HD // 2
    inv = ROPE_THETA ** (-jnp.arange(half, dtype=F32) / half)
    ang = pos.astype(F32)[:, None] * inv[None, :]
    cos, sin = jnp.cos(ang), jnp.sin(ang)
    reps = LANES // HD
    return (jnp.tile(jnp.concatenate([cos, cos], axis=-1), (1, reps)),
            jnp.tile(jnp.concatenate([-sin, sin], axis=-1), (1, reps)))


def _pick_tile(n, cap):
    t = cap
    while n % t:
        t //= 2
    return t


def kernel(x_prompt, x_sample, p_prompt, p_sample, cache_cmp_kv, cache_sel_kv, cache_win_kv, state_wkv, state_shift, page_table, norm_g, w_in, cmp_pe, cmp_w1, cmp_b1, cmp_w2, cmp_b2, w_pa, rwkv_mu, rwkv_w0, rwkv_w_up, rwkv_a0, rwkv_a_up, rwkv_k_k, rwkv_k_a, rwkv_r_k, rwkv_gn_g, rwkv_gn_b, w_pb, w_out, ple_norm_g, w_ple_gate, w_ple_proj, final_norm_g):
    b, s_len = x_prompt.shape[:2]
    db, ds = x_sample.shape[:2]
    depth = norm_g.shape[0]
    assert depth == 1, "single-layer trunk"
    n_pages = page_table.shape[1]
    past = n_pages * PAGE_SIZE
    win_buf = cache_win_kv.shape[2]
    n_pool = cache_cmp_kv.shape[1]
    i = 0

    w = w_in[i]
    o_q, o_kv, o_ng, o_ag = 0, W_A, W_A + 6 * KV_W, W_A + 6 * KV_W + 3 * NSA_HEADS
    w_pack = jnp.concatenate([w[:, o_q:o_ng], w[:, o_ag:], w[:, o_ng:o_ag],
                              jnp.zeros((D_MODEL, C_PACK - w.shape[1]), w.dtype)], axis=1).astype(BF16)
    cw = _compress_weights(cmp_pe[i], cmp_w1[i], cmp_b1[i], cmp_w2[i], cmp_b2[i])
    row = lambda u: u.reshape(1, -1)
    rw = (row(rwkv_mu[i]), row(rwkv_w0[i]), rwkv_w_up[i], row(rwkv_a0[i]), rwkv_a_up[i], row(rwkv_k_k[i]),
          row(rwkv_k_a[i]), row(rwkv_r_k[i]), row(rwkv_gn_g[i]), row(rwkv_gn_b[i]))
    ow = (w_pa[i].astype(BF16), w_pb[i].astype(BF16), w_out[i].astype(BF16), row(ple_norm_g[i]),
          w_ple_gate[i].astype(BF16), w_ple_proj[i].astype(BF16), row(final_norm_g))

    tm = _pick_tile(s_len, 256)
    cos_p, sin_p = _rope_tables(jnp.arange(s_len, dtype=jnp.int32))
    xp2 = x_prompt.reshape(b * s_len, D_MODEL)
    q, cmp_p, sel_p, win_p, gates, sa, rz, sr, mg, ksel, kwin = _project(xp2, row(norm_g[i]), w_pack, cos_p, sin_p,
                                                                        tm, True)
    kc_p = _compress_seq(cmp_p, cw)
    seq = lambda u: u.reshape(b, s_len, u.shape[-1])
    o_a = _nsa_prompt(q, gates, kc_p, ksel, sel_p, kwin, win_p, tq=128, tk=min(512, s_len))
    chunk = 64
    o_b, st_p = _rwkv(seq(rz), jnp.zeros((b, C_R), F32), jnp.zeros((b, R_HEADS, R_HD, R_HD), F32), rw, chunk,
                      chunks_per_step=4, seqs_per_step=_pick_tile(b, 4), t_valid=chunk)
    y_p = _merge(xp2, o_a.reshape(b * s_len, W_A), o_b.reshape(b * s_len, W_R), sa, sr, mg,
                 p_prompt[i].reshape(b * s_len, D_PLE), *ow, tm=_pick_tile(b * s_len, 512))
    kv_shape = lambda u, n, t: u.reshape(n, t, 2, NSA_GROUPS, HD)
    from_cm = lambda u: jnp.transpose(u.reshape(u.shape[0], 2, NSA_GROUPS, HD, u.shape[2]), (0, 4, 1, 2, 3))
    to_cm = lambda u: jnp.transpose(u, (0, 2, 3, 4, 1)).reshape(u.shape[0], 2 * KV_W, u.shape[1])
    cmp_kv_p = from_cm(cmp_p)
    sel_kv_p = from_cm(sel_p)
    win_kv_p = from_cm(win_p[:, :, s_len - min(WIN, s_len):])
    shift_p = seq(rz)[:, -1]

    n_s = db * ds
    cos_s, sin_s = _rope_tables(jnp.tile(past + jnp.arange(ds, dtype=jnp.int32), db))
    xs2 = x_sample.reshape(n_s, D_MODEL)
    q, cmp_s, sel_s, win_s, gates, sa, rz, sr, mg = _project(xs2, row(norm_g[i]), w_pack, cos_s, sin_s, n_s, False)
    kc_s = _compress_paged(page_table, to_cm(cache_cmp_kv[i]), cw, n_pages=_pick_tile(n_pages, 64))
    seqs = lambda u: u.reshape(db, ds, u.shape[-1])
    o_a = _nsa_sample(page_table, seqs(q), seqs(gates), kc_s, to_cm(cache_sel_kv[i]),
                      seqs(sel_s), to_cm(cache_win_kv[i]), seqs(win_s), n_pages=_pick_tile(n_pages, 32))
    pad_t = chunk
    rz_pad = jnp.pad(seqs(rz), ((0, 0), (0, pad_t - ds), (0, 0)))
    o_b, st_s = _rwkv(rz_pad, state_shift[i], state_wkv[i], rw, pad_t, chunks_per_step=1,
                      seqs_per_step=_pick_tile(db, 4), t_valid=ds)
    y_s = _merge(xs2, o_a.reshape(n_s, W_A), o_b[:, :ds].reshape(n_s, W_R), sa, sr, mg,
                 p_sample[i].reshape(n_s, D_PLE), *ow, tm=n_s)
    win_kv_s = jnp.concatenate([cache_win_kv[i], kv_shape(win_s, db, ds)], axis=1)[:, ds:]
    shift_s = seqs(rz)[:, -1]

    return (y_p.reshape(b, s_len, D_MODEL), y_s.reshape(db, ds, D_MODEL),
            cmp_kv_p[None], kv_shape(cmp_s, db, ds)[None], sel_kv_p[None], kv_shape(sel_s, db, ds)[None],
            win_kv_p[None], win_kv_s[None], st_p[None], st_s[None], shift_p[None], shift_s[None])
```

```python
import functools

import numpy as np
import jax
import jax.numpy as jnp
from jax import lax
from jax.experimental import pallas as pl
from jax.experimental.pallas import tpu as pltpu

F32 = jnp.float32
BF16 = jnp.bfloat16

D_MODEL = 1024
D_PLE = 256
NSA_HEADS = 8
NSA_GROUPS = 2
HD = 64
HPG = NSA_HEADS // NSA_GROUPS
W_A = NSA_HEADS * HD
KV_W = NSA_GROUPS * HD
L_CMP = 32
D_CMP = 16
CMP_HID = 128
L_SEL = 64
N_SEL = 16
WIN = 512
FORCE_SCORE = 1e4
NEG_INF = -1e30
R_HEADS = 8
R_HD = 64
W_R = R_HEADS * R_HD
R_W = 64
R_A = 64
C_R = 3 * W_R + R_W + R_A
GN_EPS = 64e-5
ROPE_THETA = 10000.0
NORM_EPS = 1e-6
PAGE_SIZE = 128

LANES = 128
CHUNKS_PER_PAGE = PAGE_SIZE // D_CMP
CHUNK_W = D_CMP * 2 * KV_W
VMEM_LIMIT = 56 * 1024 * 1024

C_Q = 0
C_KV = 512
C_AG = 1280
C_RZ = 1792
C_RG = 3456
C_MG = 3968
C_NG = 6016
C_PACK = 6144

NN = ((1,), (0,))
NT = ((1,), (1,))
TN = ((0,), (0,))


def _dg(a, b, dims=NN):
    return lax.dot_general(a, b, (dims, ((), ())), preferred_element_type=F32)


def _split(x):
    hi = x.astype(BF16)
    lo = (x - hi.astype(F32)).astype(BF16)
    return hi, lo


def _hdot_s(a, b, dims=NN):
    (ah, al), (bh, bl) = a, b
    return _dg(ah, bh, dims) + (_dg(ah, bl, dims) + _dg(al, bh, dims))


def _hdot(a, b, dims=NN):
    return _hdot_s(_split(a), _split(b), dims)


def _sigmoid(x):
    return 1.0 / (1.0 + jnp.exp(-x))


def _silu(x):
    return x * _sigmoid(x)


def _rms(x, g):
    return x * lax.rsqrt(jnp.mean(x * x, axis=-1, keepdims=True) + NORM_EPS) * g


def _const_spec(shape):
    zeros = (0,) * len(shape)
    return pl.BlockSpec(shape, lambda *_: zeros)


def _rope128(z, cos, sin, first):
    partner = jnp.where(first, pltpu.roll(z, LANES - HD // 2, 1), pltpu.roll(z, HD // 2, 1))
    return z * cos + partner * sin


def _proj_kernel(x_ref, g_ref, wa_ref, wb_ref, wc_ref, cos_ref, sin_ref,
                 q_ref, cmp_ref, sel_ref, win_ref, gate_ref, sa_ref, rz_ref, sr_ref, mg_ref, *aug_refs,
                 seq_major, n_pos_blocks):
    hb = _rms(x_ref[...], g_ref[...]).astype(BF16)
    cos = cos_ref[...]
    sin = sin_ref[...]
    tm = cos.shape[0]
    lane = lax.broadcasted_iota(jnp.int32, cos.shape, 1)
    first = (lane % HD) < (HD // 2)

    def mm(a, b):
        ref, off = (wa_ref, 0) if b <= C_AG else (wc_ref, C_NG) if a >= C_NG else (wb_ref, C_AG)
        return jnp.dot(hb, ref[:, a - off:b - off], preferred_element_type=F32)

    zq = mm(C_Q, C_Q + W_A)
    q = jnp.concatenate([_rope128(zq[:, c * LANES:(c + 1) * LANES], cos, sin, first) for c in range(W_A // LANES)],
                        axis=1)
    gates = _sigmoid(mm(C_NG, C_PACK))
    if seq_major:
        q_ref[0] = q.T
        gate_ref[0] = gates.T
    else:
        q_ref[...] = q
        gate_ref[...] = gates
    for i, ref in enumerate((cmp_ref, sel_ref, win_ref)):
        zkv = mm(C_KV + 2 * KV_W * i, C_KV + 2 * KV_W * (i + 1))
        k_rot = _rope128(zkv[:, 0:KV_W], cos, sin, first)
        if seq_major:
            ref[0] = jnp.concatenate([k_rot, zkv[:, KV_W:2 * KV_W]], axis=1).T
            if i > 0:
                aug_ref = aug_refs[i - 1]
                if i == 1:
                    t_in_seq = ((pl.program_id(0) % n_pos_blocks) * tm
                                + lax.broadcasted_iota(jnp.int32, cos.shape, 0))
                    tail = jnp.where(jnp.right_shift(t_in_seq, int(np.log2(L_SEL))) == lane - HD, 1.0, 0.0)
                else:
                    tail = jnp.zeros(cos.shape, F32)
                for g in range(NSA_GROUPS):
                    k_g = k_rot if g == 0 else pltpu.roll(k_rot, LANES - g * HD, 1)
                    aug_ref[0, g] = jnp.where(lane < HD, k_g, tail).astype(BF16)
        else:
            ref[:, 0:KV_W] = k_rot
            ref[:, KV_W:2 * KV_W] = zkv[:, KV_W:2 * KV_W]
    sa_ref[...] = _silu(mm(C_AG, C_AG + W_A))
    rz_ref[...] = mm(C_RZ, C_RZ + C_R)
    sr_ref[...] = _silu(mm(C_RG, C_RG + W_R))
    mg_ref[...] = _sigmoid(mm(C_MG, C_MG + 2 * D_MODEL))


def _project(x2d, norm_g, w_pack, cos_t, sin_t, tm, seq_major):
    n = x2d.shape[0]
    seq_len = cos_t.shape[0]
    n_pos_blocks = seq_len // tm
    n_seq = n // seq_len
    widths = (W_A, 2 * KV_W, 2 * KV_W, 2 * KV_W, LANES, W_A, C_R, W_R, 2 * D_MODEL)
    row = lambda w: pl.BlockSpec((tm, w), lambda i: (i, 0))
    pos = pl.BlockSpec((tm, LANES), lambda i: (i % n_pos_blocks, 0))
    out_specs = [row(w) for w in widths]
    out_shape = [jax.ShapeDtypeStruct((n, w), F32) for w in widths]
    in_specs = [row(D_MODEL), _const_spec((1, D_MODEL))] + [_const_spec(w.shape) for w in w_pack] + [pos, pos]
    args = [x2d, norm_g, *w_pack, cos_t, sin_t]
    if seq_major:
        assert seq_len // L_SEL <= LANES - HD, "selection-block one-hot must fit beside the key"
        for o in range(5):
            out_specs[o] = pl.BlockSpec((1, widths[o], tm), lambda i: (i // n_pos_blocks, 0, i % n_pos_blocks))
            out_shape[o] = jax.ShapeDtypeStruct((n_seq, widths[o], seq_len), F32)
        for _ in range(2):
            out_specs.append(pl.BlockSpec((1, NSA_GROUPS, tm, LANES),
                                          lambda i: (i // n_pos_blocks, 0, i % n_pos_blocks, 0)))
            out_shape.append(jax.ShapeDtypeStruct((n_seq, NSA_GROUPS, seq_len, LANES), BF16))
    return pl.pallas_call(
        functools.partial(_proj_kernel, seq_major=seq_major, n_pos_blocks=n_pos_blocks),
        grid=(n // tm,),
        in_specs=in_specs,
        out_specs=out_specs,
        out_shape=out_shape,
        compiler_params=pltpu.CompilerParams(dimension_semantics=("parallel",), vmem_limit_bytes=VMEM_LIMIT),
        name="project",
    )(*args)


def _merge_kernel(x_ref, oa_ref, ob_ref, sa_ref, sr_ref, mg_ref, p_ref,
                  wpa_ref, wpb_ref, wout_ref, pg_ref, wgate_ref, wproj_ref, fg_ref, y_ref):
    bdot = lambda a, w_ref: jnp.dot(a.astype(BF16), w_ref[...], preferred_element_type=F32)
    ya = bdot(oa_ref[...] * sa_ref[...], wpa_ref)
    yb = bdot(ob_ref[...] * sr_ref[...], wpb_ref)
    m = mg_ref[:, 0:D_MODEL] * ya + mg_ref[:, D_MODEL:2 * D_MODEL] * yb
    x2 = x_ref[...] + bdot(m, wout_ref)
    gate = _sigmoid(bdot(_rms(x2, pg_ref[...]), wgate_ref))
    x3 = x2 + gate * bdot(p_ref[...], wproj_ref)
    y_ref[...] = _rms(x3, fg_ref[...])


def _merge(x2d, oa, ob, sa, sr, mg, p2d, wpa, wpb, wout, pg, wgate, wproj, fg, tm):
    n = x2d.shape[0]
    row = lambda w: pl.BlockSpec((tm, w), lambda i: (i, 0))
    return pl.pallas_call(
        _merge_kernel,
        grid=(n // tm,),
        in_specs=[row(D_MODEL), row(W_A), row(W_R), row(W_A), row(W_R), row(2 * D_MODEL), row(D_PLE),
                  _const_spec((W_A, D_MODEL)), _const_spec((W_R, D_MODEL)), _const_spec((D_MODEL, D_MODEL)),
                  _const_spec((1, D_MODEL)), _const_spec((D_MODEL, D_MODEL)), _const_spec((D_PLE, D_MODEL)),
                  _const_spec((1, D_MODEL))],
        out_specs=row(D_MODEL),
        out_shape=jax.ShapeDtypeStruct((n, D_MODEL), F32),
        compiler_params=pltpu.CompilerParams(dimension_semantics=("parallel",), vmem_limit_bytes=VMEM_LIMIT),
        name="merge",
    )(x2d, oa, ob, sa, sr, mg, p2d, wpa, wpb, wout, pg, wgate, wproj, fg)


def _page_copies(pt_ref, pool_ref, buf, sem, b, t, slot, n_pages):
    return [pltpu.make_async_copy(pool_ref.at[pt_ref[b, t * n_pages + p]],
                                  buf.at[slot, :, pl.ds(p * PAGE_SIZE, PAGE_SIZE)], sem.at[slot])
            for p in range(n_pages)]


def _paged_tile(pt_ref, pool_ref, buf, sem, n_pages):
    b, t = pl.program_id(0), pl.program_id(1)
    nt = pl.num_programs(1)
    step = b * nt + t
    slot = step % 2

    @pl.when(step == 0)
    def _():
        for cp in _page_copies(pt_ref, pool_ref, buf, sem, b, t, slot, n_pages):
            cp.start()

    nxt = step + 1

    @pl.when(nxt < pl.num_programs(0) * nt)
    def _():
        for cp in _page_copies(pt_ref, pool_ref, buf, sem, nxt // nt, nxt % nt, 1 - slot, n_pages):
            cp.start()

    for cp in _page_copies(pt_ref, pool_ref, buf, sem, b, t, slot, n_pages):
        cp.wait()
    return slot


CMP_HALF = NSA_GROUPS * CMP_HID


def _compress_core(page_t, xs, perm_ref, w1_ref, pe_ref, b1_ref, w2_ref, b2_ref, carry, out_ref):
    n_pages = xs.shape[0]
    rows = n_pages * CHUNKS_PER_PAGE
    for p in range(n_pages):
        xp = _dg(perm_ref[...], page_t(p).astype(BF16), NT)
        xs[p] = xp.reshape(D_CMP, CHUNKS_PER_PAGE, 2 * KV_W)
    ridx = lax.broadcasted_iota(jnp.int32, (rows, CMP_HALF), 0)
    for kv in range(2):
        cs = slice(kv * KV_W, (kv + 1) * KV_W)
        x = jnp.concatenate([xs[:, j, :, cs].reshape(rows, KV_W) for j in range(D_CMP)], axis=1)
        x = jnp.concatenate([x, pe_ref[kv]], axis=0).astype(BF16)
        z = jnp.dot(x, w1_ref[kv], preferred_element_type=F32)
        pos = z[rows:rows + 1, 0:CMP_HALF] + z[rows + 1:rows + 2, CMP_HALF:]
        lo = z[0:rows, 0:CMP_HALF]
        lo_prev = jnp.where(ridx == 0, carry[kv], pltpu.roll(lo, 1, 0))
        carry[kv] = lo[rows - 1:rows]
        hid = lo_prev + z[0:rows, CMP_HALF:] + pos + b1_ref[kv]
        out_ref[0, :, cs] = jnp.dot(_silu(hid).astype(BF16), w2_ref[kv], preferred_element_type=F32) + b2_ref[kv]


def _compress_paged_kernel(pt_ref, pool_ref, perm_ref, w1_ref, pe_ref, b1_ref, w2_ref, b2_ref, out_ref,
                           buf, sem, xs, carry, *, n_pages):
    slot = _paged_tile(pt_ref, pool_ref, buf, sem, n_pages)

    @pl.when(pl.program_id(1) == 0)
    def _():
        carry[...] = jnp.zeros_like(carry)

    _compress_core(lambda p: buf[slot, :, p * PAGE_SIZE:(p + 1) * PAGE_SIZE], xs, perm_ref, w1_ref, pe_ref, b1_ref,
                   w2_ref, b2_ref, carry, out_ref)


def _compress_seq_kernel(x_ref, perm_ref, w1_ref, pe_ref, b1_ref, w2_ref, b2_ref, out_ref, xs, carry):
    carry[...] = jnp.zeros_like(carry)
    _compress_core(lambda p: x_ref[0, :, p * PAGE_SIZE:(p + 1) * PAGE_SIZE], xs, perm_ref, w1_ref, pe_ref, b1_ref,
                   w2_ref, b2_ref, carry, out_ref)


def _compress_specs(cst):
    return [cst((PAGE_SIZE, PAGE_SIZE)), cst((2, D_CMP * KV_W, 2 * CMP_HALF)), cst((2, 8, D_CMP * KV_W)),
            cst((2, 1, CMP_HALF)), cst((2, CMP_HALF, KV_W)), cst((2, 1, KV_W))]


def _compress_scratch(n_pages):
    return [pltpu.VMEM((n_pages, D_CMP, CHUNKS_PER_PAGE, 2 * KV_W), F32), pltpu.VMEM((2, 1, CMP_HALF), F32)]


def _compress_paged(page_table, pool, cw, n_pages):
    nb, pages_per_seq = page_table.shape
    nt = pages_per_seq // n_pages
    toks = n_pages * PAGE_SIZE
    rows = toks // D_CMP
    cst = lambda shape: pl.BlockSpec(shape, lambda b, t, pt: (0,) * len(shape))
    return pl.pallas_call(
        functools.partial(_compress_paged_kernel, n_pages=n_pages),
        grid_spec=pltpu.PrefetchScalarGridSpec(
            num_scalar_prefetch=1,
            grid=(nb, nt),
            in_specs=[pl.BlockSpec(memory_space=pl.ANY)] + _compress_specs(cst),
            out_specs=pl.BlockSpec((1, rows, 2 * KV_W), lambda b, t, pt: (b, t, 0)),
            scratch_shapes=[pltpu.VMEM((2, 2 * KV_W, toks), F32), pltpu.SemaphoreType.DMA((2,))]
            + _compress_scratch(n_pages)),
        out_shape=jax.ShapeDtypeStruct((nb, nt * rows, 2 * KV_W), F32),
        compiler_params=pltpu.CompilerParams(dimension_semantics=("arbitrary", "arbitrary"),
                                             vmem_limit_bytes=VMEM_LIMIT),
        name="compress_paged",
    )(page_table, pool, *cw)


def _compress_seq(x_t, cw):
    nb, _, toks = x_t.shape
    rows = toks // D_CMP
    return pl.pallas_call(
        _compress_seq_kernel,
        grid=(nb,),
        in_specs=[pl.BlockSpec((1, 2 * KV_W, toks), lambda b: (b, 0, 0))] + _compress_specs(_const_spec),
        out_specs=pl.BlockSpec((1, rows, 2 * KV_W), lambda b: (b, 0, 0)),
        out_shape=jax.ShapeDtypeStruct((nb, rows, 2 * KV_W), F32),
        scratch_shapes=_compress_scratch(toks // PAGE_SIZE),
        compiler_params=pltpu.CompilerParams(dimension_semantics=("parallel",), vmem_limit_bytes=VMEM_LIMIT),
        name="compress_seq",
    )(x_t, *cw)


def _compress_weights(cmp_pe, cmp_w1, cmp_b1, cmp_w2, cmp_b2):
    eye_g = jnp.eye(NSA_GROUPS, dtype=F32)
    w1 = cmp_w1.reshape(2, 2, D_CMP, HD, CMP_HID)
    w1j = jnp.einsum('kljdf,gG->kjgdlGf', w1, eye_g).reshape(2, D_CMP * KV_W, 2 * CMP_HALF)
    pe = cmp_pe.reshape(2, 2, D_CMP, HD)
    pe = jnp.broadcast_to(pe[:, :, :, None, :], (2, 2, D_CMP, NSA_GROUPS, HD))
    pej = jnp.pad(pe.reshape(2, 2, D_CMP * KV_W), ((0, 0), (0, 6), (0, 0)))
    b1big = jnp.broadcast_to(cmp_b1[:, None, :], (2, NSA_GROUPS, CMP_HID)).reshape(2, 1, CMP_HALF)
    w2big = jnp.einsum('kfd,gG->kgfGd', cmp_w2, eye_g).reshape(2, CMP_HALF, KV_W)
    b2big = jnp.broadcast_to(cmp_b2[:, None, :], (2, NSA_GROUPS, HD)).reshape(2, 1, KV_W)
    dst = np.arange(PAGE_SIZE)
    perm = (dst[:, None] % CHUNKS_PER_PAGE) * D_CMP + dst[:, None] // CHUNKS_PER_PAGE == dst[None, :]
    return jnp.asarray(perm, dtype=BF16), w1j.astype(BF16), pej, b1big, w2big.astype(BF16), b2big


def _overlap_matrix(n_rows, n_cols):
    c_start = (np.arange(n_rows) - 1) * D_CMP
    s_start = np.arange(n_cols) * L_SEL
    ov = (c_start[:, None] <= s_start[None, :] + L_SEL - 1) & (c_start[:, None] + L_CMP - 1 >= s_start[None, :])
    ov &= (np.arange(n_rows) >= 1)[:, None]
    return ov.astype(np.float32)


def _online(state, s, msk, v_bf, v_dims=NN):
    m, l, acc = state
    if msk is not None:
        s = jnp.where(msk, s, NEG_INF)
    m_new = jnp.maximum(m, jnp.max(s, axis=-1, keepdims=True))
    alpha = jnp.exp(m - m_new)
    p = jnp.exp(s - m_new)
    if msk is not None:
        p = jnp.where(msk, p, 0.0)
    l = alpha * l + jnp.sum(p, axis=-1, keepdims=True)
    acc = alpha * acc + _dg(p.astype(BF16), v_bf, v_dims)
    return m_new, l, acc


def _online_init(rows):
    return jnp.full((rows, 1), NEG_INF, F32), jnp.zeros((rows, 1), F32), jnp.zeros((rows, HD), F32)


def _online_out(state):
    _, l, acc = state
    return acc / jnp.maximum(l, 1e-30)


def _stack_heads(q, g):
    return jnp.concatenate([q[:, (HPG * g + h) * HD:(HPG * g + h + 1) * HD] for h in range(HPG)], axis=0) * (HD ** -0.5)


def _compressed_branch(qs, kc, vc, valid, n_tok):
    s = jnp.where(valid, _hdot(qs, kc, NT), NEG_INF)
    m = jnp.max(s, axis=-1, keepdims=True)
    p = jnp.where(valid, jnp.exp(s - m), 0.0)
    p = p / jnp.maximum(jnp.sum(p, axis=-1, keepdims=True), 1e-30)
    o = jnp.dot(p.astype(BF16), vc.astype(BF16), preferred_element_type=F32)
    psum = p[0:n_tok]
    for h in range(1, HPG):
        psum = psum + p[h * n_tok:(h + 1) * n_tok]
    return o, psum


def _select_blocks(imp, tpos, n_real):
    blk = lax.broadcasted_iota(jnp.int32, imp.shape, 1)
    cur = tpos // L_SEL
    forced = (blk == 0) | (blk == cur) | (blk == cur - 1)
    imp = jnp.where(forced, FORCE_SCORE, imp)
    imp = jnp.where(blk * L_SEL > tpos, -1.0, imp)
    imp = jnp.where(blk >= n_real, -2.0, imp)
    slabs = [imp[:, l0:l0 + LANES] for l0 in range(0, imp.shape[1], LANES)]
    cnt = [jnp.zeros(s.shape, F32) for s in slabs]
    lane = lax.broadcasted_iota(jnp.int32, slabs[0].shape, 1)
    for mcol in range(n_real):
        col = jnp.broadcast_to(imp[:, mcol:mcol + 1], slabs[0].shape)
        for si, slab in enumerate(slabs):
            ge = jnp.where(col >= slab, 1.0, 0.0)
            gt = jnp.where(col > slab, 1.0, 0.0)
            if si * LANES > mcol:
                ahead = ge
            elif si * LANES + LANES - 1 < mcol:
                ahead = gt
            else:
                ahead = jnp.where(lane > mcol - si * LANES, ge, gt)
            cnt[si] = cnt[si] + ahead
    return jnp.where(jnp.concatenate(cnt, axis=1) < N_SEL, 1.0, 0.0)


def _select_blocks_t(imp_t, tpos_row, n_real):
    sub = 8
    blk = lax.broadcasted_iota(jnp.int32, imp_t.shape, 0)
    cur = jnp.right_shift(tpos_row, int(np.log2(L_SEL)))
    forced = (blk == 0) | (blk == cur) | (blk == cur - 1)
    imp_t = jnp.where(forced, FORCE_SCORE, imp_t)
    imp_t = jnp.where(blk * L_SEL > tpos_row, -1.0, imp_t)
    slabs = [imp_t[s0:s0 + sub] for s0 in range(0, n_real, sub)]
    cnt = [jnp.zeros(s.shape, F32) for s in slabs]
    sub_iota = lax.broadcasted_iota(jnp.int32, slabs[0].shape, 0)
    for m in range(n_real):
        row = jnp.broadcast_to(imp_t[m:m + 1], slabs[0].shape)
        for si, slab in enumerate(slabs):
            ge = jnp.where(row >= slab, 1.0, 0.0)
            gt = jnp.where(row > slab, 1.0, 0.0)
            if si * sub > m:
                ahead = ge
            elif si * sub + sub - 1 < m:
                ahead = gt
            else:
                ahead = jnp.where(sub_iota > m - si * sub, ge, gt)
            cnt[si] = cnt[si] + ahead
    return jnp.where(jnp.concatenate(cnt, axis=0) < N_SEL, 1.0, 0.0)


ACC_ROWS = HD + 16


def _with_ones(v_t):
    return jnp.concatenate([v_t, jnp.ones((ACC_ROWS - HD, v_t.shape[1]), BF16)], axis=0)


def _sweep_t(units, states):
    states = list(states)
    s_t = {0: units[0][1]()}
    pending = None

    def finish(k, alpha, p, values):
        m, acc = states[k]
        states[k] = (m, alpha * acc + _dg(values(), p))

    for u, (k, _, msk, values) in enumerate(units):
        if u + 1 < len(units):
            s_t[u + 1] = units[u + 1][1]()
        m, acc = states[k]
        s = s_t.pop(u)
        if msk is not None:
            s = jnp.where(msk, s, NEG_INF)
        m_new = jnp.maximum(m, jnp.max(s, axis=0, keepdims=True))
        alpha = jnp.exp(m - m_new)
        p = jnp.exp(s - m_new)
        states[k] = (m_new, acc)
        if pending is not None:
            finish(*pending)
        pending = (k, alpha, p.astype(BF16), values)
    finish(*pending)
    return tuple(states)


def _sweep_t_init(cols):
    return jnp.full((1, cols), NEG_INF, F32), jnp.zeros((ACC_ROWS, cols), F32)


def _sweep_t_out(state):
    _, acc = state
    return acc[0:HD] / jnp.maximum(acc[HD:HD + 1], 1e-30)


def _nsa_prompt_kernel(qt_ref, gatet_ref, kc_ref, ksel_ref, vsel_ref, kwin_ref, vwin_ref, ovt_ref, o_ref, *, tq, tk):
    i = pl.program_id(1)
    n_cmp = kc_ref.shape[1]
    n_sb = ovt_ref.shape[0]
    cols = HPG * tq
    aug = LANES - HD
    t0 = i * tq
    tpos_row = t0 + lax.broadcasted_iota(jnp.int32, (1, tq), 1)
    tpos4 = jnp.concatenate([tpos_row] * HPG, axis=1)
    n_full = t0 // tk
    w_start = pl.multiple_of(jnp.maximum(t0 - WIN, 0), tq)
    gates_t = gatet_ref[0]
    groups = range(NSA_GROUPS)
    n_w = WIN + tq
    dist = tpos4 - (w_start + lax.broadcasted_iota(jnp.int32, (n_w, 1), 0))
    ridx = lax.broadcasted_iota(jnp.int32, (n_cmp, cols), 0)
    valid = (ridx >= 1) & (D_CMP * ridx + (L_CMP - D_CMP - 1) <= tpos4)
    q_t = [jnp.concatenate([qt_ref[0, (HPG * g + h) * HD:(HPG * g + h + 1) * HD, :] for h in range(HPG)], axis=1)
           * (HD ** -0.5) for g in groups]
    s_c = [jnp.where(valid, _hdot(kc_ref[0, :, g * HD:(g + 1) * HD], q_t[g]), NEG_INF) for g in groups]
    p_c = [jnp.where(valid, jnp.exp(s - jnp.max(s, axis=0, keepdims=True)), 0.0) for s in s_c]
    p_c = [p / jnp.maximum(jnp.sum(p, axis=0, keepdims=True), 1e-30) for p in p_c]
    o_c = [_dg(kc_ref[0, :, KV_W + g * HD:KV_W + (g + 1) * HD].astype(BF16), p_c[g].astype(BF16), TN) for g in groups]
    psum = jnp.concatenate([sum(p[:, h * tq:(h + 1) * tq] for h in range(HPG)) for p in p_c], axis=1)
    sel_t = _select_blocks_t(_hdot(ovt_ref[...], psum), jnp.concatenate([tpos_row] * NSA_GROUPS, axis=1), n_sb)
    qa = []
    for g in groups:
        bias = jnp.concatenate([(sel_t[:, g * tq:(g + 1) * tq] - 1.0) * (-NEG_INF)] * HPG, axis=1)
        if n_sb < aug:
            bias = jnp.concatenate([bias, jnp.zeros((aug - n_sb, cols), F32)], axis=0)
        qa.append(jnp.concatenate([q_t[g], bias], axis=0).astype(BF16))

    def sel_unit(g, j, causal):
        k0 = pl.multiple_of(j * tk, tk)
        kpos = k0 + lax.broadcasted_iota(jnp.int32, (tk, 1), 0)
        return (g, lambda: _dg(ksel_ref[0, g, pl.ds(k0, tk), :], qa[g]), (kpos <= tpos4) if causal else None,
                lambda: _with_ones(vsel_ref[0, g * HD:(g + 1) * HD, pl.ds(k0, tk)].astype(BF16)))

    def win_unit(g):
        return (NSA_GROUPS + g, lambda: _dg(kwin_ref[0, g, pl.ds(w_start, n_w), :], qa[g]),
                (dist >= 0) & (dist < WIN),
                lambda: _with_ones(vwin_ref[0, g * HD:(g + 1) * HD, pl.ds(w_start, n_w)].astype(BF16)))

    states = lax.fori_loop(
        0, n_full // 2,
        lambda jj, st: _sweep_t([sel_unit(g, 2 * jj + d, False) for d in range(2) for g in groups], st),
        tuple(_sweep_t_init(cols) for g in groups))
    last = lambda: [u for g in groups for u in (sel_unit(g, n_full, True), win_unit(g))]
    states = lax.cond(n_full % 2 == 1,
                      lambda st: _sweep_t([sel_unit(g, n_full - 1, False) for g in groups] + last(), st),
                      lambda st: _sweep_t(last(), st),
                      states + tuple(_sweep_t_init(cols) for g in groups))
    heads_out = []
    for g in groups:
        o_s = _sweep_t_out(states[g])
        o_w = _sweep_t_out(states[NSA_GROUPS + g])
        for h in range(HPG):
            hd = HPG * g + h
            c = slice(h * tq, (h + 1) * tq)
            heads_out.append(gates_t[3 * hd:3 * hd + 1] * o_c[g][:, c] + gates_t[3 * hd + 1:3 * hd + 2] * o_s[:, c]
                             + gates_t[3 * hd + 2:3 * hd + 3] * o_w[:, c])
    o_ref[0] = jnp.concatenate(heads_out, axis=0).T


def _nsa_prompt(q_t, gates_t, kc, ksel, sel_t, kwin, win_t, tq, tk):
    b, _, t = q_t.shape
    n_cmp = kc.shape[1]
    n_sb = t // L_SEL
    ovt = jnp.asarray(_overlap_matrix(n_cmp, n_sb).T)
    keys = pl.BlockSpec((1, NSA_GROUPS, t, LANES), lambda bi, i: (bi, 0, 0, 0))
    values = pl.BlockSpec((1, KV_W, t), lambda bi, i: (bi, 1, 0))
    return pl.pallas_call(
        functools.partial(_nsa_prompt_kernel, tq=tq, tk=tk),
        grid=(b, t // tq),
        in_specs=[pl.BlockSpec((1, W_A, tq), lambda bi, i: (bi, 0, i)),
                  pl.BlockSpec((1, LANES, tq), lambda bi, i: (bi, 0, i)),
                  pl.BlockSpec((1, n_cmp, 2 * KV_W), lambda bi, i: (bi, 0, 0)),
                  keys, values, keys, values, _const_spec((n_sb, n_cmp))],
        out_specs=pl.BlockSpec((1, tq, W_A), lambda bi, i: (bi, i, 0)),
        out_shape=jax.ShapeDtypeStruct((b, t, W_A), F32),
        compiler_params=pltpu.CompilerParams(dimension_semantics=("parallel", "arbitrary"),
                                             vmem_limit_bytes=VMEM_LIMIT),
        name="nsa_prompt",
    )(q_t, gates_t, kc, ksel, sel_t, kwin, win_t, ovt)


def _nsa_sample_kernel(pt_ref, q_ref, gate_ref, kc_ref, pool_ref, snew_ref, wcache_ref, wnew_ref, ov_ref, ex_ref,
                       o_ref, buf, sem, sel_scr, oc_scr, m_scr, l_scr, acc_scr, *, n_pages, past):
    slot = _paged_tile(pt_ref, pool_ref, buf, sem, n_pages)
    t = pl.program_id(1)
    nt = pl.num_programs(1)
    ds = q_ref.shape[1]
    rows = HPG * ds
    n_cmp = kc_ref.shape[1]
    n_sbp = ov_ref.shape[1]
    blocks_per_tile = n_pages * PAGE_SIZE // L_SEL
    n_past_blocks = past // L_SEL
    keys = n_pages * PAGE_SIZE
    q = q_ref[0]
    tok4 = lax.broadcasted_iota(jnp.int32, (rows, 1), 0) % ds

    @pl.when(t == 0)
    def _():
        tpos = past + lax.broadcasted_iota(jnp.int32, (ds, 1), 0)
        ridx = lax.broadcasted_iota(jnp.int32, (rows, n_cmp), 1)
        valid = (ridx >= 1) & (D_CMP * ridx + (L_CMP - D_CMP - 1) <= past + tok4)
        psum = []
        for g in range(NSA_GROUPS):
            o_c, ps = _compressed_branch(_stack_heads(q, g), kc_ref[0, :, g * HD:(g + 1) * HD],
                                         kc_ref[0, :, KV_W + g * HD:KV_W + (g + 1) * HD], valid, ds)
            oc_scr[g] = o_c
            psum.append(ps)
            m_scr[g], l_scr[g], acc_scr[g] = _online_init(rows)
        sel = _select_blocks(_hdot(jnp.concatenate(psum, axis=0), ov_ref[...]),
                             jnp.concatenate([tpos] * NSA_GROUPS, axis=0), n_past_blocks + 1)
        for g in range(NSA_GROUPS):
            sel4 = jnp.concatenate([sel[g * ds:(g + 1) * ds]] * HPG, axis=0)
            for tt in range(sel_scr.shape[0]):
                sel_scr[tt, g] = sel4[:, tt * LANES:(tt + 1) * LANES]

    kv_t = buf[slot]
    first_blk = t * blocks_per_tile
    ex = ex_ref[(first_blk % LANES) // blocks_per_tile]
    zero = jnp.zeros((rows, HD), F32)
    q_rows = jnp.concatenate([jnp.concatenate([_stack_heads(q, 0), zero], axis=1),
                              jnp.concatenate([zero, _stack_heads(q, 1)], axis=1)], axis=0)
    bias = jnp.concatenate([(sel_scr[first_blk // LANES, g] - 1.0) * (-NEG_INF) for g in range(NSA_GROUPS)], axis=0)
    s_all = _dg(jnp.concatenate([q_rows, bias], axis=1).astype(BF16),
                jnp.concatenate([kv_t[0:KV_W].astype(BF16), ex], axis=0))
    for g in range(NSA_GROUPS):
        vt = kv_t[KV_W + g * HD:KV_W + (g + 1) * HD].astype(BF16)
        m_scr[g], l_scr[g], acc_scr[g] = _online((m_scr[g], l_scr[g], acc_scr[g]), s_all[g * rows:(g + 1) * rows],
                                                 None, vt, NT)

    @pl.when(t == nt - 1)
    def _():
        gates = gate_ref[0]
        knew_pos = lax.broadcasted_iota(jnp.int32, (rows, ds), 1)
        snew = snew_ref[0]
        wnew = wnew_ref[0]
        wc_t = wcache_ref[0]
        n_wc = wc_t.shape[1]
        for g in range(NSA_GROUPS):
            qsb = _stack_heads(q, g).astype(BF16)
            kcol = slice(g * HD, (g + 1) * HD)
            vcol = slice(KV_W + g * HD, KV_W + (g + 1) * HD)
            sel_new = sel_scr[n_past_blocks // LANES, g][:, n_past_blocks % LANES:n_past_blocks % LANES + 1]
            st = _online((m_scr[g], l_scr[g], acc_scr[g]), _dg(qsb, snew[:, kcol].astype(BF16), NT),
                         (sel_new > 0.5) & (knew_pos <= tok4), snew[:, vcol].astype(BF16))
            o_s = _online_out(st)
            dist_c = (n_wc + tok4) - lax.broadcasted_iota(jnp.int32, (rows, n_wc), 1)
            st = _online(_online_init(rows), _dg(qsb, wc_t[kcol].astype(BF16)),
                         (dist_c >= 0) & (dist_c < WIN), wc_t[vcol].astype(BF16), NT)
            dist_n = tok4 - knew_pos
            st = _online(st, _dg(qsb, wnew[:, kcol].astype(BF16), NT), (dist_n >= 0) & (dist_n < WIN),
                         wnew[:, vcol].astype(BF16))
            o_w = _online_out(st)
            o_c = oc_scr[g]
            for h in range(HPG):
                hd = HPG * g + h
                r = slice(h * ds, (h + 1) * ds)
                o_ref[0, :, hd * HD:(hd + 1) * HD] = (gates[:, 3 * hd:3 * hd + 1] * o_c[r]
                                                      + gates[:, 3 * hd + 1:3 * hd + 2] * o_s[r]
                                                      + gates[:, 3 * hd + 2:3 * hd + 3] * o_w[r])


def _nsa_sample(page_table, q, gates, kc, pool, sel_new, win_cache, win_new, n_pages):
    nb, ds = q.shape[:2]
    pages_per_seq = page_table.shape[1]
    past = pages_per_seq * PAGE_SIZE
    nt = pages_per_seq // n_pages
    n_cmp = kc.shape[1]
    n_sb = past // L_SEL + 1
    n_sbp = -(-n_sb // LANES) * LANES
    blocks_per_tile = n_pages * PAGE_SIZE // L_SEL
    keys = n_pages * PAGE_SIZE
    rows = HPG * ds
    ov = jnp.asarray(_overlap_matrix(n_cmp, n_sbp) * (np.arange(n_sbp) < n_sb)[None, :])
    ex = (np.arange(LANES)[None, :, None]
          == (np.arange(LANES // blocks_per_tile)[:, None, None] * blocks_per_tile + np.arange(keys)[None, None, :] // L_SEL))
    ex = jnp.asarray(ex, dtype=BF16)
    seq = lambda r, w: pl.BlockSpec((1, r, w), lambda b, t, pt: (b, 0, 0))
    cst = lambda shape: pl.BlockSpec(shape, lambda b, t, pt: (0,) * len(shape))
    return pl.pallas_call(
        functools.partial(_nsa_sample_kernel, n_pages=n_pages, past=past),
        grid_spec=pltpu.PrefetchScalarGridSpec(
            num_scalar_prefetch=1,
            grid=(nb, nt),
            in_specs=[seq(ds, W_A), seq(ds, LANES), seq(n_cmp, 2 * KV_W), pl.BlockSpec(memory_space=pl.ANY),
                      seq(ds, 2 * KV_W), seq(2 * KV_W, win_cache.shape[2]), seq(ds, 2 * KV_W),
                      cst((n_cmp, n_sbp)), cst(ex.shape)],
            out_specs=seq(ds, W_A),
            scratch_shapes=[pltpu.VMEM((2, 2 * KV_W, keys), F32),
                            pltpu.SemaphoreType.DMA((2,)),
                            pltpu.VMEM((n_sbp // LANES, NSA_GROUPS, rows, LANES), F32),
                            pltpu.VMEM((NSA_GROUPS, rows, HD), F32),
                            pltpu.VMEM((NSA_GROUPS, rows, 1), F32),
                            pltpu.VMEM((NSA_GROUPS, rows, 1), F32),
                            pltpu.VMEM((NSA_GROUPS, rows, HD), F32)]),
        out_shape=jax.ShapeDtypeStruct((nb, ds, W_A), F32),
        compiler_params=pltpu.CompilerParams(dimension_semantics=("arbitrary", "arbitrary"),
                                             vmem_limit_bytes=VMEM_LIMIT),
        name="nsa_sample",
    )(page_table, q, gates, kc, pool, sel_new, win_cache, win_new, ov, ex)


def _sum_heads(x, ones_ref):
    hi, lo = _split(x)
    return _dg(hi, ones_ref[...]) + _dg(lo, ones_ref[...])


def _rwkv_prep_kernel(rz_ref, before_ref, shift_ref, mu_ref, w0_ref, wup_ref, a0_ref, aup_ref, kk_ref, ka_ref, rk_ref,
                      gnb_ref, ones_ref, a_ref, o_ref, y1_ref, y2_ref, yadd_ref, *, chunk, t_valid, separate):
    rows = rz_ref.shape[1]
    n_chunks = rows // chunk
    shift = int(np.log2(chunk))
    z = rz_ref[0]
    ridx = lax.broadcasted_iota(jnp.int32, (rows, 1), 0)
    prev = pltpu.roll(z, 1, 0)
    if separate:
        for c in range(n_chunks):
            prev = jnp.where(ridx == c * chunk, shift_ref[0, c:c + 1], prev)
    else:
        n_before = before_ref.shape[1]
        row_before = jnp.where(pl.program_id(1) == 0, shift_ref[0], before_ref[0, n_before - 1:n_before])
        prev = jnp.where(ridx == 0, row_before, prev)
    zs = z + (prev - z) * mu_ref[...]
    r = zs[:, 0:W_R]
    k = zs[:, W_R:2 * W_R]
    v = zs[:, 2 * W_R:3 * W_R]
    wd = zs[:, 3 * W_R:3 * W_R + R_W]
    ad = zs[:, 3 * W_R + R_W:C_R]
    wpre = w0_ref[...] + _hdot(jnp.tanh(wd), wup_ref[...])
    w = -(jnp.maximum(-wpre, 0.0) + jnp.log(1.0 + jnp.exp(-jnp.abs(wpre)))) - 0.5
    ld = -jnp.exp(w)
    a = _sigmoid(a0_ref[...] + _hdot(ad, aup_ref[...]))
    kkn = k * kk_ref[...]
    kk = kkn / jnp.maximum(jnp.sqrt(_sum_heads(kkn * kkn, ones_ref)), 1e-12)
    k2 = k * (1.0 + (a - 1.0) * ka_ref[...])
    yadd_ref[0] = _sum_heads(r * k2 * rk_ref[...], ones_ref) * v + gnb_ref[...]
    if t_valid < chunk:
        live = (ridx & (chunk - 1)) < t_valid
        ld = jnp.where(live, ld, 0.0)
        kk = jnp.where(live, kk, 0.0)
        k2 = jnp.where(live, k2, 0.0)
    ti = lax.broadcasted_iota(jnp.int32, (rows, rows), 0)
    tj = lax.broadcasted_iota(jnp.int32, (rows, rows), 1)
    tri = jnp.where((jnp.right_shift(ti, shift) == jnp.right_shift(tj, shift)) & (tj <= ti), 1.0, 0.0).astype(BF16)
    ld_h = ld.astype(BF16)
    ld_m = (ld - ld_h.astype(F32)).astype(BF16)
    ld_l = (ld - ld_h.astype(F32) - ld_m.astype(F32)).astype(BF16)
    cum = _dg(tri, ld_h) + (_dg(tri, ld_m) + _dg(tri, ld_l))
    cum_end = jnp.concatenate([jnp.broadcast_to(cum[(c + 1) * chunk - 1:(c + 1) * chunk], (chunk, W_R))
                               for c in range(n_chunks)], axis=0)
    p_inv = jnp.exp(-cum)
    p_end = jnp.exp(cum_end - cum)
    kka = kk * a
    at = -kk * jnp.exp(cum - ld)
    rt = r * jnp.exp(cum)
    at_b, rt_b, v_b = at.astype(BF16), rt.astype(BF16), v.astype(BF16)
    bt_b, kt_b = (kka * p_inv).astype(BF16), (k2 * p_inv).astype(BF16)
    be_b, ke_b = (kka * p_end).astype(BF16), (k2 * p_end).astype(BF16)
    dec_end = jnp.exp(cum_end)

    row = lax.broadcasted_iota(jnp.int32, (chunk, PAIR_W), 0)
    col = lax.broadcasted_iota(jnp.int32, (chunk, PAIR_W), 1) & (R_HD - 1)
    strict, incl = col < row, col <= row
    eye_pair = jnp.where(col == row, 1.0, 0.0)
    n_levels = max(1, int(np.ceil(np.log2(chunk))))
    units = [(c, p) for c in range(n_chunks) for p in range(R_HEADS // 2)]
    idx = range(len(units))
    cut = lambda x, u: x[u[0] * chunk:(u[0] + 1) * chunk, u[1] * PAIR_W:(u[1] + 1) * PAIR_W]
    mm = [_dg(jnp.concatenate([cut(at_b, u), cut(rt_b, u)], axis=0),
              jnp.concatenate([_pair_diag(cut(bt_b, u)), _pair_diag(cut(kt_b, u))], axis=0), NT) for u in units]
    low = [jnp.where(strict, m[0:chunk, 0:PAIR_W], 0.0) for m in mm]
    m_k = [jnp.concatenate([jnp.where(strict, m[0:chunk, PAIR_W:], 0.0), jnp.where(incl, m[chunk:, PAIR_W:], 0.0)],
                           axis=0).astype(BF16) for m in mm]
    m_rb = [jnp.where(incl, m[chunk:, 0:PAIR_W], 0.0).astype(BF16) for m in mm]
    mv = [_dg(m_k[i], _pair_diag(cut(v_b, u))) for i, u in enumerate(units)]
    inv = [eye_pair + x for x in low]
    pw = [x.astype(BF16) for x in low]
    for lvl in range(1, n_levels + 1):
        if lvl == 1:
            pw = [_dg(x, _pair_diag(x)).astype(BF16) for x in pw]
        elif lvl <= n_levels:
            last = lvl == n_levels
            nxt = [_dg(pw[i], _pair_diag(inv[i].astype(BF16)) if last else
                       jnp.concatenate([_pair_diag(inv[i].astype(BF16)), _pair_diag(pw[i])], axis=1)) for i in idx]
            inv = [inv[i] + nxt[i][:, 0:PAIR_W] for i in idx]
            if not last:
                pw = [nxt[i][:, PAIR_W:].astype(BF16) for i in idx]
    inv_b = [x.astype(BF16) for x in inv]
    w = [_dg(inv_b[i], jnp.concatenate([_pair_diag(cut(at_b, u)), _pair_diag(mv[i][0:chunk].astype(BF16))], axis=1))
         for i, u in enumerate(units)]
    w_b = [x.astype(BF16) for x in w]
    y12 = [_dg(m_rb[i], jnp.concatenate([_pair_diag(w_b[i][:, 0:PAIR_W]), _pair_diag(w_b[i][:, PAIR_W:])], axis=1))
           for i in idx]
    prow = lax.broadcasted_iota(jnp.int32, (PAIR_W, PAIR_W), 0)
    pcol = lax.broadcasted_iota(jnp.int32, (PAIR_W, PAIR_W), 1)
    same_head = (prow < R_HD) == (pcol < R_HD)
    col_first = lax.broadcasted_iota(jnp.int32, (R_HD, PAIR_W), 1) < R_HD
    for i, u in enumerate(units):
        c, p = u
        rs = slice(c * chunk, (c + 1) * chunk)
        ls = slice(p * PAIR_W, (p + 1) * PAIR_W)
        a_mat = jnp.where(same_head, _dg(w_b[i][:, 0:PAIR_W], cut(be_b, u), TN), 0.0)
        a_mat = a_mat + jnp.where(prow == pcol, jnp.broadcast_to(dec_end[rs, ls][0:1], (PAIR_W, PAIR_W)), 0.0)
        a_ref[0, c, p] = a_mat.astype(BF16)
        o_full = _dg(jnp.concatenate([w_b[i][:, PAIR_W:], cut(v_b, u)], axis=0),
                     jnp.concatenate([cut(be_b, u), cut(ke_b, u)], axis=0), TN)
        o_ref[0, c, p] = jnp.where(col_first, o_full[0:R_HD], o_full[R_HD:])
        y1_ref[0, rs, ls] = (y12[i][:, 0:PAIR_W] + cut(rt, u)).astype(BF16)
        y2_ref[0, rs, ls] = y12[i][:, PAIR_W:] + mv[i][chunk:]


PAIR_W = 2 * R_HD


def _pair_diag(x):
    first = jnp.where(lax.broadcasted_iota(jnp.int32, x.shape, 1) < R_HD, 1.0, 0.0).astype(x.dtype)
    return jnp.concatenate([x * first, x * (1 - first)], axis=0)


def _rwkv_scan_kernel(a_ref, o_ref, y1_ref, y2_ref, yadd_ref, s0_ref, gng_ref, ones_ref, y_ref, st, sout_ref):
    units = [(b, p) for b in range(st.shape[0]) for p in range(R_HEADS // 2)]

    @pl.when(pl.program_id(1) == 0)
    def _():
        for b, p in units:
            st[b, p] = jnp.concatenate([s0_ref[b, 2 * p], s0_ref[b, 2 * p + 1]], axis=1)

    lsl = lambda p: slice(p * PAIR_W, (p + 1) * PAIR_W)
    s_b = [st[b, p].astype(BF16) for b, p in units]
    ys = [_dg(y1_ref[b, :, lsl(p)], _pair_diag(s_b[i]), NT) + y2_ref[b, :, lsl(p)] for i, (b, p) in enumerate(units)]
    for i, (b, p) in enumerate(units):
        st[b, p] = _dg(s_b[i], a_ref[b, 0, p]) + o_ref[b, 0, p]
    head_mean = lambda x: _dg(x.astype(BF16), ones_ref[...]) * (1.0 / R_HD)
    mean = [head_mean(y) for y in ys]
    dev = [ys[i] - mean[i] for i in range(len(units))]
    var = [head_mean(d * d) for d in dev]
    for i, (b, p) in enumerate(units):
        y_ref[b, :, lsl(p)] = dev[i] * lax.rsqrt(var[i] + GN_EPS) * gng_ref[:, lsl(p)] + yadd_ref[b, :, lsl(p)]

    for b, p in units:
        s = st[b, p]
        sout_ref[b, 2 * p] = s[:, 0:R_HD]
        sout_ref[b, 2 * p + 1] = s[:, R_HD:]


def _rwkv(rz, shift0, state0, params, chunk, chunks_per_step, seqs_per_step, t_valid):
    mu, w0, wup, a0, aup, k_k, k_a, r_k, gn_g, gn_b = params
    b, t = rz.shape[:2]
    assert chunk == R_HD, "pair packing puts the chunk's time index on a head's 64 lanes"
    rows = chunk * chunks_per_step
    n_chunks = t // chunk
    n_pairs = R_HEADS // 2
    n_before = 8
    separate = t == chunk and chunks_per_step > 1
    pb, pt = (b // chunks_per_step, rows) if separate else (b, t)
    pc = pt // chunk
    ones = jnp.asarray(np.kron(np.eye(R_HEADS), np.ones((R_HD, R_HD))), dtype=BF16)
    vec = lambda n: _const_spec((1, n))
    rowsd = lambda dt: jax.ShapeDtypeStruct((b, t, W_R), dt)
    prowsd = lambda dt: jax.ShapeDtypeStruct((pb, pt, W_R), dt)
    row_spec = pl.BlockSpec((1, rows, W_R), lambda bi, c: (bi, c, 0))
    rz_p = rz.reshape(pb, pt, C_R)
    a_m, o_m, y1, y2, yadd = pl.pallas_call(
        functools.partial(_rwkv_prep_kernel, chunk=chunk, t_valid=t_valid, separate=separate),
        grid=(pb, pt // rows),
        in_specs=[pl.BlockSpec((1, rows, C_R), lambda bi, c: (bi, c, 0)),
                  pl.BlockSpec((1, n_before, C_R),
                               lambda bi, c: (bi, jnp.maximum(c * (rows // n_before) - 1, 0), 0)),
                  pl.BlockSpec((1, b // pb, C_R), lambda bi, c: (bi, 0, 0)),
                  vec(C_R), vec(W_R), _const_spec((R_W, W_R)), vec(W_R), _const_spec((R_A, W_R)),
                  vec(W_R), vec(W_R), vec(W_R), vec(W_R), _const_spec((W_R, W_R))],
        out_specs=[pl.BlockSpec((1, chunks_per_step, n_pairs, PAIR_W, PAIR_W), lambda bi, c: (bi, c, 0, 0, 0)),
                   pl.BlockSpec((1, chunks_per_step, n_pairs, R_HD, PAIR_W), lambda bi, c: (bi, c, 0, 0, 0)),
                   row_spec, row_spec, row_spec],
        out_shape=[jax.ShapeDtypeStruct((pb, pc, n_pairs, PAIR_W, PAIR_W), BF16),
                   jax.ShapeDtypeStruct((pb, pc, n_pairs, R_HD, PAIR_W), F32),
                   prowsd(BF16), prowsd(F32), prowsd(F32)],
        compiler_params=pltpu.CompilerParams(dimension_semantics=("parallel", "parallel"),
                                             vmem_limit_bytes=VMEM_LIMIT),
        name="rwkv_prep",
    )(rz_p, rz_p, shift0.reshape(pb, b // pb, C_R), mu, w0, wup, a0, aup, k_k, k_a, r_k, gn_b, ones)
    a_m = a_m.reshape(b, n_chunks, n_pairs, PAIR_W, PAIR_W)
    o_m = o_m.reshape(b, n_chunks, n_pairs, R_HD, PAIR_W)
    y1, y2, yadd = (u.reshape(b, t, W_R) for u in (y1, y2, yadd))
    nb = seqs_per_step
    row1 = pl.BlockSpec((nb, chunk, W_R), lambda bi, c: (bi, c, 0))
    pair_state = pl.BlockSpec((nb, n_pairs, R_HD, PAIR_W), lambda bi, c: (bi, 0, 0, 0))
    head_state = pl.BlockSpec((nb, R_HEADS, R_HD, R_HD), lambda bi, c: (bi, 0, 0, 0))
    y, _, state = pl.pallas_call(
        _rwkv_scan_kernel,
        grid=(b // nb, n_chunks),
        in_specs=[pl.BlockSpec((nb, 1, n_pairs, PAIR_W, PAIR_W), lambda bi, c: (bi, c, 0, 0, 0)),
                  pl.BlockSpec((nb, 1, n_pairs, R_HD, PAIR_W), lambda bi, c: (bi, c, 0, 0, 0)),
                  row1, row1, row1, head_state, vec(W_R), _const_spec((PAIR_W, PAIR_W))],
        out_specs=[row1, pair_state, head_state],
        out_shape=[rowsd(F32), jax.ShapeDtypeStruct((b, n_pairs, R_HD, PAIR_W), F32),
                   jax.ShapeDtypeStruct((b, R_HEADS, R_HD, R_HD), F32)],
        compiler_params=pltpu.CompilerParams(dimension_semantics=("parallel", "arbitrary"),
                                             vmem_limit_bytes=VMEM_LIMIT),
        name="rwkv_scan",
    )(a_m, o_m, y1, y2, yadd, state0, gn_g, ones[:PAIR_W, :PAIR_W])
    return y, state


def _rope_tables(pos):
    half = HD // 2
    inv = ROPE_THETA ** (-jnp.arange(half, dtype=F32) / half)
    ang = pos.astype(F32)[:, None] * inv[None, :]
    cos, sin = jnp.cos(ang), jnp.sin(ang)
    reps = LANES // HD
    return (jnp.tile(jnp.concatenate([cos, cos], axis=-1), (1, reps)),
            jnp.tile(jnp.concatenate([-sin, sin], axis=-1), (1, reps)))


def _pick_tile(n, cap):
    t = cap
    while n % t:
        t //= 2
    return t


def kernel(x_prompt, x_sample, p_prompt, p_sample, cache_cmp_kv, cache_sel_kv, cache_win_kv, state_wkv, state_shift, page_table, norm_g, w_in, cmp_pe, cmp_w1, cmp_b1, cmp_w2, cmp_b2, w_pa, rwkv_mu, rwkv_w0, rwkv_w_up, rwkv_a0, rwkv_a_up, rwkv_k_k, rwkv_k_a, rwkv_r_k, rwkv_gn_g, rwkv_gn_b, w_pb, w_out, ple_norm_g, w_ple_gate, w_ple_proj, final_norm_g):
    b, s_len = x_prompt.shape[:2]
    db, ds = x_sample.shape[:2]
    depth = norm_g.shape[0]
    assert depth == 1, "single-layer trunk"
    n_pages = page_table.shape[1]
    past = n_pages * PAGE_SIZE
    win_buf = cache_win_kv.shape[2]
    n_pool = cache_cmp_kv.shape[1]
    i = 0

    w = w_in[i]
    o_q, o_kv, o_ng, o_ag = 0, W_A, W_A + 6 * KV_W, W_A + 6 * KV_W + 3 * NSA_HEADS
    w_pack = (w[:, o_q:o_ng].astype(BF16), w[:, o_ag:].astype(BF16),
              jnp.pad(w[:, o_ng:o_ag], ((0, 0), (0, C_PACK - w.shape[1]))).astype(BF16))
    cw = _compress_weights(cmp_pe[i], cmp_w1[i], cmp_b1[i], cmp_w2[i], cmp_b2[i])
    row = lambda u: u.reshape(1, -1)
    rw = (row(rwkv_mu[i]), row(rwkv_w0[i]), rwkv_w_up[i], row(rwkv_a0[i]), rwkv_a_up[i], row(rwkv_k_k[i]),
          row(rwkv_k_a[i]), row(rwkv_r_k[i]), row(rwkv_gn_g[i]), row(rwkv_gn_b[i]))
    ow = (w_pa[i].astype(BF16), w_pb[i].astype(BF16), w_out[i].astype(BF16), row(ple_norm_g[i]),
          w_ple_gate[i].astype(BF16), w_ple_proj[i].astype(BF16), row(final_norm_g))

    tm = _pick_tile(s_len, 256)
    cos_p, sin_p = _rope_tables(jnp.arange(s_len, dtype=jnp.int32))
    xp2 = x_prompt.reshape(b * s_len, D_MODEL)
    q, cmp_p, sel_p, win_p, gates, sa, rz, sr, mg, ksel, kwin = _project(xp2, row(norm_g[i]), w_pack, cos_p, sin_p,
                                                                        tm, True)
    kc_p = _compress_seq(cmp_p, cw)
    seq = lambda u: u.reshape(b, s_len, u.shape[-1])
    o_a = _nsa_prompt(q, gates, kc_p, ksel, sel_p, kwin, win_p, tq=128, tk=min(512, s_len))
    chunk = 64
    o_b, st_p = _rwkv(seq(rz), jnp.zeros((b, C_R), F32), jnp.zeros((b, R_HEADS, R_HD, R_HD), F32), rw, chunk,
                      chunks_per_step=4, seqs_per_step=_pick_tile(b, 4), t_valid=chunk)
    y_p = _merge(xp2, o_a.reshape(b * s_len, W_A), o_b.reshape(b * s_len, W_R), sa, sr, mg,
                 p_prompt[i].reshape(b * s_len, D_PLE), *ow, tm=_pick_tile(b * s_len, 512))
    kv_shape = lambda u, n, t: u.reshape(n, t, 2, NSA_GROUPS, HD)
    from_cm = lambda u: jnp.transpose(u.reshape(u.shape[0], 2, NSA_GROUPS, HD, u.shape[2]), (0, 4, 1, 2, 3))
    to_cm = lambda u: jnp.transpose(u, (0, 2, 3, 4, 1)).reshape(u.shape[0], 2 * KV_W, u.shape[1])
    cmp_kv_p = from_cm(cmp_p)
    sel_kv_p = from_cm(sel_p)
    win_kv_p = from_cm(win_p[:, :, s_len - min(WIN, s_len):])
    shift_p = seq(rz)[:, -1]

    n_s = db * ds
    cos_s, sin_s = _rope_tables(jnp.tile(past + jnp.arange(ds, dtype=jnp.int32), db))
    xs2 = x_sample.reshape(n_s, D_MODEL)
    q, cmp_s, sel_s, win_s, gates, sa, rz, sr, mg = _project(xs2, row(norm_g[i]), w_pack, cos_s, sin_s, n_s, False)
    kc_s = _compress_paged(page_table, to_cm(cache_cmp_kv[i]), cw, n_pages=_pick_tile(n_pages, 64))
    seqs = lambda u: u.reshape(db, ds, u.shape[-1])
    o_a = _nsa_sample(page_table, seqs(q), seqs(gates), kc_s, to_cm(cache_sel_kv[i]),
                      seqs(sel_s), to_cm(cache_win_kv[i]), seqs(win_s), n_pages=_pick_tile(n_pages, 32))
    pad_t = chunk
    rz_pad = jnp.pad(seqs(rz), ((0, 0), (0, pad_t - ds), (0, 0)))
    o_b, st_s = _rwkv(rz_pad, state_shift[i], state_wkv[i], rw, pad_t, chunks_per_step=_pick_tile(db, 4),
                      seqs_per_step=_pick_tile(db, 4), t_valid=ds)
    y_s = _merge(xs2, o_a.reshape(n_s, W_A), o_b[:, :ds].reshape(n_s, W_R), sa, sr, mg,
                 p_sample[i].reshape(n_s, D_PLE), *ow, tm=n_s)
    win_kv_s = jnp.concatenate([cache_win_kv[i], kv_shape(win_s, db, ds)], axis=1)[:, ds:]
    shift_s = seqs(rz)[:, -1]

    return (y_p.reshape(b, s_len, D_MODEL), y_s.reshape(db, ds, D_MODEL),
            cmp_kv_p[None], kv_shape(cmp_s, db, ds)[None], sel_kv_p[None], kv_shape(sel_s, db, ds)[None],
            win_kv_p[None], win_kv_s[None], st_p[None], st_s[None], shift_p[None], shift_s[None])
```

```python
import functools

import numpy as np
import jax
import jax.numpy as jnp
from jax import lax
from jax.experimental import pallas as pl
from jax.experimental.pallas import tpu as pltpu

F32 = jnp.float32
BF16 = jnp.bfloat16

D_MODEL = 1024
D_PLE = 256
NSA_HEADS = 8
NSA_GROUPS = 2
HD = 64
HPG = NSA_HEADS // NSA_GROUPS
W_A = NSA_HEADS * HD
KV_W = NSA_GROUPS * HD
L_CMP = 32
D_CMP = 16
CMP_HID = 128
L_SEL = 64
N_SEL = 16
WIN = 512
FORCE_SCORE = 1e4
NEG_INF = -1e30
R_HEADS = 8
R_HD = 64
W_R = R_HEADS * R_HD
R_W = 64
R_A = 64
C_R = 3 * W_R + R_W + R_A
GN_EPS = 64e-5
ROPE_THETA = 10000.0
NORM_EPS = 1e-6
PAGE_SIZE = 128

LANES = 128
CHUNKS_PER_PAGE = PAGE_SIZE // D_CMP
VMEM_LIMIT = 56 * 1024 * 1024

C_Q = 0
C_KV = 512
C_AG = 1280
C_RZ = 1792
C_RG = 3456
C_MG = 3968
C_NG = 6016
C_PACK = 6144

NN = ((1,), (0,))
NT = ((1,), (1,))
TN = ((0,), (0,))


def _dg(a, b, dims=NN):
    return lax.dot_general(a, b, (dims, ((), ())), preferred_element_type=F32)


def _split(x):
    hi = x.astype(BF16)
    lo = (x - hi.astype(F32)).astype(BF16)
    return hi, lo


def _hdot_s(a, b, dims=NN):
    (ah, al), (bh, bl) = a, b
    return _dg(ah, bh, dims) + (_dg(ah, bl, dims) + _dg(al, bh, dims))


def _hdot(a, b, dims=NN):
    return _hdot_s(_split(a), _split(b), dims)


def _sigmoid(x):
    return 1.0 / (1.0 + jnp.exp(-x))


def _silu(x):
    return x * _sigmoid(x)


def _rms(x, g):
    return x * lax.rsqrt(jnp.mean(x * x, axis=-1, keepdims=True) + NORM_EPS) * g


def _const_spec(shape):
    zeros = (0,) * len(shape)
    return pl.BlockSpec(shape, lambda *_: zeros)


def _rope128(z, cos, sin, first):
    partner = jnp.where(first, pltpu.roll(z, LANES - HD // 2, 1), pltpu.roll(z, HD // 2, 1))
    return z * cos + partner * sin


def _proj_kernel(x_ref, g_ref, wa_ref, wb_ref, wc_ref, cos_ref, sin_ref,
                 q_ref, cmp_ref, sel_ref, win_ref, gate_ref, sa_ref, rz_ref, sr_ref, mg_ref, *aug_refs,
                 seq_major, n_pos_blocks):
    hb = _rms(x_ref[...], g_ref[...]).astype(BF16)
    cos = cos_ref[...]
    sin = sin_ref[...]
    tm = cos.shape[0]
    lane = lax.broadcasted_iota(jnp.int32, cos.shape, 1)
    first = (lane % HD) < (HD // 2)

    def mm(a, b):
        ref, off = (wa_ref, 0) if b <= C_AG else (wc_ref, C_NG) if a >= C_NG else (wb_ref, C_AG)
        return jnp.dot(hb, ref[:, a - off:b - off], preferred_element_type=F32)

    zq = mm(C_Q, C_Q + W_A)
    q = jnp.concatenate([_rope128(zq[:, c * LANES:(c + 1) * LANES], cos, sin, first) for c in range(W_A // LANES)],
                        axis=1)
    gates = _sigmoid(mm(C_NG, C_PACK))
    if seq_major:
        q_ref[0] = q.T
        gate_ref[0] = gates.T
    else:
        q_ref[...] = q
        gate_ref[...] = gates
    for i, ref in enumerate((cmp_ref, sel_ref, win_ref)):
        zkv = mm(C_KV + 2 * KV_W * i, C_KV + 2 * KV_W * (i + 1))
        k_rot = _rope128(zkv[:, 0:KV_W], cos, sin, first)
        if seq_major:
            ref[0] = jnp.concatenate([k_rot, zkv[:, KV_W:2 * KV_W]], axis=1).T
            if i > 0:
                aug_ref = aug_refs[i - 1]
                if i == 1:
                    t_in_seq = ((pl.program_id(0) % n_pos_blocks) * tm
                                + lax.broadcasted_iota(jnp.int32, cos.shape, 0))
                    tail = jnp.where(jnp.right_shift(t_in_seq, int(np.log2(L_SEL))) == lane - HD, 1.0, 0.0)
                else:
                    tail = jnp.zeros(cos.shape, F32)
                for g in range(NSA_GROUPS):
                    k_g = k_rot if g == 0 else pltpu.roll(k_rot, LANES - g * HD, 1)
                    aug_ref[0, g] = jnp.where(lane < HD, k_g, tail).astype(BF16)
        else:
            ref[:, 0:KV_W] = k_rot
            ref[:, KV_W:2 * KV_W] = zkv[:, KV_W:2 * KV_W]
    sa_ref[...] = _silu(mm(C_AG, C_AG + W_A))
    rz_ref[...] = mm(C_RZ, C_RZ + C_R)
    sr_ref[...] = _silu(mm(C_RG, C_RG + W_R))
    mg_ref[...] = _sigmoid(mm(C_MG, C_MG + 2 * D_MODEL))


def _project(x2d, norm_g, w_pack, cos_t, sin_t, tm, seq_major):
    n = x2d.shape[0]
    seq_len = cos_t.shape[0]
    n_pos_blocks = seq_len // tm
    n_seq = n // seq_len
    widths = (W_A, 2 * KV_W, 2 * KV_W, 2 * KV_W, LANES, W_A, C_R, W_R, 2 * D_MODEL)
    row = lambda w: pl.BlockSpec((tm, w), lambda i: (i, 0))
    pos = pl.BlockSpec((tm, LANES), lambda i: (i % n_pos_blocks, 0))
    out_specs = [row(w) for w in widths]
    out_shape = [jax.ShapeDtypeStruct((n, w), F32) for w in widths]
    in_specs = [row(D_MODEL), _const_spec((1, D_MODEL))] + [_const_spec(w.shape) for w in w_pack] + [pos, pos]
    args = [x2d, norm_g, *w_pack, cos_t, sin_t]
    if seq_major:
        assert seq_len // L_SEL <= LANES - HD, "selection-block one-hot must fit beside the key"
        for o in range(5):
            out_specs[o] = pl.BlockSpec((1, widths[o], tm), lambda i: (i // n_pos_blocks, 0, i % n_pos_blocks))
            out_shape[o] = jax.ShapeDtypeStruct((n_seq, widths[o], seq_len), F32)
        for _ in range(2):
            out_specs.append(pl.BlockSpec((1, NSA_GROUPS, tm, LANES),
                                          lambda i: (i // n_pos_blocks, 0, i % n_pos_blocks, 0)))
            out_shape.append(jax.ShapeDtypeStruct((n_seq, NSA_GROUPS, seq_len, LANES), BF16))
    return pl.pallas_call(
        functools.partial(_proj_kernel, seq_major=seq_major, n_pos_blocks=n_pos_blocks),
        grid=(n // tm,),
        in_specs=in_specs,
        out_specs=out_specs,
        out_shape=out_shape,
        compiler_params=pltpu.CompilerParams(dimension_semantics=("parallel",), vmem_limit_bytes=VMEM_LIMIT),
        name="project",
    )(*args)


def _merge_kernel(x_ref, oa_ref, ob_ref, sa_ref, sr_ref, mg_ref, p_ref,
                  wpa_ref, wpb_ref, wout_ref, pg_ref, wgate_ref, wproj_ref, fg_ref, y_ref):
    bdot = lambda a, w_ref: jnp.dot(a.astype(BF16), w_ref[...], preferred_element_type=F32)
    ya = bdot(oa_ref[...] * sa_ref[...], wpa_ref)
    yb = bdot(ob_ref[...] * sr_ref[...], wpb_ref)
    m = mg_ref[:, 0:D_MODEL] * ya + mg_ref[:, D_MODEL:2 * D_MODEL] * yb
    x2 = x_ref[...] + bdot(m, wout_ref)
    gate = _sigmoid(bdot(_rms(x2, pg_ref[...]), wgate_ref))
    x3 = x2 + gate * bdot(p_ref[...], wproj_ref)
    y_ref[...] = _rms(x3, fg_ref[...])


def _merge(x2d, oa, ob, sa, sr, mg, p2d, wpa, wpb, wout, pg, wgate, wproj, fg, tm):
    n = x2d.shape[0]
    row = lambda w: pl.BlockSpec((tm, w), lambda i: (i, 0))
    return pl.pallas_call(
        _merge_kernel,
        grid=(n // tm,),
        in_specs=[row(D_MODEL), row(W_A), row(W_R), row(W_A), row(W_R), row(2 * D_MODEL), row(D_PLE),
                  _const_spec((W_A, D_MODEL)), _const_spec((W_R, D_MODEL)), _const_spec((D_MODEL, D_MODEL)),
                  _const_spec((1, D_MODEL)), _const_spec((D_MODEL, D_MODEL)), _const_spec((D_PLE, D_MODEL)),
                  _const_spec((1, D_MODEL))],
        out_specs=row(D_MODEL),
        out_shape=jax.ShapeDtypeStruct((n, D_MODEL), F32),
        compiler_params=pltpu.CompilerParams(dimension_semantics=("parallel",), vmem_limit_bytes=VMEM_LIMIT),
        name="merge",
    )(x2d, oa, ob, sa, sr, mg, p2d, wpa, wpb, wout, pg, wgate, wproj, fg)


def _page_copies(pt_ref, pool_ref, buf, sem, b, t, slot, n_pages):
    return [pltpu.make_async_copy(pool_ref.at[pt_ref[b, t * n_pages + p]],
                                  buf.at[slot, :, pl.ds(p * PAGE_SIZE, PAGE_SIZE)], sem.at[slot])
            for p in range(n_pages)]


def _paged_tile(pt_ref, pool_ref, buf, sem, n_pages):
    b, t = pl.program_id(0), pl.program_id(1)
    nt = pl.num_programs(1)
    step = b * nt + t
    slot = step % 2

    @pl.when(step == 0)
    def _():
        for cp in _page_copies(pt_ref, pool_ref, buf, sem, b, t, slot, n_pages):
            cp.start()

    nxt = step + 1

    @pl.when(nxt < pl.num_programs(0) * nt)
    def _():
        for cp in _page_copies(pt_ref, pool_ref, buf, sem, nxt // nt, nxt % nt, 1 - slot, n_pages):
            cp.start()

    for cp in _page_copies(pt_ref, pool_ref, buf, sem, b, t, slot, n_pages):
        cp.wait()
    return slot


CMP_HALF = NSA_GROUPS * CMP_HID


def _compress_core(page_t, xs, perm_ref, w1_ref, pe_ref, b1_ref, w2_ref, b2_ref, carry, out_ref):
    n_pages = xs.shape[0]
    rows = n_pages * CHUNKS_PER_PAGE
    for p in range(n_pages):
        xp = _dg(perm_ref[...], page_t(p).astype(BF16), NT)
        xs[p] = xp.reshape(D_CMP, CHUNKS_PER_PAGE, 2 * KV_W)
    ridx = lax.broadcasted_iota(jnp.int32, (rows, CMP_HALF), 0)
    for kv in range(2):
        cs = slice(kv * KV_W, (kv + 1) * KV_W)
        x = jnp.concatenate([xs[:, j, :, cs].reshape(rows, KV_W) for j in range(D_CMP)], axis=1)
        x = jnp.concatenate([x, pe_ref[kv]], axis=0).astype(BF16)
        z = jnp.dot(x, w1_ref[kv], preferred_element_type=F32)
        pos = z[rows:rows + 1, 0:CMP_HALF] + z[rows + 1:rows + 2, CMP_HALF:]
        lo = z[0:rows, 0:CMP_HALF]
        lo_prev = jnp.where(ridx == 0, carry[kv], pltpu.roll(lo, 1, 0))
        carry[kv] = lo[rows - 1:rows]
        hid = lo_prev + z[0:rows, CMP_HALF:] + pos + b1_ref[kv]
        out_ref[0, :, cs] = jnp.dot(_silu(hid).astype(BF16), w2_ref[kv], preferred_element_type=F32) + b2_ref[kv]


def _compress_paged_kernel(pt_ref, pool_ref, perm_ref, w1_ref, pe_ref, b1_ref, w2_ref, b2_ref, out_ref,
                           buf, sem, xs, carry, *, n_pages):
    slot = _paged_tile(pt_ref, pool_ref, buf, sem, n_pages)

    @pl.when(pl.program_id(1) == 0)
    def _():
        carry[...] = jnp.zeros_like(carry)

    _compress_core(lambda p: buf[slot, :, p * PAGE_SIZE:(p + 1) * PAGE_SIZE], xs, perm_ref, w1_ref, pe_ref, b1_ref,
                   w2_ref, b2_ref, carry, out_ref)


def _compress_seq_kernel(x_ref, perm_ref, w1_ref, pe_ref, b1_ref, w2_ref, b2_ref, out_ref, xs, carry):
    carry[...] = jnp.zeros_like(carry)
    _compress_core(lambda p: x_ref[0, :, p * PAGE_SIZE:(p + 1) * PAGE_SIZE], xs, perm_ref, w1_ref, pe_ref, b1_ref,
                   w2_ref, b2_ref, carry, out_ref)


def _compress_specs(cst):
    return [cst((PAGE_SIZE, PAGE_SIZE)), cst((2, D_CMP * KV_W, 2 * CMP_HALF)), cst((2, 8, D_CMP * KV_W)),
            cst((2, 1, CMP_HALF)), cst((2, CMP_HALF, KV_W)), cst((2, 1, KV_W))]


def _compress_scratch(n_pages):
    return [pltpu.VMEM((n_pages, D_CMP, CHUNKS_PER_PAGE, 2 * KV_W), F32), pltpu.VMEM((2, 1, CMP_HALF), F32)]


def _compress_paged(page_table, pool, cw, n_pages):
    nb, pages_per_seq = page_table.shape
    nt = pages_per_seq // n_pages
    toks = n_pages * PAGE_SIZE
    rows = toks // D_CMP
    cst = lambda shape: pl.BlockSpec(shape, lambda b, t, pt: (0,) * len(shape))
    return pl.pallas_call(
        functools.partial(_compress_paged_kernel, n_pages=n_pages),
        grid_spec=pltpu.PrefetchScalarGridSpec(
            num_scalar_prefetch=1,
            grid=(nb, nt),
            in_specs=[pl.BlockSpec(memory_space=pl.ANY)] + _compress_specs(cst),
            out_specs=pl.BlockSpec((1, rows, 2 * KV_W), lambda b, t, pt: (b, t, 0)),
            scratch_shapes=[pltpu.VMEM((2, 2 * KV_W, toks), F32), pltpu.SemaphoreType.DMA((2,))]
            + _compress_scratch(n_pages)),
        out_shape=jax.ShapeDtypeStruct((nb, nt * rows, 2 * KV_W), F32),
        compiler_params=pltpu.CompilerParams(dimension_semantics=("arbitrary", "arbitrary"),
                                             vmem_limit_bytes=VMEM_LIMIT),
        name="compress_paged",
    )(page_table, pool, *cw)


def _compress_seq(x_t, cw):
    nb, _, toks = x_t.shape
    rows = toks // D_CMP
    return pl.pallas_call(
        _compress_seq_kernel,
        grid=(nb,),
        in_specs=[pl.BlockSpec((1, 2 * KV_W, toks), lambda b: (b, 0, 0))] + _compress_specs(_const_spec),
        out_specs=pl.BlockSpec((1, rows, 2 * KV_W), lambda b: (b, 0, 0)),
        out_shape=jax.ShapeDtypeStruct((nb, rows, 2 * KV_W), F32),
        scratch_shapes=_compress_scratch(toks // PAGE_SIZE),
        compiler_params=pltpu.CompilerParams(dimension_semantics=("parallel",), vmem_limit_bytes=VMEM_LIMIT),
        name="compress_seq",
    )(x_t, *cw)


def _compress_weights(cmp_pe, cmp_w1, cmp_b1, cmp_w2, cmp_b2):
    eye_g = jnp.eye(NSA_GROUPS, dtype=F32)
    w1 = cmp_w1.reshape(2, 2, D_CMP, HD, CMP_HID)
    w1j = jnp.einsum('kljdf,gG->kjgdlGf', w1, eye_g).reshape(2, D_CMP * KV_W, 2 * CMP_HALF)
    pe = cmp_pe.reshape(2, 2, D_CMP, HD)
    pe = jnp.broadcast_to(pe[:, :, :, None, :], (2, 2, D_CMP, NSA_GROUPS, HD))
    pej = jnp.pad(pe.reshape(2, 2, D_CMP * KV_W), ((0, 0), (0, 6), (0, 0)))
    b1big = jnp.broadcast_to(cmp_b1[:, None, :], (2, NSA_GROUPS, CMP_HID)).reshape(2, 1, CMP_HALF)
    w2big = jnp.einsum('kfd,gG->kgfGd', cmp_w2, eye_g).reshape(2, CMP_HALF, KV_W)
    b2big = jnp.broadcast_to(cmp_b2[:, None, :], (2, NSA_GROUPS, HD)).reshape(2, 1, KV_W)
    dst = np.arange(PAGE_SIZE)
    perm = (dst[:, None] % CHUNKS_PER_PAGE) * D_CMP + dst[:, None] // CHUNKS_PER_PAGE == dst[None, :]
    return jnp.asarray(perm, dtype=BF16), w1j.astype(BF16), pej, b1big, w2big.astype(BF16), b2big


def _overlap_matrix(n_rows, n_cols):
    c_start = (np.arange(n_rows) - 1) * D_CMP
    s_start = np.arange(n_cols) * L_SEL
    ov = (c_start[:, None] <= s_start[None, :] + L_SEL - 1) & (c_start[:, None] + L_CMP - 1 >= s_start[None, :])
    ov &= (np.arange(n_rows) >= 1)[:, None]
    return ov.astype(np.float32)


def _online(state, s, msk, v_bf, v_dims=NN):
    m, l, acc = state
    if msk is not None:
        s = jnp.where(msk, s, NEG_INF)
    m_new = jnp.maximum(m, jnp.max(s, axis=-1, keepdims=True))
    alpha = jnp.exp(m - m_new)
    p = jnp.exp(s - m_new)
    if msk is not None:
        p = jnp.where(msk, p, 0.0)
    l = alpha * l + jnp.sum(p, axis=-1, keepdims=True)
    acc = alpha * acc + _dg(p.astype(BF16), v_bf, v_dims)
    return m_new, l, acc


def _online_init(rows):
    return jnp.full((rows, 1), NEG_INF, F32), jnp.zeros((rows, 1), F32), jnp.zeros((rows, HD), F32)


def _online_out(state):
    _, l, acc = state
    return acc / jnp.maximum(l, 1e-30)


def _stack_heads(q, g):
    return jnp.concatenate([q[:, (HPG * g + h) * HD:(HPG * g + h + 1) * HD] for h in range(HPG)], axis=0) * (HD ** -0.5)


def _compressed_branch(qs, kc, vc, valid, n_tok):
    s = jnp.where(valid, _hdot(qs, kc, NT), NEG_INF)
    m = jnp.max(s, axis=-1, keepdims=True)
    p = jnp.where(valid, jnp.exp(s - m), 0.0)
    p = p / jnp.maximum(jnp.sum(p, axis=-1, keepdims=True), 1e-30)
    o = jnp.dot(p.astype(BF16), vc.astype(BF16), preferred_element_type=F32)
    psum = p[0:n_tok]
    for h in range(1, HPG):
        psum = psum + p[h * n_tok:(h + 1) * n_tok]
    return o, psum


def _select_blocks(imp, tpos, n_real):
    blk = lax.broadcasted_iota(jnp.int32, imp.shape, 1)
    cur = tpos // L_SEL
    forced = (blk == 0) | (blk == cur) | (blk == cur - 1)
    imp = jnp.where(forced, FORCE_SCORE, imp)
    imp = jnp.where(blk * L_SEL > tpos, -1.0, imp)
    imp = jnp.where(blk >= n_real, -2.0, imp)
    slabs = [imp[:, l0:l0 + LANES] for l0 in range(0, imp.shape[1], LANES)]
    cnt = [jnp.zeros(s.shape, F32) for s in slabs]
    lane = lax.broadcasted_iota(jnp.int32, slabs[0].shape, 1)
    for mcol in range(n_real):
        col = jnp.broadcast_to(imp[:, mcol:mcol + 1], slabs[0].shape)
        for si, slab in enumerate(slabs):
            ge = jnp.where(col >= slab, 1.0, 0.0)
            gt = jnp.where(col > slab, 1.0, 0.0)
            if si * LANES > mcol:
                ahead = ge
            elif si * LANES + LANES - 1 < mcol:
                ahead = gt
            else:
                ahead = jnp.where(lane > mcol - si * LANES, ge, gt)
            cnt[si] = cnt[si] + ahead
    return jnp.where(jnp.concatenate(cnt, axis=1) < N_SEL, 1.0, 0.0)


def _select_blocks_t(imp_t, tpos_row, n_real, n_live, cnt_ref):
    sub = 8
    blk = lax.broadcasted_iota(jnp.int32, imp_t.shape, 0)
    cur = jnp.right_shift(tpos_row, int(np.log2(L_SEL)))
    forced = (blk == 0) | (blk == cur) | (blk == cur - 1)
    imp_t = jnp.where(forced, FORCE_SCORE, imp_t)
    imp_t = jnp.where(blk * L_SEL > tpos_row, -1.0, imp_t)
    slabs = [imp_t[s0:s0 + sub] for s0 in range(0, n_real, sub)]
    sub_iota = lax.broadcasted_iota(jnp.int32, slabs[0].shape, 0)
    cnt_ref[...] = jnp.zeros(cnt_ref.shape, F32)
    for m0 in range(0, n_real, sub):

        @pl.when(m0 < n_live)
        def _():
            cnt = [cnt_ref[si * sub:(si + 1) * sub] for si in range(len(slabs))]
            for m in range(m0, min(m0 + sub, n_real)):
                row = jnp.broadcast_to(imp_t[m:m + 1], slabs[0].shape)
                for si, slab in enumerate(slabs):
                    ge = jnp.where(row >= slab, 1.0, 0.0)
                    gt = jnp.where(row > slab, 1.0, 0.0)
                    if si * sub > m:
                        ahead = ge
                    elif si * sub + sub - 1 < m:
                        ahead = gt
                    else:
                        ahead = jnp.where(sub_iota > m - si * sub, ge, gt)
                    cnt[si] = cnt[si] + ahead
            for si in range(len(slabs)):
                cnt_ref[si * sub:(si + 1) * sub] = cnt[si]

    return jnp.where(cnt_ref[...] < N_SEL, 1.0, 0.0)


ACC_ROWS = HD + 16


def _with_ones(v_t):
    return jnp.concatenate([v_t, jnp.ones((ACC_ROWS - HD, v_t.shape[1]), BF16)], axis=0)


def _sweep_t(units, states):
    states = list(states)
    s_t = {0: units[0][1]()}
    pending = None

    def finish(k, alpha, p, values):
        m, acc = states[k]
        states[k] = (m, alpha * acc + _dg(values(), p))

    for u, (k, _, msk, values) in enumerate(units):
        if u + 1 < len(units):
            s_t[u + 1] = units[u + 1][1]()
        m, acc = states[k]
        s = s_t.pop(u)
        if msk is not None:
            s = jnp.where(msk, s, NEG_INF)
        m_new = jnp.maximum(m, jnp.max(s, axis=0, keepdims=True))
        alpha = jnp.exp(m - m_new)
        p = jnp.exp(s - m_new)
        states[k] = (m_new, acc)
        if pending is not None:
            finish(*pending)
        pending = (k, alpha, p.astype(BF16), values)
    finish(*pending)
    return tuple(states)


def _sweep_t_init(cols):
    return jnp.full((1, cols), NEG_INF, F32), jnp.zeros((ACC_ROWS, cols), F32)


def _sweep_t_out(state):
    _, acc = state
    return acc[0:HD] / jnp.maximum(acc[HD:HD + 1], 1e-30)


def _nsa_prompt_kernel(qt_ref, gatet_ref, kc_ref, ksel_ref, vsel_ref, kwin_ref, vwin_ref, ovt_ref, o_ref, cnt_ref,
                       *, tq, tk):
    i = pl.program_id(1)
    n_cmp = kc_ref.shape[1]
    n_sb = ovt_ref.shape[0]
    cols = HPG * tq
    aug = LANES - HD
    t0 = i * tq
    tpos_row = t0 + lax.broadcasted_iota(jnp.int32, (1, tq), 1)
    tpos4 = jnp.concatenate([tpos_row] * HPG, axis=1)
    n_full = t0 // tk
    w_start = pl.multiple_of(jnp.maximum(t0 - WIN, 0), tq)
    gates_t = gatet_ref[0]
    groups = range(NSA_GROUPS)
    n_w = WIN + tq
    dist = tpos4 - (w_start + lax.broadcasted_iota(jnp.int32, (n_w, 1), 0))
    ridx = lax.broadcasted_iota(jnp.int32, (n_cmp, cols), 0)
    valid = (ridx >= 1) & (D_CMP * ridx + (L_CMP - D_CMP - 1) <= tpos4)
    q_t = [jnp.concatenate([qt_ref[0, (HPG * g + h) * HD:(HPG * g + h + 1) * HD, :] for h in range(HPG)], axis=1)
           * (HD ** -0.5) for g in groups]
    s_c = [jnp.where(valid, _hdot(kc_ref[0, :, g * HD:(g + 1) * HD], q_t[g]), NEG_INF) for g in groups]
    p_c = [jnp.where(valid, jnp.exp(s - jnp.max(s, axis=0, keepdims=True)), 0.0) for s in s_c]
    p_c = [p / jnp.maximum(jnp.sum(p, axis=0, keepdims=True), 1e-30) for p in p_c]
    o_c = [_dg(kc_ref[0, :, KV_W + g * HD:KV_W + (g + 1) * HD].astype(BF16), p_c[g].astype(BF16), TN) for g in groups]
    psum = jnp.concatenate([sum(p[:, h * tq:(h + 1) * tq] for h in range(HPG)) for p in p_c], axis=1)
    sel_t = _select_blocks_t(_hdot(ovt_ref[...], psum), jnp.concatenate([tpos_row] * NSA_GROUPS, axis=1), n_sb,
                             (t0 + tq - 1) // L_SEL + 1, cnt_ref)
    qa = []
    for g in groups:
        bias = jnp.concatenate([(sel_t[:, g * tq:(g + 1) * tq] - 1.0) * (-NEG_INF)] * HPG, axis=1)
        if n_sb < aug:
            bias = jnp.concatenate([bias, jnp.zeros((aug - n_sb, cols), F32)], axis=0)
        qa.append(jnp.concatenate([q_t[g], bias], axis=0).astype(BF16))

    def sel_unit(g, j, causal):
        k0 = pl.multiple_of(j * tk, tk)
        kpos = k0 + lax.broadcasted_iota(jnp.int32, (tk, 1), 0)
        return (g, lambda: _dg(ksel_ref[0, g, pl.ds(k0, tk), :], qa[g]), (kpos <= tpos4) if causal else None,
                lambda: _with_ones(vsel_ref[0, g * HD:(g + 1) * HD, pl.ds(k0, tk)].astype(BF16)))

    def win_unit(g):
        return (NSA_GROUPS + g, lambda: _dg(kwin_ref[0, g, pl.ds(w_start, n_w), :], qa[g]),
                (dist >= 0) & (dist < WIN),
                lambda: _with_ones(vwin_ref[0, g * HD:(g + 1) * HD, pl.ds(w_start, n_w)].astype(BF16)))

    states = lax.fori_loop(
        0, n_full // 2,
        lambda jj, st: _sweep_t([sel_unit(g, 2 * jj + d, False) for d in range(2) for g in groups], st),
        tuple(_sweep_t_init(cols) for g in groups))
    last = lambda: [u for g in groups for u in (sel_unit(g, n_full, True), win_unit(g))]
    states = lax.cond(n_full % 2 == 1,
                      lambda st: _sweep_t([sel_unit(g, n_full - 1, False) for g in groups] + last(), st),
                      lambda st: _sweep_t(last(), st),
                      states + tuple(_sweep_t_init(cols) for g in groups))
    heads_out = []
    for g in groups:
        o_s = _sweep_t_out(states[g])
        o_w = _sweep_t_out(states[NSA_GROUPS + g])
        for h in range(HPG):
            hd = HPG * g + h
            c = slice(h * tq, (h + 1) * tq)
            heads_out.append(gates_t[3 * hd:3 * hd + 1] * o_c[g][:, c] + gates_t[3 * hd + 1:3 * hd + 2] * o_s[:, c]
                             + gates_t[3 * hd + 2:3 * hd + 3] * o_w[:, c])
    o_ref[0] = jnp.concatenate(heads_out, axis=0).T


def _nsa_prompt(q_t, gates_t, kc, ksel, sel_t, kwin, win_t, tq, tk):
    b, _, t = q_t.shape
    n_cmp = kc.shape[1]
    n_sb = t // L_SEL
    ovt = jnp.asarray(_overlap_matrix(n_cmp, n_sb).T)
    keys = pl.BlockSpec((1, NSA_GROUPS, t, LANES), lambda bi, i: (bi, 0, 0, 0))
    values = pl.BlockSpec((1, KV_W, t), lambda bi, i: (bi, 1, 0))
    return pl.pallas_call(
        functools.partial(_nsa_prompt_kernel, tq=tq, tk=tk),
        grid=(b, t // tq),
        in_specs=[pl.BlockSpec((1, W_A, tq), lambda bi, i: (bi, 0, i)),
                  pl.BlockSpec((1, LANES, tq), lambda bi, i: (bi, 0, i)),
                  pl.BlockSpec((1, n_cmp, 2 * KV_W), lambda bi, i: (bi, 0, 0)),
                  keys, values, keys, values, _const_spec((n_sb, n_cmp))],
        out_specs=pl.BlockSpec((1, tq, W_A), lambda bi, i: (bi, i, 0)),
        out_shape=jax.ShapeDtypeStruct((b, t, W_A), F32),
        scratch_shapes=[pltpu.VMEM((n_sb, NSA_GROUPS * tq), F32)],
        compiler_params=pltpu.CompilerParams(dimension_semantics=("parallel", "arbitrary"),
                                             vmem_limit_bytes=VMEM_LIMIT),
        name="nsa_prompt",
    )(q_t, gates_t, kc, ksel, sel_t, kwin, win_t, ovt)


def _nsa_sample_kernel(pt_ref, q_ref, gate_ref, kc_ref, pool_ref, snew_ref, wcache_ref, wnew_ref, ov_ref, ex_ref,
                       o_ref, buf, sem, sel_scr, oc_scr, m_scr, l_scr, acc_scr, *, n_pages, past):
    slot = _paged_tile(pt_ref, pool_ref, buf, sem, n_pages)
    t = pl.program_id(1)
    nt = pl.num_programs(1)
    ds = q_ref.shape[1]
    rows = HPG * ds
    n_cmp = kc_ref.shape[1]
    n_sbp = ov_ref.shape[1]
    blocks_per_tile = n_pages * PAGE_SIZE // L_SEL
    n_past_blocks = past // L_SEL
    keys = n_pages * PAGE_SIZE
    q = q_ref[0]
    tok4 = lax.broadcasted_iota(jnp.int32, (rows, 1), 0) % ds

    @pl.when(t == 0)
    def _():
        tpos = past + lax.broadcasted_iota(jnp.int32, (ds, 1), 0)
        ridx = lax.broadcasted_iota(jnp.int32, (rows, n_cmp), 1)
        valid = (ridx >= 1) & (D_CMP * ridx + (L_CMP - D_CMP - 1) <= past + tok4)
        psum = []
        for g in range(NSA_GROUPS):
            o_c, ps = _compressed_branch(_stack_heads(q, g), kc_ref[0, :, g * HD:(g + 1) * HD],
                                         kc_ref[0, :, KV_W + g * HD:KV_W + (g + 1) * HD], valid, ds)
            oc_scr[g] = o_c
            psum.append(ps)
            m_scr[g], l_scr[g], acc_scr[g] = _online_init(rows)
        sel = _select_blocks(_hdot(jnp.concatenate(psum, axis=0), ov_ref[...]),
                             jnp.concatenate([tpos] * NSA_GROUPS, axis=0), n_past_blocks + 1)
        for g in range(NSA_GROUPS):
            sel4 = jnp.concatenate([sel[g * ds:(g + 1) * ds]] * HPG, axis=0)
            for tt in range(sel_scr.shape[0]):
                sel_scr[tt, g] = sel4[:, tt * LANES:(tt + 1) * LANES]

    kv_t = buf[slot]
    first_blk = t * blocks_per_tile
    ex = ex_ref[(first_blk % LANES) // blocks_per_tile]
    zero = jnp.zeros((rows, HD), F32)
    q_rows = jnp.concatenate([jnp.concatenate([_stack_heads(q, 0), zero], axis=1),
                              jnp.concatenate([zero, _stack_heads(q, 1)], axis=1)], axis=0)
    bias = jnp.concatenate([(sel_scr[first_blk // LANES, g] - 1.0) * (-NEG_INF) for g in range(NSA_GROUPS)], axis=0)
    s_all = _dg(jnp.concatenate([q_rows, bias], axis=1).astype(BF16),
                jnp.concatenate([kv_t[0:KV_W].astype(BF16), ex], axis=0))
    for g in range(NSA_GROUPS):
        vt = kv_t[KV_W + g * HD:KV_W + (g + 1) * HD].astype(BF16)
        m_scr[g], l_scr[g], acc_scr[g] = _online((m_scr[g], l_scr[g], acc_scr[g]), s_all[g * rows:(g + 1) * rows],
                                                 None, vt, NT)

    @pl.when(t == nt - 1)
    def _():
        gates = gate_ref[0]
        knew_pos = lax.broadcasted_iota(jnp.int32, (rows, ds), 1)
        snew = snew_ref[0]
        wnew = wnew_ref[0]
        wc_t = wcache_ref[0]
        n_wc = wc_t.shape[1]
        for g in range(NSA_GROUPS):
            qsb = _stack_heads(q, g).astype(BF16)
            kcol = slice(g * HD, (g + 1) * HD)
            vcol = slice(KV_W + g * HD, KV_W + (g + 1) * HD)
            sel_new = sel_scr[n_past_blocks // LANES, g][:, n_past_blocks % LANES:n_past_blocks % LANES + 1]
            st = _online((m_scr[g], l_scr[g], acc_scr[g]), _dg(qsb, snew[:, kcol].astype(BF16), NT),
                         (sel_new > 0.5) & (knew_pos <= tok4), snew[:, vcol].astype(BF16))
            o_s = _online_out(st)
            dist_c = (n_wc + tok4) - lax.broadcasted_iota(jnp.int32, (rows, n_wc), 1)
            st = _online(_online_init(rows), _dg(qsb, wc_t[kcol].astype(BF16)),
                         (dist_c >= 0) & (dist_c < WIN), wc_t[vcol].astype(BF16), NT)
            dist_n = tok4 - knew_pos
            st = _online(st, _dg(qsb, wnew[:, kcol].astype(BF16), NT), (dist_n >= 0) & (dist_n < WIN),
                         wnew[:, vcol].astype(BF16))
            o_w = _online_out(st)
            o_c = oc_scr[g]
            for h in range(HPG):
                hd = HPG * g + h
                r = slice(h * ds, (h + 1) * ds)
                o_ref[0, :, hd * HD:(hd + 1) * HD] = (gates[:, 3 * hd:3 * hd + 1] * o_c[r]
                                                      + gates[:, 3 * hd + 1:3 * hd + 2] * o_s[r]
                                                      + gates[:, 3 * hd + 2:3 * hd + 3] * o_w[r])


def _nsa_sample(page_table, q, gates, kc, pool, sel_new, win_cache, win_new, n_pages):
    nb, ds = q.shape[:2]
    pages_per_seq = page_table.shape[1]
    past = pages_per_seq * PAGE_SIZE
    nt = pages_per_seq // n_pages
    n_cmp = kc.shape[1]
    n_sb = past // L_SEL + 1
    n_sbp = -(-n_sb // LANES) * LANES
    blocks_per_tile = n_pages * PAGE_SIZE // L_SEL
    keys = n_pages * PAGE_SIZE
    rows = HPG * ds
    ov = jnp.asarray(_overlap_matrix(n_cmp, n_sbp) * (np.arange(n_sbp) < n_sb)[None, :])
    ex = (np.arange(LANES)[None, :, None]
          == (np.arange(LANES // blocks_per_tile)[:, None, None] * blocks_per_tile + np.arange(keys)[None, None, :] // L_SEL))
    ex = jnp.asarray(ex, dtype=BF16)
    seq = lambda r, w: pl.BlockSpec((1, r, w), lambda b, t, pt: (b, 0, 0))
    cst = lambda shape: pl.BlockSpec(shape, lambda b, t, pt: (0,) * len(shape))
    return pl.pallas_call(
        functools.partial(_nsa_sample_kernel, n_pages=n_pages, past=past),
        grid_spec=pltpu.PrefetchScalarGridSpec(
            num_scalar_prefetch=1,
            grid=(nb, nt),
            in_specs=[seq(ds, W_A), seq(ds, LANES), seq(n_cmp, 2 * KV_W), pl.BlockSpec(memory_space=pl.ANY),
                      seq(ds, 2 * KV_W), seq(2 * KV_W, win_cache.shape[2]), seq(ds, 2 * KV_W),
                      cst((n_cmp, n_sbp)), cst(ex.shape)],
            out_specs=seq(ds, W_A),
            scratch_shapes=[pltpu.VMEM((2, 2 * KV_W, keys), F32),
                            pltpu.SemaphoreType.DMA((2,)),
                            pltpu.VMEM((n_sbp // LANES, NSA_GROUPS, rows, LANES), F32),
                            pltpu.VMEM((NSA_GROUPS, rows, HD), F32),
                            pltpu.VMEM((NSA_GROUPS, rows, 1), F32),
                            pltpu.VMEM((NSA_GROUPS, rows, 1), F32),
                            pltpu.VMEM((NSA_GROUPS, rows, HD), F32)]),
        out_shape=jax.ShapeDtypeStruct((nb, ds, W_A), F32),
        compiler_params=pltpu.CompilerParams(dimension_semantics=("arbitrary", "arbitrary"),
                                             vmem_limit_bytes=VMEM_LIMIT),
        name="nsa_sample",
    )(page_table, q, gates, kc, pool, sel_new, win_cache, win_new, ov, ex)


def _sum_heads(x, ones_ref):
    hi, lo = _split(x)
    return _dg(hi, ones_ref[...]) + _dg(lo, ones_ref[...])


def _rwkv_prep_kernel(rz_ref, before_ref, shift_ref, mu_ref, w0_ref, wup_ref, a0_ref, aup_ref, kk_ref, ka_ref, rk_ref,
                      gnb_ref, ones_ref, a_ref, o_ref, y1_ref, y2_ref, yadd_ref, *, chunk, t_valid, separate):
    rows = rz_ref.shape[1]
    n_chunks = rows // chunk
    shift = int(np.log2(chunk))
    z = rz_ref[0]
    ridx = lax.broadcasted_iota(jnp.int32, (rows, 1), 0)
    prev = pltpu.roll(z, 1, 0)
    if separate:
        for c in range(n_chunks):
            prev = jnp.where(ridx == c * chunk, shift_ref[0, c:c + 1], prev)
    else:
        n_before = before_ref.shape[1]
        row_before = jnp.where(pl.program_id(1) == 0, shift_ref[0], before_ref[0, n_before - 1:n_before])
        prev = jnp.where(ridx == 0, row_before, prev)
    zs = z + (prev - z) * mu_ref[...]
    r = zs[:, 0:W_R]
    k = zs[:, W_R:2 * W_R]
    v = zs[:, 2 * W_R:3 * W_R]
    wd = zs[:, 3 * W_R:3 * W_R + R_W]
    ad = zs[:, 3 * W_R + R_W:C_R]
    wpre = w0_ref[...] + _hdot(jnp.tanh(wd), wup_ref[...])
    w = -(jnp.maximum(-wpre, 0.0) + jnp.log(1.0 + jnp.exp(-jnp.abs(wpre)))) - 0.5
    ld = -jnp.exp(w)
    a = _sigmoid(a0_ref[...] + _hdot(ad, aup_ref[...]))
    kkn = k * kk_ref[...]
    kk = kkn / jnp.maximum(jnp.sqrt(_sum_heads(kkn * kkn, ones_ref)), 1e-12)
    k2 = k * (1.0 + (a - 1.0) * ka_ref[...])
    yadd_ref[0] = _sum_heads(r * k2 * rk_ref[...], ones_ref) * v + gnb_ref[...]
    if t_valid < chunk:
        live = (ridx & (chunk - 1)) < t_valid
        ld = jnp.where(live, ld, 0.0)
        kk = jnp.where(live, kk, 0.0)
        k2 = jnp.where(live, k2, 0.0)
    ti = lax.broadcasted_iota(jnp.int32, (rows, rows), 0)
    tj = lax.broadcasted_iota(jnp.int32, (rows, rows), 1)
    tri = jnp.where((jnp.right_shift(ti, shift) == jnp.right_shift(tj, shift)) & (tj <= ti), 1.0, 0.0).astype(BF16)
    ld_h = ld.astype(BF16)
    ld_m = (ld - ld_h.astype(F32)).astype(BF16)
    ld_l = (ld - ld_h.astype(F32) - ld_m.astype(F32)).astype(BF16)
    cum = _dg(tri, ld_h) + (_dg(tri, ld_m) + _dg(tri, ld_l))
    cum_end = jnp.concatenate([jnp.broadcast_to(cum[(c + 1) * chunk - 1:(c + 1) * chunk], (chunk, W_R))
                               for c in range(n_chunks)], axis=0)
    p_inv = jnp.exp(-cum)
    p_end = jnp.exp(cum_end - cum)
    kka = kk * a
    at = -kk * jnp.exp(cum - ld)
    rt = r * jnp.exp(cum)
    at_b, rt_b, v_b = at.astype(BF16), rt.astype(BF16), v.astype(BF16)
    bt_b, kt_b = (kka * p_inv).astype(BF16), (k2 * p_inv).astype(BF16)
    be_b, ke_b = (kka * p_end).astype(BF16), (k2 * p_end).astype(BF16)
    dec_end = jnp.exp(cum_end)

    row = lax.broadcasted_iota(jnp.int32, (chunk, PAIR_W), 0)
    col = lax.broadcasted_iota(jnp.int32, (chunk, PAIR_W), 1) & (R_HD - 1)
    strict, incl = col < row, col <= row
    eye_pair = jnp.where(col == row, 1.0, 0.0)
    n_levels = max(1, int(np.ceil(np.log2(chunk))))
    units = [(c, p) for c in range(n_chunks) for p in range(R_HEADS // 2)]
    idx = range(len(units))
    cut = lambda x, u: x[u[0] * chunk:(u[0] + 1) * chunk, u[1] * PAIR_W:(u[1] + 1) * PAIR_W]
    mm = [_dg(jnp.concatenate([cut(at_b, u), cut(rt_b, u)], axis=0),
              jnp.concatenate([_pair_diag(cut(bt_b, u)), _pair_diag(cut(kt_b, u))], axis=0), NT) for u in units]
    low = [jnp.where(strict, m[0:chunk, 0:PAIR_W], 0.0) for m in mm]
    m_k = [jnp.concatenate([jnp.where(strict, m[0:chunk, PAIR_W:], 0.0), jnp.where(incl, m[chunk:, PAIR_W:], 0.0)],
                           axis=0).astype(BF16) for m in mm]
    m_rb = [jnp.where(incl, m[chunk:, 0:PAIR_W], 0.0).astype(BF16) for m in mm]
    mv = [_dg(m_k[i], _pair_diag(cut(v_b, u))) for i, u in enumerate(units)]
    inv = [eye_pair + x for x in low]
    pw = [x.astype(BF16) for x in low]
    for lvl in range(1, n_levels + 1):
        if lvl == 1:
            pw = [_dg(x, _pair_diag(x)).astype(BF16) for x in pw]
        elif lvl <= n_levels:
            last = lvl == n_levels
            nxt = [_dg(pw[i], _pair_diag(inv[i].astype(BF16)) if last else
                       jnp.concatenate([_pair_diag(inv[i].astype(BF16)), _pair_diag(pw[i])], axis=1)) for i in idx]
            inv = [inv[i] + nxt[i][:, 0:PAIR_W] for i in idx]
            if not last:
                pw = [nxt[i][:, PAIR_W:].astype(BF16) for i in idx]
    inv_b = [x.astype(BF16) for x in inv]
    w = [_dg(inv_b[i], jnp.concatenate([_pair_diag(cut(at_b, u)), _pair_diag(mv[i][0:chunk].astype(BF16))], axis=1))
         for i, u in enumerate(units)]
    w_b = [x.astype(BF16) for x in w]
    y12 = [_dg(m_rb[i], jnp.concatenate([_pair_diag(w_b[i][:, 0:PAIR_W]), _pair_diag(w_b[i][:, PAIR_W:])], axis=1))
           for i in idx]
    prow = lax.broadcasted_iota(jnp.int32, (PAIR_W, PAIR_W), 0)
    pcol = lax.broadcasted_iota(jnp.int32, (PAIR_W, PAIR_W), 1)
    same_head = (prow < R_HD) == (pcol < R_HD)
    col_first = lax.broadcasted_iota(jnp.int32, (R_HD, PAIR_W), 1) < R_HD
    for i, u in enumerate(units):
        c, p = u
        rs = slice(c * chunk, (c + 1) * chunk)
        ls = slice(p * PAIR_W, (p + 1) * PAIR_W)
        a_mat = jnp.where(same_head, _dg(w_b[i][:, 0:PAIR_W], cut(be_b, u), TN), 0.0)
        a_mat = a_mat + jnp.where(prow == pcol, jnp.broadcast_to(dec_end[rs, ls][0:1], (PAIR_W, PAIR_W)), 0.0)
        a_ref[0, c, p] = a_mat.astype(BF16)
        o_full = _dg(jnp.concatenate([w_b[i][:, PAIR_W:], cut(v_b, u)], axis=0),
                     jnp.concatenate([cut(be_b, u), cut(ke_b, u)], axis=0), TN)
        o_ref[0, c, p] = jnp.where(col_first, o_full[0:R_HD], o_full[R_HD:])
        y1_ref[0, rs, ls] = (y12[i][:, 0:PAIR_W] + cut(rt, u)).astype(BF16)
        y2_ref[0, rs, ls] = y12[i][:, PAIR_W:] + mv[i][chunk:]


PAIR_W = 2 * R_HD


def _pair_diag(x):
    first = jnp.where(lax.broadcasted_iota(jnp.int32, x.shape, 1) < R_HD, 1.0, 0.0).astype(x.dtype)
    return jnp.concatenate([x * first, x * (1 - first)], axis=0)


def _rwkv_scan_kernel(a_ref, o_ref, y1_ref, y2_ref, yadd_ref, s0_ref, gng_ref, ones_ref, y_ref, st, sout_ref):
    units = [(b, p) for b in range(st.shape[0]) for p in range(R_HEADS // 2)]

    @pl.when(pl.program_id(1) == 0)
    def _():
        for b, p in units:
            st[b, p] = jnp.concatenate([s0_ref[b, 2 * p], s0_ref[b, 2 * p + 1]], axis=1)

    lsl = lambda p: slice(p * PAIR_W, (p + 1) * PAIR_W)
    s_b = [st[b, p].astype(BF16) for b, p in units]
    ys = [_dg(y1_ref[b, :, lsl(p)], _pair_diag(s_b[i]), NT) + y2_ref[b, :, lsl(p)] for i, (b, p) in enumerate(units)]
    for i, (b, p) in enumerate(units):
        st[b, p] = _dg(s_b[i], a_ref[b, 0, p]) + o_ref[b, 0, p]
    head_mean = lambda x: _dg(x.astype(BF16), ones_ref[...]) * (1.0 / R_HD)
    mean = [head_mean(y) for y in ys]
    dev = [ys[i] - mean[i] for i in range(len(units))]
    var = [head_mean(d * d) for d in dev]
    for i, (b, p) in enumerate(units):
        y_ref[b, :, lsl(p)] = dev[i] * lax.rsqrt(var[i] + GN_EPS) * gng_ref[:, lsl(p)] + yadd_ref[b, :, lsl(p)]

    for b, p in units:
        s = st[b, p]
        sout_ref[b, 2 * p] = s[:, 0:R_HD]
        sout_ref[b, 2 * p + 1] = s[:, R_HD:]


def _rwkv(rz, shift0, state0, params, chunk, chunks_per_step, seqs_per_step, t_valid):
    mu, w0, wup, a0, aup, k_k, k_a, r_k, gn_g, gn_b = params
    b, t = rz.shape[:2]
    assert chunk == R_HD, "pair packing puts the chunk's time index on a head's 64 lanes"
    rows = chunk * chunks_per_step
    n_chunks = t // chunk
    n_pairs = R_HEADS // 2
    n_before = 8
    separate = t == chunk and chunks_per_step > 1
    pb, pt = (b // chunks_per_step, rows) if separate else (b, t)
    pc = pt // chunk
    ones = jnp.asarray(np.kron(np.eye(R_HEADS), np.ones((R_HD, R_HD))), dtype=BF16)
    vec = lambda n: _const_spec((1, n))
    rowsd = lambda dt: jax.ShapeDtypeStruct((b, t, W_R), dt)
    prowsd = lambda dt: jax.ShapeDtypeStruct((pb, pt, W_R), dt)
    row_spec = pl.BlockSpec((1, rows, W_R), lambda bi, c: (bi, c, 0))
    rz_p = rz.reshape(pb, pt, C_R)
    a_m, o_m, y1, y2, yadd = pl.pallas_call(
        functools.partial(_rwkv_prep_kernel, chunk=chunk, t_valid=t_valid, separate=separate),
        grid=(pb, pt // rows),
        in_specs=[pl.BlockSpec((1, rows, C_R), lambda bi, c: (bi, c, 0)),
                  pl.BlockSpec((1, n_before, C_R),
                               lambda bi, c: (bi, jnp.maximum(c * (rows // n_before) - 1, 0), 0)),
                  pl.BlockSpec((1, b // pb, C_R), lambda bi, c: (bi, 0, 0)),
                  vec(C_R), vec(W_R), _const_spec((R_W, W_R)), vec(W_R), _const_spec((R_A, W_R)),
                  vec(W_R), vec(W_R), vec(W_R), vec(W_R), _const_spec((W_R, W_R))],
        out_specs=[pl.BlockSpec((1, chunks_per_step, n_pairs, PAIR_W, PAIR_W), lambda bi, c: (bi, c, 0, 0, 0)),
                   pl.BlockSpec((1, chunks_per_step, n_pairs, R_HD, PAIR_W), lambda bi, c: (bi, c, 0, 0, 0)),
                   row_spec, row_spec, row_spec],
        out_shape=[jax.ShapeDtypeStruct((pb, pc, n_pairs, PAIR_W, PAIR_W), BF16),
                   jax.ShapeDtypeStruct((pb, pc, n_pairs, R_HD, PAIR_W), F32),
                   prowsd(BF16), prowsd(F32), prowsd(F32)],
        compiler_params=pltpu.CompilerParams(dimension_semantics=("parallel", "parallel"),
                                             vmem_limit_bytes=VMEM_LIMIT),
        name="rwkv_prep",
    )(rz_p, rz_p, shift0.reshape(pb, b // pb, C_R), mu, w0, wup, a0, aup, k_k, k_a, r_k, gn_b, ones)
    a_m = a_m.reshape(b, n_chunks, n_pairs, PAIR_W, PAIR_W)
    o_m = o_m.reshape(b, n_chunks, n_pairs, R_HD, PAIR_W)
    y1, y2, yadd = (u.reshape(b, t, W_R) for u in (y1, y2, yadd))
    nb = seqs_per_step
    row1 = pl.BlockSpec((nb, chunk, W_R), lambda bi, c: (bi, c, 0))
    pair_state = pl.BlockSpec((nb, n_pairs, R_HD, PAIR_W), lambda bi, c: (bi, 0, 0, 0))
    head_state = pl.BlockSpec((nb, R_HEADS, R_HD, R_HD), lambda bi, c: (bi, 0, 0, 0))
    y, _, state = pl.pallas_call(
        _rwkv_scan_kernel,
        grid=(b // nb, n_chunks),
        in_specs=[pl.BlockSpec((nb, 1, n_pairs, PAIR_W, PAIR_W), lambda bi, c: (bi, c, 0, 0, 0)),
                  pl.BlockSpec((nb, 1, n_pairs, R_HD, PAIR_W), lambda bi, c: (bi, c, 0, 0, 0)),
                  row1, row1, row1, head_state, vec(W_R), _const_spec((PAIR_W, PAIR_W))],
        out_specs=[row1, pair_state, head_state],
        out_shape=[rowsd(F32), jax.ShapeDtypeStruct((b, n_pairs, R_HD, PAIR_W), F32),
                   jax.ShapeDtypeStruct((b, R_HEADS, R_HD, R_HD), F32)],
        compiler_params=pltpu.CompilerParams(dimension_semantics=("parallel", "arbitrary"),
                                             vmem_limit_bytes=VMEM_LIMIT),
        name="rwkv_scan",
    )(a_m, o_m, y1, y2, yadd, state0, gn_g, ones[:PAIR_W, :PAIR_W])
    return y, state


def _rope_tables(pos):
    half = HD // 2
    inv = ROPE_THETA ** (-jnp.arange(half, dtype=F32) / half)
    ang = pos.astype(F32)[:, None] * inv[None, :]
    cos, sin = jnp.cos(ang), jnp.sin(ang)
    reps = LANES // HD
    return (jnp.tile(jnp.concatenate([cos, cos], axis=-1), (1, reps)),
            jnp.tile(jnp.concatenate([-sin, sin], axis=-1), (1, reps)))


def _pick_tile(n, cap):
    t = cap
    while n % t:
        t //= 2
    return t


def kernel(x_prompt, x_sample, p_prompt, p_sample, cache_cmp_kv, cache_sel_kv, cache_win_kv, state_wkv, state_shift, page_table, norm_g, w_in, cmp_pe, cmp_w1, cmp_b1, cmp_w2, cmp_b2, w_pa, rwkv_mu, rwkv_w0, rwkv_w_up, rwkv_a0, rwkv_a_up, rwkv_k_k, rwkv_k_a, rwkv_r_k, rwkv_gn_g, rwkv_gn_b, w_pb, w_out, ple_norm_g, w_ple_gate, w_ple_proj, final_norm_g):
    b, s_len = x_prompt.shape[:2]
    db, ds = x_sample.shape[:2]
    depth = norm_g.shape[0]
    assert depth == 1, "single-layer trunk"
    n_pages = page_table.shape[1]
    past = n_pages * PAGE_SIZE
    i = 0

    w = w_in[i]
    o_q, o_ng, o_ag = 0, W_A + 6 * KV_W, W_A + 6 * KV_W + 3 * NSA_HEADS
    w_pack = (w[:, o_q:o_ng].astype(BF16), w[:, o_ag:].astype(BF16),
              jnp.pad(w[:, o_ng:o_ag], ((0, 0), (0, C_PACK - w.shape[1]))).astype(BF16))
    cw = _compress_weights(cmp_pe[i], cmp_w1[i], cmp_b1[i], cmp_w2[i], cmp_b2[i])
    row = lambda u: u.reshape(1, -1)
    rw = (row(rwkv_mu[i]), row(rwkv_w0[i]), rwkv_w_up[i], row(rwkv_a0[i]), rwkv_a_up[i], row(rwkv_k_k[i]),
          row(rwkv_k_a[i]), row(rwkv_r_k[i]), row(rwkv_gn_g[i]), row(rwkv_gn_b[i]))
    ow = (w_pa[i].astype(BF16), w_pb[i].astype(BF16), w_out[i].astype(BF16), row(ple_norm_g[i]),
          w_ple_gate[i].astype(BF16), w_ple_proj[i].astype(BF16), row(final_norm_g))

    tm = _pick_tile(s_len, 256)
    cos_p, sin_p = _rope_tables(jnp.arange(s_len, dtype=jnp.int32))
    xp2 = x_prompt.reshape(b * s_len, D_MODEL)
    q, cmp_p, sel_p, win_p, gates, sa, rz, sr, mg, ksel, kwin = _project(xp2, row(norm_g[i]), w_pack, cos_p, sin_p,
                                                                        tm, True)
    kc_p = _compress_seq(cmp_p, cw)
    seq = lambda u: u.reshape(b, s_len, u.shape[-1])
    o_a = _nsa_prompt(q, gates, kc_p, ksel, sel_p, kwin, win_p, tq=128, tk=min(512, s_len))
    chunk = 64
    o_b, st_p = _rwkv(seq(rz), jnp.zeros((b, C_R), F32), jnp.zeros((b, R_HEADS, R_HD, R_HD), F32), rw, chunk,
                      chunks_per_step=4, seqs_per_step=_pick_tile(b, 4), t_valid=chunk)
    y_p = _merge(xp2, o_a.reshape(b * s_len, W_A), o_b.reshape(b * s_len, W_R), sa, sr, mg,
                 p_prompt[i].reshape(b * s_len, D_PLE), *ow, tm=_pick_tile(b * s_len, 512))
    kv_shape = lambda u, n, t: u.reshape(n, t, 2, NSA_GROUPS, HD)
    from_cm = lambda u: jnp.transpose(u.reshape(u.shape[0], 2, NSA_GROUPS, HD, u.shape[2]), (0, 4, 1, 2, 3))
    to_cm = lambda u: jnp.transpose(u, (0, 2, 3, 4, 1)).reshape(u.shape[0], 2 * KV_W, u.shape[1])
    cmp_kv_p = from_cm(cmp_p)
    sel_kv_p = from_cm(sel_p)
    win_kv_p = from_cm(win_p[:, :, s_len - min(WIN, s_len):])
    shift_p = seq(rz)[:, -1]

    n_s = db * ds
    cos_s, sin_s = _rope_tables(jnp.tile(past + jnp.arange(ds, dtype=jnp.int32), db))
    xs2 = x_sample.reshape(n_s, D_MODEL)
    q, cmp_s, sel_s, win_s, gates, sa, rz, sr, mg = _project(xs2, row(norm_g[i]), w_pack, cos_s, sin_s, n_s, False)
    kc_s = _compress_paged(page_table, to_cm(cache_cmp_kv[i]), cw, n_pages=_pick_tile(n_pages, 64))
    seqs = lambda u: u.reshape(db, ds, u.shape[-1])
    o_a = _nsa_sample(page_table, seqs(q), seqs(gates), kc_s, to_cm(cache_sel_kv[i]),
                      seqs(sel_s), to_cm(cache_win_kv[i]), seqs(win_s), n_pages=_pick_tile(n_pages, 64))
    pad_t = chunk
    rz_pad = jnp.pad(seqs(rz), ((0, 0), (0, pad_t - ds), (0, 0)))
    o_b, st_s = _rwkv(rz_pad, state_shift[i], state_wkv[i], rw, pad_t, chunks_per_step=_pick_tile(db, 4),
                      seqs_per_step=_pick_tile(db, 4), t_valid=ds)
    y_s = _merge(xs2, o_a.reshape(n_s, W_A), o_b[:, :ds].reshape(n_s, W_R), sa, sr, mg,
                 p_sample[i].reshape(n_s, D_PLE), *ow, tm=n_s)
    win_kv_s = jnp.concatenate([cache_win_kv[i], kv_shape(win_s, db, ds)], axis=1)[:, ds:]
    shift_s = seqs(rz)[:, -1]

    return (y_p.reshape(b, s_len, D_MODEL), y_s.reshape(db, ds, D_MODEL),
            cmp_kv_p[None], kv_shape(cmp_s, db, ds)[None], sel_kv_p[None], kv_shape(sel_s, db, ds)[None],
            win_kv_p[None], win_kv_s[None], st_p[None], st_s[None], shift_p[None], shift_s[None])
```

```python
import functools

import numpy as np
import jax
import jax.numpy as jnp
from jax import lax
from jax.experimental import pallas as pl
from jax.experimental.pallas import tpu as pltpu

F32 = jnp.float32
BF16 = jnp.bfloat16

D_MODEL = 1024
D_PLE = 256
NSA_HEADS = 8
NSA_GROUPS = 2
HD = 64
HPG = NSA_HEADS // NSA_GROUPS
W_A = NSA_HEADS * HD
KV_W = NSA_GROUPS * HD
L_CMP = 32
D_CMP = 16
CMP_HID = 128
L_SEL = 64
N_SEL = 16
WIN = 512
FORCE_SCORE = 1e4
NEG_INF = -1e30
R_HEADS = 8
R_HD = 64
W_R = R_HEADS * R_HD
R_W = 64
R_A = 64
C_R = 3 * W_R + R_W + R_A
GN_EPS = 64e-5
ROPE_THETA = 10000.0
NORM_EPS = 1e-6
PAGE_SIZE = 128

LANES = 128
CHUNKS_PER_PAGE = PAGE_SIZE // D_CMP
VMEM_LIMIT = 56 * 1024 * 1024

C_Q = 0
C_KV = 512
C_AG = 1280
C_RZ = 1792
C_RG = 3456
C_MG = 3968
C_NG = 6016
C_PACK = 6144

NN = ((1,), (0,))
NT = ((1,), (1,))
TN = ((0,), (0,))


def _dg(a, b, dims=NN):
    return lax.dot_general(a, b, (dims, ((), ())), preferred_element_type=F32)


def _split(x):
    hi = x.astype(BF16)
    lo = (x - hi.astype(F32)).astype(BF16)
    return hi, lo


def _hdot_s(a, b, dims=NN):
    (ah, al), (bh, bl) = a, b
    return _dg(ah, bh, dims) + (_dg(ah, bl, dims) + _dg(al, bh, dims))


def _hdot_01(a, b, dims=NN):
    if a.dtype == BF16:
        hi, lo = _split(b)
        return _dg(a, hi, dims) + _dg(a, lo, dims)
    hi, lo = _split(a)
    return _dg(hi, b, dims) + _dg(lo, b, dims)


def _hdot(a, b, dims=NN):
    return _hdot_s(_split(a), _split(b), dims)


def _sigmoid(x):
    return 1.0 / (1.0 + jnp.exp(-x))


def _silu(x):
    return x * _sigmoid(x)


def _rms(x, g):
    return x * lax.rsqrt(jnp.mean(x * x, axis=-1, keepdims=True) + NORM_EPS) * g


def _const_spec(shape):
    zeros = (0,) * len(shape)
    return pl.BlockSpec(shape, lambda *_: zeros)


def _rope128(z, cos, sin, first):
    partner = jnp.where(first, pltpu.roll(z, LANES - HD // 2, 1), pltpu.roll(z, HD // 2, 1))
    return z * cos + partner * sin


def _proj_kernel(x_ref, g_ref, wa_ref, wb_ref, wc_ref, cos_ref, sin_ref,
                 q_ref, cmp_ref, sel_ref, win_ref, gate_ref, sa_ref, rz_ref, sr_ref, mg_ref, *aug_refs,
                 seq_major, n_pos_blocks):
    hb = _rms(x_ref[...], g_ref[...]).astype(BF16)
    cos = cos_ref[...]
    sin = sin_ref[...]
    tm = cos.shape[0]
    lane = lax.broadcasted_iota(jnp.int32, cos.shape, 1)
    first = (lane % HD) < (HD // 2)

    def mm(a, b):
        ref, off = (wa_ref, 0) if b <= C_AG else (wc_ref, C_NG) if a >= C_NG else (wb_ref, C_AG)
        return jnp.dot(hb, ref[:, a - off:b - off], preferred_element_type=F32)

    zq = mm(C_Q, C_Q + W_A)
    q = jnp.concatenate([_rope128(zq[:, c * LANES:(c + 1) * LANES], cos, sin, first) for c in range(W_A // LANES)],
                        axis=1)
    gates = _sigmoid(mm(C_NG, C_PACK))
    if seq_major:
        q_ref[0] = q.T
        gate_ref[0] = gates.T
    else:
        q_ref[...] = q
        gate_ref[...] = gates
    for i, ref in enumerate((cmp_ref, sel_ref, win_ref)):
        zkv = mm(C_KV + 2 * KV_W * i, C_KV + 2 * KV_W * (i + 1))
        k_rot = _rope128(zkv[:, 0:KV_W], cos, sin, first)
        if seq_major:
            ref[0] = jnp.concatenate([k_rot, zkv[:, KV_W:2 * KV_W]], axis=1).T
            if i > 0:
                aug_ref = aug_refs[i - 1]
                if i == 1:
                    t_in_seq = ((pl.program_id(0) % n_pos_blocks) * tm
                                + lax.broadcasted_iota(jnp.int32, cos.shape, 0))
                    tail = jnp.where(jnp.right_shift(t_in_seq, int(np.log2(L_SEL))) == lane - HD, 1.0, 0.0)
                else:
                    tail = jnp.zeros(cos.shape, F32)
                for g in range(NSA_GROUPS):
                    k_g = k_rot if g == 0 else pltpu.roll(k_rot, LANES - g * HD, 1)
                    aug_ref[0, g] = jnp.where(lane < HD, k_g, tail).astype(BF16)
        else:
            ref[:, 0:KV_W] = k_rot
            ref[:, KV_W:2 * KV_W] = zkv[:, KV_W:2 * KV_W]
    sa_ref[...] = _silu(mm(C_AG, C_AG + W_A))
    rz_ref[...] = mm(C_RZ, C_RZ + C_R)
    sr_ref[...] = _silu(mm(C_RG, C_RG + W_R))
    mg_ref[...] = _sigmoid(mm(C_MG, C_MG + 2 * D_MODEL))


def _project(x2d, norm_g, w_pack, cos_t, sin_t, tm, seq_major):
    n = x2d.shape[0]
    seq_len = cos_t.shape[0]
    n_pos_blocks = seq_len // tm
    n_seq = n // seq_len
    widths = (W_A, 2 * KV_W, 2 * KV_W, 2 * KV_W, LANES, W_A, C_R, W_R, 2 * D_MODEL)
    row = lambda w: pl.BlockSpec((tm, w), lambda i: (i, 0))
    pos = pl.BlockSpec((tm, LANES), lambda i: (i % n_pos_blocks, 0))
    out_specs = [row(w) for w in widths]
    out_shape = [jax.ShapeDtypeStruct((n, w), F32) for w in widths]
    in_specs = [row(D_MODEL), _const_spec((1, D_MODEL))] + [_const_spec(w.shape) for w in w_pack] + [pos, pos]
    args = [x2d, norm_g, *w_pack, cos_t, sin_t]
    if seq_major:
        assert seq_len // L_SEL <= LANES - HD, "selection-block one-hot must fit beside the key"
        for o in range(5):
            out_specs[o] = pl.BlockSpec((1, widths[o], tm), lambda i: (i // n_pos_blocks, 0, i % n_pos_blocks))
            out_shape[o] = jax.ShapeDtypeStruct((n_seq, widths[o], seq_len), F32)
        for _ in range(2):
            out_specs.append(pl.BlockSpec((1, NSA_GROUPS, tm, LANES),
                                          lambda i: (i // n_pos_blocks, 0, i % n_pos_blocks, 0)))
            out_shape.append(jax.ShapeDtypeStruct((n_seq, NSA_GROUPS, seq_len, LANES), BF16))
    return pl.pallas_call(
        functools.partial(_proj_kernel, seq_major=seq_major, n_pos_blocks=n_pos_blocks),
        grid=(n // tm,),
        in_specs=in_specs,
        out_specs=out_specs,
        out_shape=out_shape,
        compiler_params=pltpu.CompilerParams(dimension_semantics=("parallel",), vmem_limit_bytes=VMEM_LIMIT),
        name="project",
    )(*args)


def _merge_kernel(x_ref, oa_ref, ob_ref, sa_ref, sr_ref, mg_ref, p_ref,
                  wpa_ref, wpb_ref, wout_ref, pg_ref, wgate_ref, wproj_ref, fg_ref, y_ref):
    bdot = lambda a, w_ref: jnp.dot(a.astype(BF16), w_ref[...], preferred_element_type=F32)
    ya = bdot(oa_ref[...] * sa_ref[...], wpa_ref)
    yb = bdot(ob_ref[...] * sr_ref[...], wpb_ref)
    m = mg_ref[:, 0:D_MODEL] * ya + mg_ref[:, D_MODEL:2 * D_MODEL] * yb
    x2 = x_ref[...] + bdot(m, wout_ref)
    gate = _sigmoid(bdot(_rms(x2, pg_ref[...]), wgate_ref))
    x3 = x2 + gate * bdot(p_ref[...], wproj_ref)
    y_ref[...] = _rms(x3, fg_ref[...])


def _merge(x2d, oa, ob, sa, sr, mg, p2d, wpa, wpb, wout, pg, wgate, wproj, fg, tm):
    n = x2d.shape[0]
    row = lambda w: pl.BlockSpec((tm, w), lambda i: (i, 0))
    return pl.pallas_call(
        _merge_kernel,
        grid=(n // tm,),
        in_specs=[row(D_MODEL), row(W_A), row(W_R), row(W_A), row(W_R), row(2 * D_MODEL), row(D_PLE),
                  _const_spec((W_A, D_MODEL)), _const_spec((W_R, D_MODEL)), _const_spec((D_MODEL, D_MODEL)),
                  _const_spec((1, D_MODEL)), _const_spec((D_MODEL, D_MODEL)), _const_spec((D_PLE, D_MODEL)),
                  _const_spec((1, D_MODEL))],
        out_specs=row(D_MODEL),
        out_shape=jax.ShapeDtypeStruct((n, D_MODEL), F32),
        compiler_params=pltpu.CompilerParams(dimension_semantics=("parallel",), vmem_limit_bytes=VMEM_LIMIT),
        name="merge",
    )(x2d, oa, ob, sa, sr, mg, p2d, wpa, wpb, wout, pg, wgate, wproj, fg)


def _page_copies(pt_ref, pool_ref, buf, sem, b, t, slot, n_pages):
    return [pltpu.make_async_copy(pool_ref.at[pt_ref[b, t * n_pages + p]],
                                  buf.at[slot, :, pl.ds(p * PAGE_SIZE, PAGE_SIZE)], sem.at[slot])
            for p in range(n_pages)]


def _paged_tile(pt_ref, pool_ref, buf, sem, n_pages):
    b, t = pl.program_id(0), pl.program_id(1)
    nt = pl.num_programs(1)
    step = b * nt + t
    slot = step % 2

    @pl.when(step == 0)
    def _():
        for cp in _page_copies(pt_ref, pool_ref, buf, sem, b, t, slot, n_pages):
            cp.start()

    nxt = step + 1

    @pl.when(nxt < pl.num_programs(0) * nt)
    def _():
        for cp in _page_copies(pt_ref, pool_ref, buf, sem, nxt // nt, nxt % nt, 1 - slot, n_pages):
            cp.start()

    for cp in _page_copies(pt_ref, pool_ref, buf, sem, b, t, slot, n_pages):
        cp.wait()
    return slot


CMP_HALF = NSA_GROUPS * CMP_HID


def _compress_core(page_t, xs, perm_ref, w1_ref, pe_ref, b1_ref, w2_ref, b2_ref, carry, out_ref):
    n_pages = xs.shape[0]
    rows = n_pages * CHUNKS_PER_PAGE
    for p in range(n_pages):
        xp = _dg(perm_ref[...], page_t(p).astype(BF16), NT)
        xs[p] = xp.reshape(D_CMP, CHUNKS_PER_PAGE, 2 * KV_W)
    ridx = lax.broadcasted_iota(jnp.int32, (rows, CMP_HALF), 0)
    for kv in range(2):
        cs = slice(kv * KV_W, (kv + 1) * KV_W)
        x = jnp.concatenate([xs[:, j, :, cs].reshape(rows, KV_W) for j in range(D_CMP)], axis=1)
        x = jnp.concatenate([x, pe_ref[kv]], axis=0).astype(BF16)
        z = jnp.dot(x, w1_ref[kv], preferred_element_type=F32)
        pos = z[rows:rows + 1, 0:CMP_HALF] + z[rows + 1:rows + 2, CMP_HALF:]
        lo = z[0:rows, 0:CMP_HALF]
        lo_prev = jnp.where(ridx == 0, carry[kv], pltpu.roll(lo, 1, 0))
        carry[kv] = lo[rows - 1:rows]
        hid = lo_prev + z[0:rows, CMP_HALF:] + pos + b1_ref[kv]
        out_ref[0, :, cs] = jnp.dot(_silu(hid).astype(BF16), w2_ref[kv], preferred_element_type=F32) + b2_ref[kv]


def _compress_paged_kernel(pt_ref, pool_ref, perm_ref, w1_ref, pe_ref, b1_ref, w2_ref, b2_ref, out_ref,
                           buf, sem, xs, carry, *, n_pages):
    slot = _paged_tile(pt_ref, pool_ref, buf, sem, n_pages)

    @pl.when(pl.program_id(1) == 0)
    def _():
        carry[...] = jnp.zeros_like(carry)

    _compress_core(lambda p: buf[slot, :, p * PAGE_SIZE:(p + 1) * PAGE_SIZE], xs, perm_ref, w1_ref, pe_ref, b1_ref,
                   w2_ref, b2_ref, carry, out_ref)


def _compress_seq_kernel(x_ref, perm_ref, w1_ref, pe_ref, b1_ref, w2_ref, b2_ref, out_ref, xs, carry):
    carry[...] = jnp.zeros_like(carry)
    _compress_core(lambda p: x_ref[0, :, p * PAGE_SIZE:(p + 1) * PAGE_SIZE], xs, perm_ref, w1_ref, pe_ref, b1_ref,
                   w2_ref, b2_ref, carry, out_ref)


def _compress_specs(cst):
    return [cst((PAGE_SIZE, PAGE_SIZE)), cst((2, D_CMP * KV_W, 2 * CMP_HALF)), cst((2, 8, D_CMP * KV_W)),
            cst((2, 1, CMP_HALF)), cst((2, CMP_HALF, KV_W)), cst((2, 1, KV_W))]


def _compress_scratch(n_pages):
    return [pltpu.VMEM((n_pages, D_CMP, CHUNKS_PER_PAGE, 2 * KV_W), F32), pltpu.VMEM((2, 1, CMP_HALF), F32)]


def _compress_paged(page_table, pool, cw, n_pages):
    nb, pages_per_seq = page_table.shape
    nt = pages_per_seq // n_pages
    toks = n_pages * PAGE_SIZE
    rows = toks // D_CMP
    cst = lambda shape: pl.BlockSpec(shape, lambda b, t, pt: (0,) * len(shape))
    return pl.pallas_call(
        functools.partial(_compress_paged_kernel, n_pages=n_pages),
        grid_spec=pltpu.PrefetchScalarGridSpec(
            num_scalar_prefetch=1,
            grid=(nb, nt),
            in_specs=[pl.BlockSpec(memory_space=pl.ANY)] + _compress_specs(cst),
            out_specs=pl.BlockSpec((1, rows, 2 * KV_W), lambda b, t, pt: (b, t, 0)),
            scratch_shapes=[pltpu.VMEM((2, 2 * KV_W, toks), F32), pltpu.SemaphoreType.DMA((2,))]
            + _compress_scratch(n_pages)),
        out_shape=jax.ShapeDtypeStruct((nb, nt * rows, 2 * KV_W), F32),
        compiler_params=pltpu.CompilerParams(dimension_semantics=("arbitrary", "arbitrary"),
                                             vmem_limit_bytes=VMEM_LIMIT),
        name="compress_paged",
    )(page_table, pool, *cw)


def _compress_seq(x_t, cw):
    nb, _, toks = x_t.shape
    rows = toks // D_CMP
    return pl.pallas_call(
        _compress_seq_kernel,
        grid=(nb,),
        in_specs=[pl.BlockSpec((1, 2 * KV_W, toks), lambda b: (b, 0, 0))] + _compress_specs(_const_spec),
        out_specs=pl.BlockSpec((1, rows, 2 * KV_W), lambda b: (b, 0, 0)),
        out_shape=jax.ShapeDtypeStruct((nb, rows, 2 * KV_W), F32),
        scratch_shapes=_compress_scratch(toks // PAGE_SIZE),
        compiler_params=pltpu.CompilerParams(dimension_semantics=("parallel",), vmem_limit_bytes=VMEM_LIMIT),
        name="compress_seq",
    )(x_t, *cw)


def _compress_weights(cmp_pe, cmp_w1, cmp_b1, cmp_w2, cmp_b2):
    eye_g = jnp.eye(NSA_GROUPS, dtype=F32)
    w1 = cmp_w1.reshape(2, 2, D_CMP, HD, CMP_HID)
    w1j = jnp.einsum('kljdf,gG->kjgdlGf', w1, eye_g).reshape(2, D_CMP * KV_W, 2 * CMP_HALF)
    pe = cmp_pe.reshape(2, 2, D_CMP, HD)
    pe = jnp.broadcast_to(pe[:, :, :, None, :], (2, 2, D_CMP, NSA_GROUPS, HD))
    pej = jnp.pad(pe.reshape(2, 2, D_CMP * KV_W), ((0, 0), (0, 6), (0, 0)))
    b1big = jnp.broadcast_to(cmp_b1[:, None, :], (2, NSA_GROUPS, CMP_HID)).reshape(2, 1, CMP_HALF)
    w2big = jnp.einsum('kfd,gG->kgfGd', cmp_w2, eye_g).reshape(2, CMP_HALF, KV_W)
    b2big = jnp.broadcast_to(cmp_b2[:, None, :], (2, NSA_GROUPS, HD)).reshape(2, 1, KV_W)
    dst = np.arange(PAGE_SIZE)
    perm = (dst[:, None] % CHUNKS_PER_PAGE) * D_CMP + dst[:, None] // CHUNKS_PER_PAGE == dst[None, :]
    return jnp.asarray(perm, dtype=BF16), w1j.astype(BF16), pej, b1big, w2big.astype(BF16), b2big


def _overlap_matrix(n_rows, n_cols):
    c_start = (np.arange(n_rows) - 1) * D_CMP
    s_start = np.arange(n_cols) * L_SEL
    ov = (c_start[:, None] <= s_start[None, :] + L_SEL - 1) & (c_start[:, None] + L_CMP - 1 >= s_start[None, :])
    ov &= (np.arange(n_rows) >= 1)[:, None]
    return ov.astype(np.float32)


def _online(state, s, msk, v_bf, v_dims=NN):
    m, l, acc = state
    if msk is not None:
        s = jnp.where(msk, s, NEG_INF)
    m_new = jnp.maximum(m, jnp.max(s, axis=-1, keepdims=True))
    alpha = jnp.exp(m - m_new)
    p = jnp.exp(s - m_new)
    if msk is not None:
        p = jnp.where(msk, p, 0.0)
    l = alpha * l + jnp.sum(p, axis=-1, keepdims=True)
    acc = alpha * acc + _dg(p.astype(BF16), v_bf, v_dims)
    return m_new, l, acc


def _online_init(rows):
    return jnp.full((rows, 1), NEG_INF, F32), jnp.zeros((rows, 1), F32), jnp.zeros((rows, HD), F32)


def _online_out(state):
    _, l, acc = state
    return acc / jnp.maximum(l, 1e-30)


def _stack_heads(q, g):
    return jnp.concatenate([q[:, (HPG * g + h) * HD:(HPG * g + h + 1) * HD] for h in range(HPG)], axis=0) * (HD ** -0.5)


def _compressed_branch(qs, kc, vc, valid, n_tok):
    s = jnp.where(valid, _hdot(qs, kc, NT), NEG_INF)
    m = jnp.max(s, axis=-1, keepdims=True)
    p = jnp.where(valid, jnp.exp(s - m), 0.0)
    p = p / jnp.maximum(jnp.sum(p, axis=-1, keepdims=True), 1e-30)
    o = jnp.dot(p.astype(BF16), vc.astype(BF16), preferred_element_type=F32)
    psum = p[0:n_tok]
    for h in range(1, HPG):
        psum = psum + p[h * n_tok:(h + 1) * n_tok]
    return o, psum


def _select_blocks(imp, tpos, n_real):
    blk = lax.broadcasted_iota(jnp.int32, imp.shape, 1)
    cur = tpos // L_SEL
    forced = (blk == 0) | (blk == cur) | (blk == cur - 1)
    imp = jnp.where(forced, FORCE_SCORE, imp)
    imp = jnp.where(blk * L_SEL > tpos, -1.0, imp)
    imp = jnp.where(blk >= n_real, -2.0, imp)
    slabs = [imp[:, l0:l0 + LANES] for l0 in range(0, imp.shape[1], LANES)]
    cnt = [jnp.zeros(s.shape, F32) for s in slabs]
    lane = lax.broadcasted_iota(jnp.int32, slabs[0].shape, 1)
    for mcol in range(n_real):
        col = jnp.broadcast_to(imp[:, mcol:mcol + 1], slabs[0].shape)
        for si, slab in enumerate(slabs):
            ge = jnp.where(col >= slab, 1.0, 0.0)
            gt = jnp.where(col > slab, 1.0, 0.0)
            if si * LANES > mcol:
                ahead = ge
            elif si * LANES + LANES - 1 < mcol:
                ahead = gt
            else:
                ahead = jnp.where(lane > mcol - si * LANES, ge, gt)
            cnt[si] = cnt[si] + ahead
    return jnp.where(jnp.concatenate(cnt, axis=1) < N_SEL, 1.0, 0.0)


def _select_blocks_t(imp_t, tpos_row, n_real, n_live, cnt_ref):
    sub = 8
    blk = lax.broadcasted_iota(jnp.int32, imp_t.shape, 0)
    cur = jnp.right_shift(tpos_row, int(np.log2(L_SEL)))
    forced = (blk == 0) | (blk == cur) | (blk == cur - 1)
    imp_t = jnp.where(forced, FORCE_SCORE, imp_t)
    imp_t = jnp.where(blk * L_SEL > tpos_row, -1.0, imp_t)
    slabs = [imp_t[s0:s0 + sub] for s0 in range(0, n_real, sub)]
    sub_iota = lax.broadcasted_iota(jnp.int32, slabs[0].shape, 0)
    cnt_ref[...] = jnp.zeros(cnt_ref.shape, F32)
    for m0 in range(0, n_real, sub):

        @pl.when(m0 < n_live)
        def _():
            cnt = [cnt_ref[si * sub:(si + 1) * sub] for si in range(len(slabs))]
            for m in range(m0, min(m0 + sub, n_real)):
                row = jnp.broadcast_to(imp_t[m:m + 1], slabs[0].shape)
                for si, slab in enumerate(slabs):
                    ge = jnp.where(row >= slab, 1.0, 0.0)
                    gt = jnp.where(row > slab, 1.0, 0.0)
                    if si * sub > m:
                        ahead = ge
                    elif si * sub + sub - 1 < m:
                        ahead = gt
                    else:
                        ahead = jnp.where(sub_iota > m - si * sub, ge, gt)
                    cnt[si] = cnt[si] + ahead
            for si in range(len(slabs)):
                cnt_ref[si * sub:(si + 1) * sub] = cnt[si]

    return jnp.where(cnt_ref[...] < N_SEL, 1.0, 0.0)


ACC_ROWS = HD + 16


def _with_ones(v_t):
    return jnp.concatenate([v_t, jnp.ones((ACC_ROWS - HD, v_t.shape[1]), BF16)], axis=0)


def _sweep_t(units, states):
    states = list(states)
    s_t = {0: units[0][1]()}
    pending = None

    def finish(k, alpha, p, values):
        m, acc = states[k]
        states[k] = (m, alpha * acc + _dg(values(), p))

    for u, (k, _, msk, values) in enumerate(units):
        if u + 1 < len(units):
            s_t[u + 1] = units[u + 1][1]()
        m, acc = states[k]
        s = s_t.pop(u)
        if msk is not None:
            s = jnp.where(msk, s, NEG_INF)
        m_new = jnp.maximum(m, jnp.max(s, axis=0, keepdims=True))
        alpha = jnp.exp(m - m_new)
        p = jnp.exp(s - m_new)
        states[k] = (m_new, acc)
        if pending is not None:
            finish(*pending)
        pending = (k, alpha, p.astype(BF16), values)
    finish(*pending)
    return tuple(states)


def _sweep_t_init(cols):
    return jnp.full((1, cols), NEG_INF, F32), jnp.zeros((ACC_ROWS, cols), F32)


def _sweep_t_out(state):
    _, acc = state
    return acc[0:HD] / jnp.maximum(acc[HD:HD + 1], 1e-30)


def _nsa_prompt_kernel(qt_ref, gatet_ref, kc_ref, ksel_ref, vsel_ref, kwin_ref, vwin_ref, ovt_ref, o_ref, cnt_ref,
                       *, tq, tk):
    i = pl.program_id(1)
    n_cmp = kc_ref.shape[1]
    n_sb = ovt_ref.shape[0]
    cols = HPG * tq
    aug = LANES - HD
    t0 = i * tq
    tpos_row = t0 + lax.broadcasted_iota(jnp.int32, (1, tq), 1)
    tpos4 = jnp.concatenate([tpos_row] * HPG, axis=1)
    n_full = t0 // tk
    w_start = pl.multiple_of(jnp.maximum(t0 - WIN, 0), tq)
    gates_t = gatet_ref[0]
    groups = range(NSA_GROUPS)
    n_w = WIN + tq
    dist = tpos4 - (w_start + lax.broadcasted_iota(jnp.int32, (n_w, 1), 0))
    ridx = lax.broadcasted_iota(jnp.int32, (n_cmp, cols), 0)
    valid = (ridx >= 1) & (D_CMP * ridx + (L_CMP - D_CMP - 1) <= tpos4)
    q_t = [jnp.concatenate([qt_ref[0, (HPG * g + h) * HD:(HPG * g + h + 1) * HD, :] for h in range(HPG)], axis=1)
           * (HD ** -0.5) for g in groups]
    s_c = [jnp.where(valid, _hdot(kc_ref[0, :, g * HD:(g + 1) * HD], q_t[g]), NEG_INF) for g in groups]
    p_c = [jnp.where(valid, jnp.exp(s - jnp.max(s, axis=0, keepdims=True)), 0.0) for s in s_c]
    p_c = [p / jnp.maximum(jnp.sum(p, axis=0, keepdims=True), 1e-30) for p in p_c]
    o_c = [_dg(kc_ref[0, :, KV_W + g * HD:KV_W + (g + 1) * HD].astype(BF16), p_c[g].astype(BF16), TN) for g in groups]
    psum = jnp.concatenate([sum(p[:, h * tq:(h + 1) * tq] for h in range(HPG)) for p in p_c], axis=1)
    sel_t = _select_blocks_t(_hdot_01(ovt_ref[...], psum), jnp.concatenate([tpos_row] * NSA_GROUPS, axis=1), n_sb,
                             (t0 + tq - 1) // L_SEL + 1, cnt_ref)
    qa = []
    for g in groups:
        bias = jnp.concatenate([(sel_t[:, g * tq:(g + 1) * tq] - 1.0) * (-NEG_INF)] * HPG, axis=1)
        if n_sb < aug:
            bias = jnp.concatenate([bias, jnp.zeros((aug - n_sb, cols), F32)], axis=0)
        qa.append(jnp.concatenate([q_t[g], bias], axis=0).astype(BF16))

    def sel_unit(g, j, causal):
        k0 = pl.multiple_of(j * tk, tk)
        kpos = k0 + lax.broadcasted_iota(jnp.int32, (tk, 1), 0)
        return (g, lambda: _dg(ksel_ref[0, g, pl.ds(k0, tk), :], qa[g]), (kpos <= tpos4) if causal else None,
                lambda: _with_ones(vsel_ref[0, g * HD:(g + 1) * HD, pl.ds(k0, tk)].astype(BF16)))

    def win_unit(g):
        return (NSA_GROUPS + g, lambda: _dg(kwin_ref[0, g, pl.ds(w_start, n_w), :], qa[g]),
                (dist >= 0) & (dist < WIN),
                lambda: _with_ones(vwin_ref[0, g * HD:(g + 1) * HD, pl.ds(w_start, n_w)].astype(BF16)))

    states = lax.fori_loop(
        0, n_full // 2,
        lambda jj, st: _sweep_t([sel_unit(g, 2 * jj + d, False) for d in range(2) for g in groups], st),
        tuple(_sweep_t_init(cols) for g in groups))
    last = lambda: [u for g in groups for u in (sel_unit(g, n_full, True), win_unit(g))]
    states = lax.cond(n_full % 2 == 1,
                      lambda st: _sweep_t([sel_unit(g, n_full - 1, False) for g in groups] + last(), st),
                      lambda st: _sweep_t(last(), st),
                      states + tuple(_sweep_t_init(cols) for g in groups))
    heads_out = []
    for g in groups:
        o_s = _sweep_t_out(states[g])
        o_w = _sweep_t_out(states[NSA_GROUPS + g])
        for h in range(HPG):
            hd = HPG * g + h
            c = slice(h * tq, (h + 1) * tq)
            heads_out.append(gates_t[3 * hd:3 * hd + 1] * o_c[g][:, c] + gates_t[3 * hd + 1:3 * hd + 2] * o_s[:, c]
                             + gates_t[3 * hd + 2:3 * hd + 3] * o_w[:, c])
    o_ref[0] = jnp.concatenate(heads_out, axis=0).T


def _nsa_prompt(q_t, gates_t, kc, ksel, sel_t, kwin, win_t, tq, tk):
    b, _, t = q_t.shape
    n_cmp = kc.shape[1]
    n_sb = t // L_SEL
    ovt = jnp.asarray(_overlap_matrix(n_cmp, n_sb).T, dtype=BF16)
    keys = pl.BlockSpec((1, NSA_GROUPS, t, LANES), lambda bi, i: (bi, 0, 0, 0))
    values = pl.BlockSpec((1, KV_W, t), lambda bi, i: (bi, 1, 0))
    return pl.pallas_call(
        functools.partial(_nsa_prompt_kernel, tq=tq, tk=tk),
        grid=(b, t // tq),
        in_specs=[pl.BlockSpec((1, W_A, tq), lambda bi, i: (bi, 0, i)),
                  pl.BlockSpec((1, LANES, tq), lambda bi, i: (bi, 0, i)),
                  pl.BlockSpec((1, n_cmp, 2 * KV_W), lambda bi, i: (bi, 0, 0)),
                  keys, values, keys, values, _const_spec((n_sb, n_cmp))],
        out_specs=pl.BlockSpec((1, tq, W_A), lambda bi, i: (bi, i, 0)),
        out_shape=jax.ShapeDtypeStruct((b, t, W_A), F32),
        scratch_shapes=[pltpu.VMEM((n_sb, NSA_GROUPS * tq), F32)],
        compiler_params=pltpu.CompilerParams(dimension_semantics=("parallel", "arbitrary"),
                                             vmem_limit_bytes=VMEM_LIMIT),
        name="nsa_prompt",
    )(q_t, gates_t, kc, ksel, sel_t, kwin, win_t, ovt)


def _nsa_sample_kernel(pt_ref, q_ref, gate_ref, kc_ref, pool_ref, snew_ref, wcache_ref, wnew_ref, ov_ref, ex_ref,
                       o_ref, buf, sem, sel_scr, oc_scr, m_scr, l_scr, acc_scr, *, n_pages, past):
    slot = _paged_tile(pt_ref, pool_ref, buf, sem, n_pages)
    t = pl.program_id(1)
    nt = pl.num_programs(1)
    ds = q_ref.shape[1]
    rows = HPG * ds
    n_cmp = kc_ref.shape[1]
    n_sbp = ov_ref.shape[1]
    blocks_per_tile = n_pages * PAGE_SIZE // L_SEL
    n_past_blocks = past // L_SEL
    keys = n_pages * PAGE_SIZE
    q = q_ref[0]
    tok4 = lax.broadcasted_iota(jnp.int32, (rows, 1), 0) % ds

    @pl.when(t == 0)
    def _():
        tpos = past + lax.broadcasted_iota(jnp.int32, (ds, 1), 0)
        ridx = lax.broadcasted_iota(jnp.int32, (rows, n_cmp), 1)
        valid = (ridx >= 1) & (D_CMP * ridx + (L_CMP - D_CMP - 1) <= past + tok4)
        psum = []
        for g in range(NSA_GROUPS):
            o_c, ps = _compressed_branch(_stack_heads(q, g), kc_ref[0, :, g * HD:(g + 1) * HD],
                                         kc_ref[0, :, KV_W + g * HD:KV_W + (g + 1) * HD], valid, ds)
            oc_scr[g] = o_c
            psum.append(ps)
            m_scr[g], l_scr[g], acc_scr[g] = _online_init(rows)
        sel = _select_blocks(_hdot_01(jnp.concatenate(psum, axis=0), ov_ref[...]),
                             jnp.concatenate([tpos] * NSA_GROUPS, axis=0), n_past_blocks + 1)
        for g in range(NSA_GROUPS):
            sel4 = jnp.concatenate([sel[g * ds:(g + 1) * ds]] * HPG, axis=0)
            for tt in range(sel_scr.shape[0]):
                sel_scr[tt, g] = sel4[:, tt * LANES:(tt + 1) * LANES]

    kv_t = buf[slot]
    first_blk = t * blocks_per_tile
    ex = ex_ref[(first_blk % LANES) // blocks_per_tile]
    zero = jnp.zeros((rows, HD), F32)
    q_rows = jnp.concatenate([jnp.concatenate([_stack_heads(q, 0), zero], axis=1),
                              jnp.concatenate([zero, _stack_heads(q, 1)], axis=1)], axis=0)
    bias = jnp.concatenate([(sel_scr[first_blk // LANES, g] - 1.0) * (-NEG_INF) for g in range(NSA_GROUPS)], axis=0)
    s_all = _dg(jnp.concatenate([q_rows, bias], axis=1).astype(BF16),
                jnp.concatenate([kv_t[0:KV_W].astype(BF16), ex], axis=0))
    for g in range(NSA_GROUPS):
        vt = kv_t[KV_W + g * HD:KV_W + (g + 1) * HD].astype(BF16)
        m_scr[g], l_scr[g], acc_scr[g] = _online((m_scr[g], l_scr[g], acc_scr[g]), s_all[g * rows:(g + 1) * rows],
                                                 None, vt, NT)

    @pl.when(t == nt - 1)
    def _():
        gates = gate_ref[0]
        knew_pos = lax.broadcasted_iota(jnp.int32, (rows, ds), 1)
        snew = snew_ref[0]
        wnew = wnew_ref[0]
        wc_t = wcache_ref[0]
        n_wc = wc_t.shape[1]
        for g in range(NSA_GROUPS):
            qsb = _stack_heads(q, g).astype(BF16)
            kcol = slice(g * HD, (g + 1) * HD)
            vcol = slice(KV_W + g * HD, KV_W + (g + 1) * HD)
            sel_new = sel_scr[n_past_blocks // LANES, g][:, n_past_blocks % LANES:n_past_blocks % LANES + 1]
            st = _online((m_scr[g], l_scr[g], acc_scr[g]), _dg(qsb, snew[:, kcol].astype(BF16), NT),
                         (sel_new > 0.5) & (knew_pos <= tok4), snew[:, vcol].astype(BF16))
            o_s = _online_out(st)
            dist_c = (n_wc + tok4) - lax.broadcasted_iota(jnp.int32, (rows, n_wc), 1)
            st = _online(_online_init(rows), _dg(qsb, wc_t[kcol].astype(BF16)),
                         (dist_c >= 0) & (dist_c < WIN), wc_t[vcol].astype(BF16), NT)
            dist_n = tok4 - knew_pos
            st = _online(st, _dg(qsb, wnew[:, kcol].astype(BF16), NT), (dist_n >= 0) & (dist_n < WIN),
                         wnew[:, vcol].astype(BF16))
            o_w = _online_out(st)
            o_c = oc_scr[g]
            for h in range(HPG):
                hd = HPG * g + h
                r = slice(h * ds, (h + 1) * ds)
                o_ref[0, :, hd * HD:(hd + 1) * HD] = (gates[:, 3 * hd:3 * hd + 1] * o_c[r]
                                                      + gates[:, 3 * hd + 1:3 * hd + 2] * o_s[r]
                                                      + gates[:, 3 * hd + 2:3 * hd + 3] * o_w[r])


def _nsa_sample(page_table, q, gates, kc, pool, sel_new, win_cache, win_new, n_pages):
    nb, ds = q.shape[:2]
    pages_per_seq = page_table.shape[1]
    past = pages_per_seq * PAGE_SIZE
    nt = pages_per_seq // n_pages
    n_cmp = kc.shape[1]
    n_sb = past // L_SEL + 1
    n_sbp = -(-n_sb // LANES) * LANES
    blocks_per_tile = n_pages * PAGE_SIZE // L_SEL
    keys = n_pages * PAGE_SIZE
    rows = HPG * ds
    ov = jnp.asarray(_overlap_matrix(n_cmp, n_sbp) * (np.arange(n_sbp) < n_sb)[None, :], dtype=BF16)
    ex = (np.arange(LANES)[None, :, None]
          == (np.arange(LANES // blocks_per_tile)[:, None, None] * blocks_per_tile + np.arange(keys)[None, None, :] // L_SEL))
    ex = jnp.asarray(ex, dtype=BF16)
    seq = lambda r, w: pl.BlockSpec((1, r, w), lambda b, t, pt: (b, 0, 0))
    cst = lambda shape: pl.BlockSpec(shape, lambda b, t, pt: (0,) * len(shape))
    return pl.pallas_call(
        functools.partial(_nsa_sample_kernel, n_pages=n_pages, past=past),
        grid_spec=pltpu.PrefetchScalarGridSpec(
            num_scalar_prefetch=1,
            grid=(nb, nt),
            in_specs=[seq(ds, W_A), seq(ds, LANES), seq(n_cmp, 2 * KV_W), pl.BlockSpec(memory_space=pl.ANY),
                      seq(ds, 2 * KV_W), seq(2 * KV_W, win_cache.shape[2]), seq(ds, 2 * KV_W),
                      cst((n_cmp, n_sbp)), cst(ex.shape)],
            out_specs=seq(ds, W_A),
            scratch_shapes=[pltpu.VMEM((2, 2 * KV_W, keys), F32),
                            pltpu.SemaphoreType.DMA((2,)),
                            pltpu.VMEM((n_sbp // LANES, NSA_GROUPS, rows, LANES), F32),
                            pltpu.VMEM((NSA_GROUPS, rows, HD), F32),
                            pltpu.VMEM((NSA_GROUPS, rows, 1), F32),
                            pltpu.VMEM((NSA_GROUPS, rows, 1), F32),
                            pltpu.VMEM((NSA_GROUPS, rows, HD), F32)]),
        out_shape=jax.ShapeDtypeStruct((nb, ds, W_A), F32),
        compiler_params=pltpu.CompilerParams(dimension_semantics=("arbitrary", "arbitrary"),
                                             vmem_limit_bytes=VMEM_LIMIT),
        name="nsa_sample",
    )(page_table, q, gates, kc, pool, sel_new, win_cache, win_new, ov, ex)


def _sum_heads(x, ones_ref):
    return _hdot_01(x, ones_ref[...])


def _rwkv_prep_kernel(rz_ref, before_ref, shift_ref, mu_ref, w0_ref, wup_ref, a0_ref, aup_ref, kk_ref, ka_ref, rk_ref,
                      gnb_ref, ones_ref, a_ref, o_ref, y1_ref, y2_ref, yadd_ref, *, chunk, t_valid, separate):
    rows = rz_ref.shape[1]
    n_chunks = rows // chunk
    shift = int(np.log2(chunk))
    z = rz_ref[0]
    ridx = lax.broadcasted_iota(jnp.int32, (rows, 1), 0)
    prev = pltpu.roll(z, 1, 0)
    if separate:
        for c in range(n_chunks):
            prev = jnp.where(ridx == c * chunk, shift_ref[0, c:c + 1], prev)
    else:
        n_before = before_ref.shape[1]
        row_before = jnp.where(pl.program_id(1) == 0, shift_ref[0], before_ref[0, n_before - 1:n_before])
        prev = jnp.where(ridx == 0, row_before, prev)
    zs = z + (prev - z) * mu_ref[...]
    r = zs[:, 0:W_R]
    k = zs[:, W_R:2 * W_R]
    v = zs[:, 2 * W_R:3 * W_R]
    wd = zs[:, 3 * W_R:3 * W_R + R_W]
    ad = zs[:, 3 * W_R + R_W:C_R]
    wpre = w0_ref[...] + _hdot(jnp.tanh(wd), wup_ref[...])
    w = -(jnp.maximum(-wpre, 0.0) + jnp.log(1.0 + jnp.exp(-jnp.abs(wpre)))) - 0.5
    ld = -jnp.exp(w)
    a = _sigmoid(a0_ref[...] + _hdot(ad, aup_ref[...]))
    kkn = k * kk_ref[...]
    kk = kkn / jnp.maximum(jnp.sqrt(_sum_heads(kkn * kkn, ones_ref)), 1e-12)
    k2 = k * (1.0 + (a - 1.0) * ka_ref[...])
    yadd_ref[0] = _sum_heads(r * k2 * rk_ref[...], ones_ref) * v + gnb_ref[...]
    if t_valid < chunk:
        live = (ridx & (chunk - 1)) < t_valid
        ld = jnp.where(live, ld, 0.0)
        kk = jnp.where(live, kk, 0.0)
        k2 = jnp.where(live, k2, 0.0)
    ti = lax.broadcasted_iota(jnp.int32, (rows, rows), 0)
    tj = lax.broadcasted_iota(jnp.int32, (rows, rows), 1)
    tri = jnp.where((jnp.right_shift(ti, shift) == jnp.right_shift(tj, shift)) & (tj <= ti), 1.0, 0.0).astype(BF16)
    ld_h = ld.astype(BF16)
    ld_m = (ld - ld_h.astype(F32)).astype(BF16)
    ld_l = (ld - ld_h.astype(F32) - ld_m.astype(F32)).astype(BF16)
    cum = _dg(tri, ld_h) + (_dg(tri, ld_m) + _dg(tri, ld_l))
    cum_end = jnp.concatenate([jnp.broadcast_to(cum[(c + 1) * chunk - 1:(c + 1) * chunk], (chunk, W_R))
                               for c in range(n_chunks)], axis=0)
    p_inv = jnp.exp(-cum)
    p_end = jnp.exp(cum_end - cum)
    kka = kk * a
    at = -kk * jnp.exp(cum - ld)
    rt = r * jnp.exp(cum)
    at_b, rt_b, v_b = at.astype(BF16), rt.astype(BF16), v.astype(BF16)
    bt_b, kt_b = (kka * p_inv).astype(BF16), (k2 * p_inv).astype(BF16)
    be_b, ke_b = (kka * p_end).astype(BF16), (k2 * p_end).astype(BF16)
    dec_end = jnp.exp(cum_end)

    row = lax.broadcasted_iota(jnp.int32, (chunk, PAIR_W), 0)
    col = lax.broadcasted_iota(jnp.int32, (chunk, PAIR_W), 1) & (R_HD - 1)
    strict, incl = col < row, col <= row
    eye_pair = jnp.where(col == row, 1.0, 0.0)
    n_levels = max(1, int(np.ceil(np.log2(chunk))))
    units = [(c, p) for c in range(n_chunks) for p in range(R_HEADS // 2)]
    idx = range(len(units))
    cut = lambda x, u: x[u[0] * chunk:(u[0] + 1) * chunk, u[1] * PAIR_W:(u[1] + 1) * PAIR_W]
    mm = [_dg(jnp.concatenate([cut(at_b, u), cut(rt_b, u)], axis=0),
              jnp.concatenate([_pair_diag(cut(bt_b, u)), _pair_diag(cut(kt_b, u))], axis=0), NT) for u in units]
    low = [jnp.where(strict, m[0:chunk, 0:PAIR_W], 0.0) for m in mm]
    m_k = [jnp.concatenate([jnp.where(strict, m[0:chunk, PAIR_W:], 0.0), jnp.where(incl, m[chunk:, PAIR_W:], 0.0)],
                           axis=0).astype(BF16) for m in mm]
    m_rb = [jnp.where(incl, m[chunk:, 0:PAIR_W], 0.0).astype(BF16) for m in mm]
    mv = [_dg(m_k[i], _pair_diag(cut(v_b, u))) for i, u in enumerate(units)]
    inv = [eye_pair + x for x in low]
    pw = [x.astype(BF16) for x in low]
    for lvl in range(1, n_levels + 1):
        if lvl == 1:
            pw = [_dg(x, _pair_diag(x)).astype(BF16) for x in pw]
        elif lvl <= n_levels:
            last = lvl == n_levels
            nxt = [_dg(pw[i], _pair_diag(inv[i].astype(BF16)) if last else
                       jnp.concatenate([_pair_diag(inv[i].astype(BF16)), _pair_diag(pw[i])], axis=1)) for i in idx]
            inv = [inv[i] + nxt[i][:, 0:PAIR_W] for i in idx]
            if not last:
                pw = [nxt[i][:, PAIR_W:].astype(BF16) for i in idx]
    inv_b = [x.astype(BF16) for x in inv]
    w = [_dg(inv_b[i], jnp.concatenate([_pair_diag(cut(at_b, u)), _pair_diag(mv[i][0:chunk].astype(BF16))], axis=1))
         for i, u in enumerate(units)]
    w_b = [x.astype(BF16) for x in w]
    y12 = [_dg(m_rb[i], jnp.concatenate([_pair_diag(w_b[i][:, 0:PAIR_W]), _pair_diag(w_b[i][:, PAIR_W:])], axis=1))
           for i in idx]
    prow = lax.broadcasted_iota(jnp.int32, (PAIR_W, PAIR_W), 0)
    pcol = lax.broadcasted_iota(jnp.int32, (PAIR_W, PAIR_W), 1)
    same_head = (prow < R_HD) == (pcol < R_HD)
    col_first = lax.broadcasted_iota(jnp.int32, (R_HD, PAIR_W), 1) < R_HD
    for i, u in enumerate(units):
        c, p = u
        rs = slice(c * chunk, (c + 1) * chunk)
        ls = slice(p * PAIR_W, (p + 1) * PAIR_W)
        a_mat = jnp.where(same_head, _dg(w_b[i][:, 0:PAIR_W], cut(be_b, u), TN), 0.0)
        a_mat = a_mat + jnp.where(prow == pcol, jnp.broadcast_to(dec_end[rs, ls][0:1], (PAIR_W, PAIR_W)), 0.0)
        a_ref[0, c, p] = a_mat.astype(BF16)
        o_full = _dg(jnp.concatenate([w_b[i][:, PAIR_W:], cut(v_b, u)], axis=0),
                     jnp.concatenate([cut(be_b, u), cut(ke_b, u)], axis=0), TN)
        o_ref[0, c, p] = jnp.where(col_first, o_full[0:R_HD], o_full[R_HD:])
        y1_ref[0, rs, ls] = (y12[i][:, 0:PAIR_W] + cut(rt, u)).astype(BF16)
        y2_ref[0, rs, ls] = y12[i][:, PAIR_W:] + mv[i][chunk:]


PAIR_W = 2 * R_HD


def _pair_diag(x):
    first = jnp.where(lax.broadcasted_iota(jnp.int32, x.shape, 1) < R_HD, 1.0, 0.0).astype(x.dtype)
    return jnp.concatenate([x * first, x * (1 - first)], axis=0)


def _rwkv_scan_kernel(a_ref, o_ref, y1_ref, y2_ref, yadd_ref, s0_ref, gng_ref, ones_ref, y_ref, st, sout_ref):
    units = [(b, p) for b in range(st.shape[0]) for p in range(R_HEADS // 2)]

    @pl.when(pl.program_id(1) == 0)
    def _():
        for b, p in units:
            st[b, p] = jnp.concatenate([s0_ref[b, 2 * p], s0_ref[b, 2 * p + 1]], axis=1)

    lsl = lambda p: slice(p * PAIR_W, (p + 1) * PAIR_W)
    s_b = [st[b, p].astype(BF16) for b, p in units]
    ys = [_dg(y1_ref[b, :, lsl(p)], _pair_diag(s_b[i]), NT) + y2_ref[b, :, lsl(p)] for i, (b, p) in enumerate(units)]
    for i, (b, p) in enumerate(units):
        st[b, p] = _dg(s_b[i], a_ref[b, 0, p]) + o_ref[b, 0, p]
    head_mean = lambda x: _dg(x.astype(BF16), ones_ref[...]) * (1.0 / R_HD)
    mean = [head_mean(y) for y in ys]
    dev = [ys[i] - mean[i] for i in range(len(units))]
    var = [head_mean(d * d) for d in dev]
    for i, (b, p) in enumerate(units):
        y_ref[b, :, lsl(p)] = dev[i] * lax.rsqrt(var[i] + GN_EPS) * gng_ref[:, lsl(p)] + yadd_ref[b, :, lsl(p)]

    for b, p in units:
        s = st[b, p]
        sout_ref[b, 2 * p] = s[:, 0:R_HD]
        sout_ref[b, 2 * p + 1] = s[:, R_HD:]


def _rwkv(rz, shift0, state0, params, chunk, chunks_per_step, seqs_per_step, t_valid):
    mu, w0, wup, a0, aup, k_k, k_a, r_k, gn_g, gn_b = params
    b, t = rz.shape[:2]
    assert chunk == R_HD, "pair packing puts the chunk's time index on a head's 64 lanes"
    rows = chunk * chunks_per_step
    n_chunks = t // chunk
    n_pairs = R_HEADS // 2
    n_before = 8
    separate = t == chunk and chunks_per_step > 1
    pb, pt = (b // chunks_per_step, rows) if separate else (b, t)
    pc = pt // chunk
    ones = jnp.asarray(np.kron(np.eye(R_HEADS), np.ones((R_HD, R_HD))), dtype=BF16)
    vec = lambda n: _const_spec((1, n))
    rowsd = lambda dt: jax.ShapeDtypeStruct((b, t, W_R), dt)
    prowsd = lambda dt: jax.ShapeDtypeStruct((pb, pt, W_R), dt)
    row_spec = pl.BlockSpec((1, rows, W_R), lambda bi, c: (bi, c, 0))
    rz_p = rz.reshape(pb, pt, C_R)
    a_m, o_m, y1, y2, yadd = pl.pallas_call(
        functools.partial(_rwkv_prep_kernel, chunk=chunk, t_valid=t_valid, separate=separate),
        grid=(pb, pt // rows),
        in_specs=[pl.BlockSpec((1, rows, C_R), lambda bi, c: (bi, c, 0)),
                  pl.BlockSpec((1, n_before, C_R),
                               lambda bi, c: (bi, jnp.maximum(c * (rows // n_before) - 1, 0), 0)),
                  pl.BlockSpec((1, b // pb, C_R), lambda bi, c: (bi, 0, 0)),
                  vec(C_R), vec(W_R), _const_spec((R_W, W_R)), vec(W_R), _const_spec((R_A, W_R)),
                  vec(W_R), vec(W_R), vec(W_R), vec(W_R), _const_spec((W_R, W_R))],
        out_specs=[pl.BlockSpec((1, chunks_per_step, n_pairs, PAIR_W, PAIR_W), lambda bi, c: (bi, c, 0, 0, 0)),
                   pl.BlockSpec((1, chunks_per_step, n_pairs, R_HD, PAIR_W), lambda bi, c: (bi, c, 0, 0, 0)),
                   row_spec, row_spec, row_spec],
        out_shape=[jax.ShapeDtypeStruct((pb, pc, n_pairs, PAIR_W, PAIR_W), BF16),
                   jax.ShapeDtypeStruct((pb, pc, n_pairs, R_HD, PAIR_W), F32),
                   prowsd(BF16), prowsd(F32), prowsd(F32)],
        compiler_params=pltpu.CompilerParams(dimension_semantics=("parallel", "parallel"),
                                             vmem_limit_bytes=VMEM_LIMIT),
        name="rwkv_prep",
    )(rz_p, rz_p, shift0.reshape(pb, b // pb, C_R), mu, w0, wup, a0, aup, k_k, k_a, r_k, gn_b, ones)
    a_m = a_m.reshape(b, n_chunks, n_pairs, PAIR_W, PAIR_W)
    o_m = o_m.reshape(b, n_chunks, n_pairs, R_HD, PAIR_W)
    y1, y2, yadd = (u.reshape(b, t, W_R) for u in (y1, y2, yadd))
    nb = seqs_per_step
    row1 = pl.BlockSpec((nb, chunk, W_R), lambda bi, c: (bi, c, 0))
    pair_state = pl.BlockSpec((nb, n_pairs, R_HD, PAIR_W), lambda bi, c: (bi, 0, 0, 0))
    head_state = pl.BlockSpec((nb, R_HEADS, R_HD, R_HD), lambda bi, c: (bi, 0, 0, 0))
    y, _, state = pl.pallas_call(
        _rwkv_scan_kernel,
        grid=(b // nb, n_chunks),
        in_specs=[pl.BlockSpec((nb, 1, n_pairs, PAIR_W, PAIR_W), lambda bi, c: (bi, c, 0, 0, 0)),
                  pl.BlockSpec((nb, 1, n_pairs, R_HD, PAIR_W), lambda bi, c: (bi, c, 0, 0, 0)),
                  row1, row1, row1, head_state, vec(W_R), _const_spec((PAIR_W, PAIR_W))],
        out_specs=[row1, pair_state, head_state],
        out_shape=[rowsd(F32), jax.ShapeDtypeStruct((b, n_pairs, R_HD, PAIR_W), F32),
                   jax.ShapeDtypeStruct((b, R_HEADS, R_HD, R_HD), F32)],
        compiler_params=pltpu.CompilerParams(dimension_semantics=("parallel", "arbitrary"),
                                             vmem_limit_bytes=VMEM_LIMIT),
        name="rwkv_scan",
    )(a_m, o_m, y1, y2, yadd, state0, gn_g, ones[:PAIR_W, :PAIR_W])
    return y, state


def _rope_tables(pos):
    half = HD // 2
    inv = ROPE_THETA ** (-jnp.arange(half, dtype=F32) / half)
    ang = pos.astype(F32)[:, None] * inv[None, :]
    cos, sin = jnp.cos(ang), jnp.sin(ang)
    reps = LANES // HD
    return (jnp.tile(jnp.concatenate([cos, cos], axis=-1), (1, reps)),
            jnp.tile(jnp.concatenate([-sin, sin], axis=-1), (1, reps)))


def _pick_tile(n, cap):
    t = cap
    while n % t:
        t //= 2
    return t


def kernel(x_prompt, x_sample, p_prompt, p_sample, cache_cmp_kv, cache_sel_kv, cache_win_kv, state_wkv, state_shift, page_table, norm_g, w_in, cmp_pe, cmp_w1, cmp_b1, cmp_w2, cmp_b2, w_pa, rwkv_mu, rwkv_w0, rwkv_w_up, rwkv_a0, rwkv_a_up, rwkv_k_k, rwkv_k_a, rwkv_r_k, rwkv_gn_g, rwkv_gn_b, w_pb, w_out, ple_norm_g, w_ple_gate, w_ple_proj, final_norm_g):
    b, s_len = x_prompt.shape[:2]
    db, ds = x_sample.shape[:2]
    depth = norm_g.shape[0]
    assert depth == 1, "single-layer trunk"
    n_pages = page_table.shape[1]
    past = n_pages * PAGE_SIZE
    i = 0

    w = w_in[i]
    o_q, o_ng, o_ag = 0, W_A + 6 * KV_W, W_A + 6 * KV_W + 3 * NSA_HEADS
    w_pack = (w[:, o_q:o_ng].astype(BF16), w[:, o_ag:].astype(BF16),
              jnp.pad(w[:, o_ng:o_ag], ((0, 0), (0, C_PACK - w.shape[1]))).astype(BF16))
    cw = _compress_weights(cmp_pe[i], cmp_w1[i], cmp_b1[i], cmp_w2[i], cmp_b2[i])
    row = lambda u: u.reshape(1, -1)
    rw = (row(rwkv_mu[i]), row(rwkv_w0[i]), rwkv_w_up[i], row(rwkv_a0[i]), rwkv_a_up[i], row(rwkv_k_k[i]),
          row(rwkv_k_a[i]), row(rwkv_r_k[i]), row(rwkv_gn_g[i]), row(rwkv_gn_b[i]))
    ow = (w_pa[i].astype(BF16), w_pb[i].astype(BF16), w_out[i].astype(BF16), row(ple_norm_g[i]),
          w_ple_gate[i].astype(BF16), w_ple_proj[i].astype(BF16), row(final_norm_g))

    tm = _pick_tile(s_len, 256)
    cos_p, sin_p = _rope_tables(jnp.arange(s_len, dtype=jnp.int32))
    xp2 = x_prompt.reshape(b * s_len, D_MODEL)
    q, cmp_p, sel_p, win_p, gates, sa, rz, sr, mg, ksel, kwin = _project(xp2, row(norm_g[i]), w_pack, cos_p, sin_p,
                                                                        tm, True)
    kc_p = _compress_seq(cmp_p, cw)
    seq = lambda u: u.reshape(b, s_len, u.shape[-1])
    o_a = _nsa_prompt(q, gates, kc_p, ksel, sel_p, kwin, win_p, tq=128, tk=min(512, s_len))
    chunk = 64
    o_b, st_p = _rwkv(seq(rz), jnp.zeros((b, C_R), F32), jnp.zeros((b, R_HEADS, R_HD, R_HD), F32), rw, chunk,
                      chunks_per_step=4, seqs_per_step=_pick_tile(b, 4), t_valid=chunk)
    y_p = _merge(xp2, o_a.reshape(b * s_len, W_A), o_b.reshape(b * s_len, W_R), sa, sr, mg,
                 p_prompt[i].reshape(b * s_len, D_PLE), *ow, tm=_pick_tile(b * s_len, 512))
    kv_shape = lambda u, n, t: u.reshape(n, t, 2, NSA_GROUPS, HD)
    from_cm = lambda u: jnp.transpose(u.reshape(u.shape[0], 2, NSA_GROUPS, HD, u.shape[2]), (0, 4, 1, 2, 3))
    to_cm = lambda u: jnp.transpose(u, (0, 2, 3, 4, 1)).reshape(u.shape[0], 2 * KV_W, u.shape[1])
    cmp_kv_p = from_cm(cmp_p)
    sel_kv_p = from_cm(sel_p)
    win_kv_p = from_cm(win_p[:, :, s_len - min(WIN, s_len):])
    shift_p = seq(rz)[:, -1]

    n_s = db * ds
    cos_s, sin_s = _rope_tables(jnp.tile(past + jnp.arange(ds, dtype=jnp.int32), db))
    xs2 = x_sample.reshape(n_s, D_MODEL)
    q, cmp_s, sel_s, win_s, gates, sa, rz, sr, mg = _project(xs2, row(norm_g[i]), w_pack, cos_s, sin_s, n_s, False)
    kc_s = _compress_paged(page_table, to_cm(cache_cmp_kv[i]), cw, n_pages=_pick_tile(n_pages, 64))
    seqs = lambda u: u.reshape(db, ds, u.shape[-1])
    o_a = _nsa_sample(page_table, seqs(q), seqs(gates), kc_s, to_cm(cache_sel_kv[i]),
                      seqs(sel_s), to_cm(cache_win_kv[i]), seqs(win_s), n_pages=_pick_tile(n_pages, 64))
    pad_t = chunk
    rz_pad = jnp.pad(seqs(rz), ((0, 0), (0, pad_t - ds), (0, 0)))
    o_b, st_s = _rwkv(rz_pad, state_shift[i], state_wkv[i], rw, pad_t, chunks_per_step=_pick_tile(db, 4),
                      seqs_per_step=_pick_tile(db, 4), t_valid=ds)
    y_s = _merge(xs2, o_a.reshape(n_s, W_A), o_b[:, :ds].reshape(n_s, W_R), sa, sr, mg,
                 p_sample[i].reshape(n_s, D_PLE), *ow, tm=n_s)
    win_kv_s = jnp.concatenate([cache_win_kv[i], kv_shape(win_s, db, ds)], axis=1)[:, ds:]
    shift_s = seqs(rz)[:, -1]

    return (y_p.reshape(b, s_len, D_MODEL), y_s.reshape(db, ds, D_MODEL),
            cmp_kv_p[None], kv_shape(cmp_s, db, ds)[None], sel_kv_p[None], kv_shape(sel_s, db, ds)[None],
            win_kv_p[None], win_kv_s[None], st_p[None], st_s[None], shift_p[None], shift_s[None])
```
